```python
import jax, jax.numpy as jnp
from jax import lax
import numpy as np

D_MODEL = 1024
BATCH = 8
SEQ = 2048
DEPTH = 4
DEC_BATCH = 32
DEC_SEQ = 4
PAST_LEN = 8192
PAGE_SIZE = 128

NSA_HEADS = 8
NSA_KV_HEADS = 2
NSA_GROUP = NSA_HEADS // NSA_KV_HEADS
NSA_HD = 64
CMP_LEN = 32
CMP_STRIDE = 16
CMP_HIDDEN = 128
SEL_LEN = 64
N_SEL = 8
WINDOW = 512
Q_BLOCK = 128
ROPE_DIMS = NSA_HD // 4
ROPE_THETA = 500000.0
ML_HEADS = 4
ML_HD = 128
ML_W = ML_HEADS * ML_HD
ML_CHUNK = 64
CONV_W = 4
XA_HEADS = 4
XA_HD = 64
N_MEM = 256
D_FF = 4 * D_MODEL
EPS = 1e-6
NSA_Q_W = NSA_HEADS * NSA_HD
NSA_KV_W = 6 * NSA_KV_HEADS * NSA_HD
NSA_G_W = 3 * NSA_HEADS
COL_SPLITS = (NSA_Q_W, NSA_KV_W, NSA_G_W, 2 * ML_W, ML_W, ML_W, 2 * ML_HEADS)
IN_W = sum(COL_SPLITS)
MIX_W = NSA_Q_W + ML_W

kernel_name = 'nsa_mlstm_hymba_decode_step'


def rms_norm(x, g):
    xf = x.astype(jnp.float32)
    y = xf * lax.rsqrt(jnp.mean(xf * xf, axis=-1, keepdims=True) + EPS)
    return (y * g.astype(jnp.float32)).astype(x.dtype)


def rope(x, pos):
    half = ROPE_DIMS // 2
    freqs = ROPE_THETA ** (-jnp.arange(half, dtype=jnp.float32) / half)
    ang = pos.astype(jnp.float32)[:, None] * freqs
    shape = (ang.shape[0],) + (1,) * (x.ndim - 3) + (half,)
    cos, sin = jnp.cos(ang).reshape(shape), jnp.sin(ang).reshape(shape)
    xf = x.astype(jnp.float32)
    x1, x2 = xf[..., :half], xf[..., half:ROPE_DIMS]
    out = jnp.concatenate([x1 * cos - x2 * sin, x2 * cos + x1 * sin, xf[..., ROPE_DIMS:]], axis=-1)
    return out.astype(x.dtype)


def masked_softmax(s, mask):
    s = jnp.where(mask, s.astype(jnp.float32), -jnp.inf)
    m = jnp.max(s, axis=-1, keepdims=True)
    m = jnp.where(jnp.isfinite(m), m, 0.0)
    e = jnp.exp(s - m)
    den = jnp.sum(e, axis=-1, keepdims=True)
    return e / jnp.where(den > 0, den, 1.0)


def split_cols(z):
    idx = np.cumsum(COL_SPLITS)[:-1].tolist()
    return jnp.split(z, idx, axis=-1)


def nsa_prepare(z_q, z_kv, z_g, pos, qn_g, kn_g):
    B, T = z_q.shape[:2]
    q = rope(rms_norm(z_q.reshape(B, T, NSA_HEADS, NSA_HD), qn_g), pos)
    kv = z_kv.reshape(B, T, 3, 2, NSA_KV_HEADS, NSA_HD)
    k = rope(rms_norm(kv[:, :, :, 0], kn_g[:, None, :]), pos)
    kvp = jnp.stack([k, kv[:, :, :, 1]], axis=3)
    gates = jax.nn.sigmoid(z_g.reshape(B, T, NSA_HEADS, 3).astype(jnp.float32))
    return q, kvp, gates


def compress(x, pos_emb, w1, w2):
    B, T, G, D = x.shape
    nc = (T - CMP_LEN) // CMP_STRIDE + 1
    r = CMP_LEN // CMP_STRIDE
    sub = x[:, :(nc + r - 1) * CMP_STRIDE].reshape(B, nc + r - 1, CMP_STRIDE, G, D)
    blocks = jnp.concatenate([sub[:, j:j + nc] for j in range(r)], axis=2)
    blocks = blocks + pos_emb[:, None, :]
    flat = blocks.transpose(0, 1, 3, 2, 4).reshape(B, nc, G, CMP_LEN * D)
    return jax.nn.silu(flat @ w1) @ w2


def sel_blocks(x):
    B, T, G, D = x.shape
    ns = -(-T // SEL_LEN)
    x = jnp.pad(x, ((0, 0), (0, ns * SEL_LEN - T), (0, 0), (0, 0)))
    return x.reshape(B, ns, SEL_LEN, G, D).transpose(0, 3, 1, 2, 4)


def nsa_attend(q, q_pos, gates, kc, vc, c_end, kb, vb, kw, vw, w_pos):
    B, Tq = q.shape[:2]
    G, HG, D = NSA_KV_HEADS, NSA_GROUP, NSA_HD
    scale = D ** -0.5
    qg = q.reshape(B, Tq, G, HG, D)
    s_c = jnp.einsum('btghd,bcgd->btghc', qg, kc) * scale
    m_c = (c_end[None, :] <= q_pos[:, None])[None, :, None, None, :]
    p_c = masked_softmax(s_c, m_c)
    o_c = jnp.einsum('btghc,bcgd->btghd', p_c.astype(vc.dtype), vc)
    NC, NS = kc.shape[1], kb.shape[2]
    ci = jnp.arange(NC)[:, None] * CMP_STRIDE
    sj = jnp.arange(NS)[None, :]
    cover = ((ci < (sj + 1) * SEL_LEN) & (ci + CMP_LEN > sj * SEL_LEN)).astype(jnp.float32)
    imp = jnp.einsum('btghc,cs->btgs', p_c, cover)
    cur = (q_pos // SEL_LEN)[:, None]
    forced = ((sj == cur) | (sj == 0))[None, :, None, :]
    future = (sj > cur)[None, :, None, :]
    imp = jnp.where(forced, jnp.inf, jnp.where(future, -jnp.inf, imp))
    n_top = min(N_SEL, NS)
    top_v, top_i = lax.top_k(imp, n_top)
    gather = jax.vmap(jax.vmap(lambda blk, ix: blk[ix]))
    idx = top_i.transpose(0, 2, 1, 3)
    ks = gather(kb, idx).reshape(B, G, Tq, n_top * SEL_LEN, D)
    vs = gather(vb, idx).reshape(B, G, Tq, n_top * SEL_LEN, D)
    kpos = (top_i[..., None] * SEL_LEN + jnp.arange(SEL_LEN)).reshape(B, Tq, G, n_top * SEL_LEN)
    valid = jnp.repeat(top_v > -jnp.inf, SEL_LEN, axis=-1)
    m_s = (valid & (kpos <= q_pos[None, :, None, None]))[:, :, :, None, :]
    s_s = jnp.einsum('btghd,bgtkd->btghk', qg, ks) * scale
    p_s = masked_softmax(s_s, m_s)
    o_s = jnp.einsum('btghk,bgtkd->btghd', p_s.astype(vs.dtype), vs)
    dpos = q_pos[:, None] - w_pos[None, :]
    m_w = ((dpos >= 0) & (dpos <= WINDOW) & (w_pos[None, :] >= 0))[None, :, None, None, :]
    s_w = jnp.einsum('btghd,bwgd->btghw', qg, kw) * scale
    p_w = masked_softmax(s_w, m_w)
    o_w = jnp.einsum('btghw,bwgd->btghd', p_w.astype(vw.dtype), vw)
    g = gates.reshape(B, Tq, G, HG, 3)
    o = g[..., 0:1] * o_c + g[..., 1:2] * o_s + g[..., 2:3] * o_w
    return o.reshape(B, Tq, G * HG * D).astype(q.dtype)


def nsa_prompt(q, kvp, gates, cmp_pos, cmp_w1, cmp_w2):
    B, T = q.shape[:2]
    kc = compress(kvp[:, :, 0, 0], cmp_pos[0], cmp_w1[0], cmp_w2[0])
    vc = compress(kvp[:, :, 0, 1], cmp_pos[1], cmp_w1[1], cmp_w2[1])
    c_end = jnp.arange(kc.shape[1], dtype=jnp.int32) * CMP_STRIDE + CMP_LEN - 1
    kb, vb = sel_blocks(kvp[:, :, 1, 0]), sel_blocks(kvp[:, :, 1, 1])
    pad = ((0, 0), (WINDOW, 0), (0, 0), (0, 0))
    kw, vw = jnp.pad(kvp[:, :, 2, 0], pad), jnp.pad(kvp[:, :, 2, 1], pad)
    span = WINDOW + Q_BLOCK

    def block(i):
        q0 = i * Q_BLOCK
        sl = lambda a, n: lax.dynamic_slice_in_dim(a, q0, n, axis=1)
        q_pos = q0 + jnp.arange(Q_BLOCK, dtype=jnp.int32)
        w_pos = q0 - WINDOW + jnp.arange(span, dtype=jnp.int32)
        return nsa_attend(sl(q, Q_BLOCK), q_pos, sl(gates, Q_BLOCK), kc, vc, c_end, kb, vb,
                          sl(kw, span), sl(vw, span), w_pos)

    o = lax.map(block, jnp.arange(T // Q_BLOCK, dtype=jnp.int32))
    return jnp.moveaxis(o, 0, 1).reshape(B, T, -1)


def nsa_sample(q, kvp, gates, pos, past_rows, win_buf, cmp_pos, cmp_w1, cmp_w2):
    DB, DS = q.shape[:2]
    new_rows = kvp[:, :, :2].reshape(DB, DS, 4, NSA_KV_HEADS, NSA_HD)
    full = jnp.concatenate([past_rows, new_rows], axis=1)
    kc = compress(full[:, :, 0], cmp_pos[0], cmp_w1[0], cmp_w2[0])
    vc = compress(full[:, :, 1], cmp_pos[1], cmp_w1[1], cmp_w2[1])
    c_end = jnp.arange(kc.shape[1], dtype=jnp.int32) * CMP_STRIDE + CMP_LEN - 1
    kb, vb = sel_blocks(full[:, :, 2]), sel_blocks(full[:, :, 3])
    WB = win_buf.shape[1]
    wfull = jnp.concatenate([win_buf, kvp[:, :, 2]], axis=1)
    w_pos = pos[0] - WB + jnp.arange(WB + DS, dtype=jnp.int32)
    o = nsa_attend(q, pos, gates, kc, vc, c_end, kb, vb, wfull[:, :, 0], wfull[:, :, 1], w_pos)
    return o, new_rows, wfull[:, -WB:]


def causal_conv(x, buf, w, b):
    T = x.shape[1]
    xp = jnp.concatenate([buf, x], axis=1)
    y = sum(xp[:, j:j + T] * w[j] for j in range(CONV_W)) + b
    return y, xp[:, -(CONV_W - 1):]


def mlstm_prepare(z_qk, z_v, z_if, conv_buf, conv_w, conv_b, gate_b):
    B, T = z_qk.shape[:2]
    qk, new_buf = causal_conv(z_qk, conv_buf, conv_w, conv_b)
    qk = jax.nn.silu(qk).astype(jnp.float32)
    heads = lambda a: a.reshape(B, T, ML_HEADS, ML_HD).transpose(0, 2, 1, 3)
    q = heads(qk[..., :ML_W])
    k = heads(qk[..., ML_W:]) * (ML_HD ** -0.5)
    v = heads(z_v.astype(jnp.float32))
    g = (z_if.astype(jnp.float32) + gate_b.astype(jnp.float32)).transpose(0, 2, 1)
    li = g[:, :ML_HEADS]
    lf = jax.nn.log_sigmoid(g[:, ML_HEADS:])
    return (q, k, v, li, lf), new_buf


def mlstm_chunk(state, inp):
    C, n, m = state
    q, k, v, li, lf = inp
    L = q.shape[2]
    b = jnp.cumsum(lf, axis=-1)
    causal = jnp.tril(jnp.ones((L, L), dtype=bool))
    Dm = jnp.where(causal, b[..., :, None] - b[..., None, :] + li[..., None, :], -jnp.inf)
    inter = b + m[..., None]
    m_t = jnp.maximum(inter, jnp.max(Dm, axis=-1))
    W = jnp.exp(Dm - m_t[..., None])
    a = jnp.exp(inter - m_t)
    Wqk = W * jnp.einsum('bhtd,bhsd->bhts', q, k)
    num = a[..., None] * jnp.einsum('bhvk,bhtk->bhtv', C, q) + jnp.einsum('bhts,bhsv->bhtv', Wqk, v)
    den = a * jnp.einsum('bhk,bhtk->bht', n, q) + jnp.sum(Wqk, axis=-1)
    h = num / jnp.maximum(jnp.abs(den), jnp.exp(-m_t))[..., None]
    m_new = m_t[..., -1]
    wgt = jnp.exp(b[..., -1:] - b + li - m_new[..., None])
    decay = jnp.exp(b[..., -1] + m - m_new)
    C_new = decay[..., None, None] * C + jnp.einsum('bhs,bhsv,bhsk->bhvk', wgt, v, k)
    n_new = decay[..., None] * n + jnp.einsum('bhs,bhsk->bhk', wgt, k)
    return (C_new, n_new, m_new), h


def mlstm_sequence(inp, state):
    T = inp[0].shape[2]
    nch = T // ML_CHUNK

    def chunks(a):
        a = a.reshape(a.shape[:2] + (nch, ML_CHUNK) + a.shape[3:])
        return jnp.moveaxis(a, 2, 0)

    state, h = lax.scan(mlstm_chunk, state, tuple(chunks(a) for a in inp))
    h = jnp.moveaxis(h, 0, 2)
    return state, h.reshape(h.shape[:2] + (T, ML_HD))


def mlstm_output(h, z_o, hn_g):
    B, NH, T, DV = h.shape
    h = rms_norm(h.transpose(0, 2, 1, 3), hn_g).reshape(B, T, NH * DV)
    return (h * jax.nn.sigmoid(z_o.astype(jnp.float32))).astype(z_o.dtype)


def memory_kv(mem, g, wkv, kn_g):
    B, M, _ = mem.shape
    z = (rms_norm(mem, g) @ wkv).reshape(B, M, 2, XA_HEADS, XA_HD)
    return jnp.stack([rms_norm(z[:, :, 0], kn_g), z[:, :, 1]], axis=2)


def cross_attn(xn, mkv, wq, qn_g, wo):
    B, T, _ = xn.shape
    q = rms_norm((xn @ wq).reshape(B, T, XA_HEADS, XA_HD), qn_g)
    s = jnp.einsum('bthd,bmhd->bhtm', q, mkv[:, :, 0]).astype(jnp.float32) * (XA_HD ** -0.5)
    p = jax.nn.softmax(s, axis=-1).astype(xn.dtype)
    o = jnp.einsum('bhtm,bmhd->bthd', p, mkv[:, :, 1])
    return o.reshape(B, T, XA_HEADS * XA_HD) @ wo


def sq_relu_mlp(xn, w1, w2):
    return jnp.square(jax.nn.relu(xn @ w1)) @ w2


def setup_inputs(seed: int = 0) -> dict:
    key = jax.random.key(seed)
    ks = iter(jax.random.split(key, 48))
    f32 = jnp.float32

    def nrm(shape, scale):
        return jax.random.normal(next(ks), shape, f32) * scale

    n_pages = PAST_LEN // PAGE_SIZE
    n_pool = (DEC_BATCH * n_pages * 5) // 4
    wbuf = min(WINDOW, PAST_LEN)
    G, D = NSA_KV_HEADS, NSA_HD
    x_prompt = nrm((BATCH, SEQ, D_MODEL), 1.0)
    x_sample = nrm((DEC_BATCH, DEC_SEQ, D_MODEL), 1.0)
    mem_prompt = nrm((BATCH, N_MEM, D_MODEL), 1.0)
    cache_nsa_kv = nrm((n_pool, PAGE_SIZE, DEPTH, 4, G, D), 1.0)
    cache_win_kv = nrm((DEPTH, DEC_BATCH, wbuf, 2, G, D), 1.0)
    cache_mem_kv = nrm((DEPTH, DEC_BATCH, N_MEM, 2, XA_HEADS, XA_HD), 1.0)
    state_mlstm_C = nrm((DEPTH, DEC_BATCH, ML_HEADS, ML_HD, ML_HD), 0.1)
    state_mlstm_n = nrm((DEPTH, DEC_BATCH, ML_HEADS, ML_HD), 0.1)
    state_mlstm_m = nrm((DEPTH, DEC_BATCH, ML_HEADS), 0.5)
    state_conv = nrm((DEPTH, DEC_BATCH, CONV_W - 1, 2 * ML_W), 1.0)
    perm = jax.random.permutation(next(ks), n_pool)
    page_table = perm[:DEC_BATCH * n_pages].reshape(DEC_BATCH, n_pages).astype(jnp.int32)
    ml_gate_b = jnp.concatenate([nrm((DEPTH, ML_HEADS), 0.1), 3.0 + nrm((DEPTH, ML_HEADS), 0.5)], axis=-1)
    return {
        'x_prompt': x_prompt,
        'x_sample': x_sample,
        'mem_prompt': mem_prompt,
        'cache_nsa_kv': cache_nsa_kv,
        'cache_win_kv': cache_win_kv,
        'cache_mem_kv': cache_mem_kv,
        'state_mlstm_C': state_mlstm_C,
        'state_mlstm_n': state_mlstm_n,
        'state_mlstm_m': state_mlstm_m,
        'state_conv': state_conv,
        'page_table': page_table,
        'norm_mix_g': 1.0 + nrm((DEPTH, D_MODEL), 0.05),
        'w_in': nrm((DEPTH, D_MODEL, IN_W), D_MODEL ** -0.5),
        'nsa_qn_g': 1.0 + nrm((DEPTH, NSA_HD), 0.05),
        'nsa_kn_g': 1.0 + nrm((DEPTH, 3, NSA_HD), 0.05),
        'cmp_pos': nrm((DEPTH, 2, CMP_LEN, NSA_HD), 0.1),
        'cmp_w1': nrm((DEPTH, 2, CMP_LEN * NSA_HD, CMP_HIDDEN), (CMP_LEN * NSA_HD) ** -0.5),
        'cmp_w2': nrm((DEPTH, 2, CMP_HIDDEN, NSA_HD), CMP_HIDDEN ** -0.5),
        'ml_conv_w': nrm((DEPTH, CONV_W, 2 * ML_W), CONV_W ** -0.5),
        'ml_conv_b': nrm((DEPTH, 2 * ML_W), 0.02),
        'ml_gate_b': ml_gate_b,
        'ml_hn_g': 1.0 + nrm((DEPTH, ML_HD), 0.05),
        'w_out': nrm((DEPTH, MIX_W, D_MODEL), MIX_W ** -0.5),
        'norm_xa_g': 1.0 + nrm((DEPTH, D_MODEL), 0.05),
        'norm_mem_g': 1.0 + nrm((DEPTH, D_MODEL), 0.05),
        'xa_wq': nrm((DEPTH, D_MODEL, XA_HEADS * XA_HD), D_MODEL ** -0.5),
        'xa_wkv': nrm((DEPTH, D_MODEL, 2 * XA_HEADS * XA_HD), D_MODEL ** -0.5),
        'xa_qn_g': 1.0 + nrm((DEPTH, XA_HD), 0.05),
        'xa_kn_g': 1.0 + nrm((DEPTH, XA_HD), 0.05),
        'xa_wo': nrm((DEPTH, XA_HEADS * XA_HD, D_MODEL), (XA_HEADS * XA_HD) ** -0.5),
        'norm_mlp_g': 1.0 + nrm((DEPTH, D_MODEL), 0.05),
        'mlp_w1': nrm((DEPTH, D_MODEL, D_FF), D_MODEL ** -0.5),
        'mlp_w2': nrm((DEPTH, D_FF, D_MODEL), D_FF ** -0.5),
    }


def reference(x_prompt, x_sample, mem_prompt, cache_nsa_kv, cache_win_kv, cache_mem_kv,
              state_mlstm_C, state_mlstm_n, state_mlstm_m, state_conv, page_table,
              norm_mix_g, w_in, nsa_qn_g, nsa_kn_g, cmp_pos, cmp_w1, cmp_w2,
              ml_conv_w, ml_conv_b, ml_gate_b, ml_hn_g, w_out,
              norm_xa_g, norm_mem_g, xa_wq, xa_wkv, xa_qn_g, xa_kn_g, xa_wo,
              norm_mlp_g, mlp_w1, mlp_w2):
    f32 = jnp.float32
    B, T, _ = x_prompt.shape
    DB, DS, _ = x_sample.shape
    P = page_table.shape[1] * cache_nsa_kv.shape[1]
    pos_p = jnp.arange(T, dtype=jnp.int32)
    pos_s = P + jnp.arange(DS, dtype=jnp.int32)
    wkeep = min(WINDOW, T)
    xp, xs = x_prompt, x_sample
    nsa_p, nsa_s, win_p, win_s = [], [], [], []
    C_p, C_s, n_p, n_s, m_p, m_s = [], [], [], [], [], []
    cv_p, cv_s, mem_p = [], [], []
    for l in range(DEPTH):
        zq_p, zkv_p, zg_p, zqk_p, zv_p, zo_p, zif_p = split_cols(rms_norm(xp, norm_mix_g[l]) @ w_in[l])
        zq_s, zkv_s, zg_s, zqk_s, zv_s, zo_s, zif_s = split_cols(rms_norm(xs, norm_mix_g[l]) @ w_in[l])
        q, kvp, gt = nsa_prepare(zq_p, zkv_p, zg_p, pos_p, nsa_qn_g[l], nsa_kn_g[l])
        a_p = nsa_prompt(q, kvp, gt, cmp_pos[l], cmp_w1[l], cmp_w2[l])
        nsa_p.append(kvp[:, :, :2].reshape(B, T, 4, NSA_KV_HEADS, NSA_HD))
        win_p.append(kvp[:, T - wkeep:, 2])
        q, kvp, gt = nsa_prepare(zq_s, zkv_s, zg_s, pos_s, nsa_qn_g[l], nsa_kn_g[l])
        past = cache_nsa_kv[page_table, :, l].reshape(DB, P, 4, NSA_KV_HEADS, NSA_HD)
        a_s, rows_s, wbuf_s = nsa_sample(q, kvp, gt, pos_s, past, cache_win_kv[l], cmp_pos[l], cmp_w1[l], cmp_w2[l])
        nsa_s.append(rows_s)
        win_s.append(wbuf_s)
        inp, buf_p = mlstm_prepare(zqk_p, zv_p, zif_p, jnp.zeros((B, CONV_W - 1, 2 * ML_W), xp.dtype),
                                   ml_conv_w[l], ml_conv_b[l], ml_gate_b[l])
        st0 = (jnp.zeros((B, ML_HEADS, ML_HD, ML_HD), f32), jnp.zeros((B, ML_HEADS, ML_HD), f32),
               jnp.zeros((B, ML_HEADS), f32))
        (Cn, nn, mn), h = mlstm_sequence(inp, st0)
        b_p = mlstm_output(h, zo_p, ml_hn_g[l])
        C_p.append(Cn); n_p.append(nn); m_p.append(mn); cv_p.append(buf_p)
        inp, buf_s = mlstm_prepare(zqk_s, zv_s, zif_s, state_conv[l], ml_conv_w[l], ml_conv_b[l], ml_gate_b[l])
        st = (state_mlstm_C[l].astype(f32), state_mlstm_n[l].astype(f32), state_mlstm_m[l].astype(f32))
        (Cn, nn, mn), h = mlstm_chunk(st, inp)
        b_s = mlstm_output(h, zo_s, ml_hn_g[l])
        C_s.append(Cn); n_s.append(nn); m_s.append(mn); cv_s.append(buf_s)
        xp = xp + jnp.concatenate([a_p, b_p], axis=-1) @ w_out[l]
        xs = xs + jnp.concatenate([a_s, b_s], axis=-1) @ w_out[l]
        mkv_p = memory_kv(mem_prompt, norm_mem_g[l], xa_wkv[l], xa_kn_g[l])
        mem_p.append(mkv_p)
        xp = xp + cross_attn(rms_norm(xp, norm_xa_g[l]), mkv_p, xa_wq[l], xa_qn_g[l], xa_wo[l])
        xs = xs + cross_attn(rms_norm(xs, norm_xa_g[l]), cache_mem_kv[l], xa_wq[l], xa_qn_g[l], xa_wo[l])
        xp = xp + sq_relu_mlp(rms_norm(xp, norm_mlp_g[l]), mlp_w1[l], mlp_w2[l])
        xs = xs + sq_relu_mlp(rms_norm(xs, norm_mlp_g[l]), mlp_w1[l], mlp_w2[l])
    dt = xp.dtype
    return (xp, xs,
            jnp.stack(nsa_p, axis=2), jnp.stack(nsa_s, axis=2),
            jnp.stack(win_p), jnp.stack(win_s),
            jnp.stack(C_p).astype(dt), jnp.stack(C_s).astype(dt),
            jnp.stack(n_p).astype(dt), jnp.stack(n_s).astype(dt),
            jnp.stack(m_p).astype(dt), jnp.stack(m_s).astype(dt),
            jnp.stack(cv_p), jnp.stack(cv_s),
            jnp.stack(mem_p))
```

```python
import functools

import jax
import jax.numpy as jnp
import numpy as np
from jax import lax
from jax.experimental import pallas as pl
from jax.experimental.pallas import tpu as pltpu

import reference as _r

EPS = 1e-6
VMEM_LIMIT = 56 * 1024 * 1024


def _mlp_kernel(x_ref, g_ref, w1_ref, w2_ref, o_ref, xn_ref, acc_ref):
    f = pl.program_id(1)

    @pl.when(f == 0)
    def _():
        x = x_ref[...]
        y = x * lax.rsqrt(jnp.mean(x * x, axis=-1, keepdims=True) + EPS)
        xn_ref[...] = (y * g_ref[...]).astype(jnp.bfloat16)
        acc_ref[...] = x

    h = jnp.dot(xn_ref[...], w1_ref[...], preferred_element_type=jnp.float32)
    h = jnp.square(jnp.maximum(h, 0.0)).astype(jnp.bfloat16)
    acc_ref[...] += jnp.dot(h, w2_ref[...], preferred_element_type=jnp.float32)

    @pl.when(f == pl.num_programs(1) - 1)
    def _():
        o_ref[...] = acc_ref[...]


def mlp_block(x, g, w1b, w2b, tm, tf=512):
    M, D = x.shape
    FF = w1b.shape[1]
    return pl.pallas_call(
        _mlp_kernel,
        grid=(M // tm, FF // tf),
        in_specs=[
            pl.BlockSpec((tm, D), lambda i, f: (i, 0)),
            pl.BlockSpec((1, D), lambda i, f: (0, 0)),
            pl.BlockSpec((D, tf), lambda i, f: (0, f)),
            pl.BlockSpec((tf, D), lambda i, f: (f, 0)),
        ],
        out_specs=pl.BlockSpec((tm, D), lambda i, f: (i, 0)),
        out_shape=jax.ShapeDtypeStruct((M, D), jnp.float32),
        scratch_shapes=[pltpu.VMEM((tm, D), jnp.bfloat16), pltpu.VMEM((tm, D), jnp.float32)],
        compiler_params=pltpu.CompilerParams(
            dimension_semantics=("parallel", "arbitrary"), vmem_limit_bytes=VMEM_LIMIT),
    )(x, g.reshape(1, D), w1b, w2b)


def kernel(x_prompt, x_sample, mem_prompt, cache_nsa_kv, cache_win_kv, cache_mem_kv, state_mlstm_C, state_mlstm_n, state_mlstm_m, state_conv, page_table, norm_mix_g, w_in, nsa_qn_g, nsa_kn_g, cmp_pos, cmp_w1, cmp_w2, ml_conv_w, ml_conv_b, ml_gate_b, ml_hn_g, w_out, norm_xa_g, norm_mem_g, xa_wq, xa_wkv, xa_qn_g, xa_kn_g, xa_wo, norm_mlp_g, mlp_w1, mlp_w2):
    r = _r
    f32 = jnp.float32
    B, T, _ = x_prompt.shape
    DB, DS, _ = x_sample.shape
    P = page_table.shape[1] * cache_nsa_kv.shape[1]
    pos_p = jnp.arange(T, dtype=jnp.int32)
    pos_s = P + jnp.arange(DS, dtype=jnp.int32)
    wkeep = min(r.WINDOW, T)
    xp, xs = x_prompt, x_sample
    nsa_p, nsa_s, win_p, win_s = [], [], [], []
    C_p, C_s, n_p, n_s, m_p, m_s = [], [], [], [], [], []
    cv_p, cv_s, mem_p = [], [], []
    for l in range(r.DEPTH):
        zq_p, zkv_p, zg_p, zqk_p, zv_p, zo_p, zif_p = r.split_cols(r.rms_norm(xp, norm_mix_g[l]) @ w_in[l])
        zq_s, zkv_s, zg_s, zqk_s, zv_s, zo_s, zif_s = r.split_cols(r.rms_norm(xs, norm_mix_g[l]) @ w_in[l])
        q, kvp, gt = r.nsa_prepare(zq_p, zkv_p, zg_p, pos_p, nsa_qn_g[l], nsa_kn_g[l])
        a_p = r.nsa_prompt(q, kvp, gt, cmp_pos[l], cmp_w1[l], cmp_w2[l])
        nsa_p.append(kvp[:, :, :2].reshape(B, T, 4, r.NSA_KV_HEADS, r.NSA_HD))
        win_p.append(kvp[:, T - wkeep:, 2])
        q, kvp, gt = r.nsa_prepare(zq_s, zkv_s, zg_s, pos_s, nsa_qn_g[l], nsa_kn_g[l])
        past = cache_nsa_kv[page_table, :, l].reshape(DB, P, 4, r.NSA_KV_HEADS, r.NSA_HD)
        a_s, rows_s, wbuf_s = r.nsa_sample(q, kvp, gt, pos_s, past, cache_win_kv[l], cmp_pos[l], cmp_w1[l], cmp_w2[l])
        nsa_s.append(rows_s)
        win_s.append(wbuf_s)
        inp, buf_p = r.mlstm_prepare(zqk_p, zv_p, zif_p, jnp.zeros((B, r.CONV_W - 1, 2 * r.ML_W), xp.dtype),
                                     ml_conv_w[l], ml_conv_b[l], ml_gate_b[l])
        st0 = (jnp.zeros((B, r.ML_HEADS, r.ML_HD, r.ML_HD), f32), jnp.zeros((B, r.ML_HEADS, r.ML_HD), f32),
               jnp.zeros((B, r.ML_HEADS), f32))
        (Cn, nn, mn), h = r.mlstm_sequence(inp, st0)
        b_p = r.mlstm_output(h, zo_p, ml_hn_g[l])
        C_p.append(Cn); n_p.append(nn); m_p.append(mn); cv_p.append(buf_p)
        inp, buf_s = r.mlstm_prepare(zqk_s, zv_s, zif_s, state_conv[l], ml_conv_w[l], ml_conv_b[l], ml_gate_b[l])
        st = (state_mlstm_C[l].astype(f32), state_mlstm_n[l].astype(f32), state_mlstm_m[l].astype(f32))
        (Cn, nn, mn), h = r.mlstm_chunk(st, inp)
        b_s = r.mlstm_output(h, zo_s, ml_hn_g[l])
        C_s.append(Cn); n_s.append(nn); m_s.append(mn); cv_s.append(buf_s)
        xp = xp + jnp.concatenate([a_p, b_p], axis=-1) @ w_out[l]
        xs = xs + jnp.concatenate([a_s, b_s], axis=-1) @ w_out[l]
        mkv_p = r.memory_kv(mem_prompt, norm_mem_g[l], xa_wkv[l], xa_kn_g[l])
        mem_p.append(mkv_p)
        xp = xp + r.cross_attn(r.rms_norm(xp, norm_xa_g[l]), mkv_p, xa_wq[l], xa_qn_g[l], xa_wo[l])
        xs = xs + r.cross_attn(r.rms_norm(xs, norm_xa_g[l]), cache_mem_kv[l], xa_wq[l], xa_qn_g[l], xa_wo[l])
        w1b = mlp_w1[l].astype(jnp.bfloat16)
        w2b = mlp_w2[l].astype(jnp.bfloat16)
        xp = mlp_block(xp.reshape(B * T, -1), norm_mlp_g[l], w1b, w2b, tm=1024).reshape(B, T, -1)
        xs = mlp_block(xs.reshape(DB * DS, -1), norm_mlp_g[l], w1b, w2b, tm=128).reshape(DB, DS, -1)
    dt = xp.dtype
    return (xp, xs,
            jnp.stack(nsa_p, axis=2), jnp.stack(nsa_s, axis=2),
            jnp.stack(win_p), jnp.stack(win_s),
            jnp.stack(C_p).astype(dt), jnp.stack(C_s).astype(dt),
            jnp.stack(n_p).astype(dt), jnp.stack(n_s).astype(dt),
            jnp.stack(m_p).astype(dt), jnp.stack(m_s).astype(dt),
            jnp.stack(cv_p), jnp.stack(cv_s),
            jnp.stack(mem_p))
```

```python
import functools

import jax
import jax.numpy as jnp
import numpy as np
from jax import lax
from jax.experimental import pallas as pl
from jax.experimental.pallas import tpu as pltpu

F32 = jnp.float32
BF = jnp.bfloat16

EPS = 1e-6
NSA_HEADS, NSA_KV_HEADS, NSA_HD = 8, 2, 64
NSA_GROUP = NSA_HEADS // NSA_KV_HEADS
CMP_LEN, CMP_STRIDE, CMP_HIDDEN = 32, 16, 128
SEL_LEN, N_SEL, WINDOW = 64, 8, 512
ROPE_DIMS, ROPE_THETA = NSA_HD // 4, 500000.0
ML_HEADS, ML_HD, CONV_W = 4, 128, 4
ML_W = ML_HEADS * ML_HD
XA_HEADS, XA_HD = 4, 64
LANES = 128
QB = 128
VMEM_LIMIT = 56 * 1024 * 1024

NSA_Q_W = NSA_HEADS * NSA_HD
NSA_KV_W = 6 * NSA_KV_HEADS * NSA_HD
NSA_G_W = 3 * NSA_HEADS
COL_SPLITS = (NSA_Q_W, NSA_KV_W, NSA_G_W, 2 * ML_W, ML_W, ML_W, 2 * ML_HEADS)
NEG = -1e30


def _dot(a, b):
    return jnp.dot(a, b, preferred_element_type=F32)


def _dot_nt(a, b):
    return lax.dot_general(a, b, (((1,), (1,)), ((), ())), preferred_element_type=F32)


def _dot_hl(a, b):
    hi = a.astype(BF)
    lo = (a - hi.astype(F32)).astype(BF)
    return _dot(hi, b) + _dot(lo, b)


def _rms(x, g):
    return x * lax.rsqrt(jnp.mean(x * x, axis=-1, keepdims=True) + EPS) * g


def _head_norm(z, g, seg):
    return z * lax.rsqrt(_dot_hl(z * z, seg) + EPS) * g


def _rope(y, c, sa, sb):
    half = ROPE_DIMS // 2
    return y * c + pltpu.roll(y, LANES - half, 1) * sa + pltpu.roll(y, half, 1) * sb


def _params(*sem):
    return pltpu.CompilerParams(dimension_semantics=sem, vmem_limit_bytes=VMEM_LIMIT)


def _full(shape):
    n = len(shape)
    return pl.BlockSpec(shape, lambda *_: (0,) * n)


def _inproj_kernel(x_ref, g_ref, wq_ref, wkv_ref, wqk_ref, wv_ref, wo_ref, wgi_ref,
                   rc_ref, rsa_ref, rsb_ref, qg_ref, kg_ref, seg_ref,
                   q_out, kv_out, kvb_out, zqk_out, zv_out, zo_out, gi_out):
    xn = _rms(x_ref[...], g_ref[...]).astype(BF)
    c, sa, sb = rc_ref[...], rsa_ref[...], rsb_ref[...]
    seg = seg_ref[...]
    zq = _dot(xn, wq_ref[...])
    for j in range(NSA_Q_W // LANES):
        sl = slice(j * LANES, (j + 1) * LANES)
        q_out[:, sl] = _rope(_head_norm(zq[:, sl], qg_ref[...], seg), c, sa, sb).astype(BF)
    zkv = _dot(xn, wkv_ref[...])
    for j in range(NSA_KV_W // LANES):
        sl = slice(j * LANES, (j + 1) * LANES)
        z = zkv[:, sl]
        if j % 2 == 0:
            z = _rope(_head_norm(z, kg_ref[j // 2:j // 2 + 1, :], seg), c, sa, sb)
        kv_out[:, sl] = z
        kvb_out[:, sl] = z.astype(BF)
    zqk_out[...] = _dot(xn, wqk_ref[...])
    zv_out[...] = _dot(xn, wv_ref[...])
    zo_out[...] = _dot(xn, wo_ref[...])
    gi_out[...] = _dot(xn, wgi_ref[...])


def in_proj(x, g, w, rope_tabs, qg, kg, seg, tm):
    M, D = x.shape
    rc, rsa, rsb = rope_tabs
    nt = rc.shape[0] // tm
    widths = (NSA_Q_W, NSA_KV_W, NSA_KV_W, 2 * ML_W, ML_W, ML_W, LANES)
    dtypes = (BF, F32, BF, F32, F32, F32, F32)
    row = lambda wd: pl.BlockSpec((tm, wd), lambda i: (i, 0))
    tab = pl.BlockSpec((tm, LANES), lambda i: (i % nt, 0))
    return pl.pallas_call(
        _inproj_kernel,
        grid=(M // tm,),
        in_specs=[row(D), _full((1, D))] + [_full(a.shape) for a in w] + [tab, tab, tab,
                  _full(qg.shape), _full(kg.shape), _full(seg.shape)],
        out_specs=[row(wd) for wd in widths],
        out_shape=[jax.ShapeDtypeStruct((M, wd), dt) for wd, dt in zip(widths, dtypes)],
        compiler_params=_params("parallel"),
        name="in_proj",
    )(x, g.reshape(1, D), *w, rc, rsa, rsb, qg, kg, seg)


def _compress_kernel(xk_ref, xv_ref, pa_ref, pb_ref, w1a_ref, w1b_ref, w2_ref, o_ref, sh_ref, *, nsub):
    G = NSA_KV_HEADS
    acc_a = [jnp.zeros((nsub, CMP_HIDDEN), F32) for _ in range(2 * G)]
    acc_b = [jnp.zeros((nsub, CMP_HIDDEN), F32) for _ in range(2 * G)]
    for u in range(CMP_STRIDE):
        for kind, x_ref in enumerate((xk_ref, xv_ref)):
            sl = slice(kind * LANES, (kind + 1) * LANES)
            xu = x_ref[pl.ds(u, nsub, stride=CMP_STRIDE), :]
            xa = (xu + pa_ref[u:u + 1, sl]).astype(BF)
            xb = (xu + pb_ref[u:u + 1, sl]).astype(BF)
            for g in range(G):
                acc_a[kind * G + g] += _dot(xa, w1a_ref[kind, g, u])
                acc_b[kind * G + g] += _dot(xb, w1b_ref[kind, g, u])
    sh_ref[pl.ds(nsub, 8), :] = jnp.zeros((8, CMP_HIDDEN), F32)
    for kind in range(2):
        for g in range(G):
            sh_ref[pl.ds(0, nsub), :] = acc_b[kind * G + g]
            h = acc_a[kind * G + g] + sh_ref[pl.ds(1, nsub), :]
            h = h * (1.0 / (1.0 + jnp.exp(-h)))
            col = (kind * G + g) * NSA_HD
            o_ref[0, :, col:col + NSA_HD] = _dot(h.astype(BF), w2_ref[kind])


def compress_prompt(kv, cw, B, T):
    pa, pb, w1a, w1b, w2 = cw
    nsub = T // CMP_STRIDE
    return pl.pallas_call(
        functools.partial(_compress_kernel, nsub=nsub),
        grid=(B,),
        in_specs=[pl.BlockSpec((T, LANES), lambda b: (b, 0)), pl.BlockSpec((T, LANES), lambda b: (b, 1)),
                  _full(pa.shape), _full(pb.shape), _full(w1a.shape), _full(w1b.shape), _full(w2.shape)],
        out_specs=pl.BlockSpec((1, nsub, 2 * LANES), lambda b: (b, 0, 0)),
        out_shape=jax.ShapeDtypeStruct((B, nsub, 2 * LANES), F32),
        scratch_shapes=[pltpu.VMEM((nsub + 8, CMP_HIDDEN), F32)],
        compiler_params=_params("parallel"),
        name="compress_prompt",
    )(kv, kv, pa, pb, w1a, w1b, w2)


def _select_topk(imp, cur, n_blocks):
    j = lax.broadcasted_iota(jnp.int32, imp.shape, 1)
    forced = (j == cur) | (j == 0)
    dead = (j > cur) | (j >= n_blocks)
    val = jnp.where(forced, jnp.inf, jnp.where(dead, -jnp.inf, imp))
    sel = jnp.zeros(imp.shape, F32)
    for _ in range(N_SEL):
        m = jnp.max(val, axis=1, keepdims=True)
        idx = jnp.min(jnp.where(val == m, j, imp.shape[1]), axis=1, keepdims=True)
        hit = j == idx
        sel = jnp.where(hit & (m > -jnp.inf), 1.0, sel)
        val = jnp.where(hit, -jnp.inf, val)
    return sel


def _flash_update(s, mask, v, m_ref, l_ref, acc_ref, rows):
    s = jnp.where(mask, s, NEG)
    m_old = m_ref[rows, :]
    m_new = jnp.maximum(m_old, jnp.max(s, axis=1, keepdims=True))
    p = jnp.where(mask, jnp.exp(s - m_new), 0.0)
    alpha = jnp.exp(m_old - m_new)
    l_ref[rows, :] = alpha * l_ref[rows, :] + jnp.sum(p, axis=1, keepdims=True)
    acc_ref[rows, :] = alpha * acc_ref[rows, :] + _dot(p.astype(BF), v)
    m_ref[rows, :] = m_new


def _nsa_prompt_kernel(q_ref, gi_ref, kcv_ref, sel_ref, win_ref, cover_ref, exp_ref, o_ref,
                       m_ref, l_ref, acc_ref, *, n_sel_blocks):
    G, HG, D = NSA_KV_HEADS, NSA_GROUP, NSA_HD
    qb = pl.program_id(1)
    q0 = qb * QB
    qpos = q0 + lax.broadcasted_iota(jnp.int32, (QB, LANES), 0)
    lane = lax.broadcasted_iota(jnp.int32, (QB, LANES), 1)
    gates = 1.0 / (1.0 + jnp.exp(-gi_ref[...]))
    nc = kcv_ref.shape[1]
    outs = []
    for g in range(G):
        qg = jnp.concatenate([q_ref[:, (g * HG + h) * D:(g * HG + h + 1) * D] for h in range(HG)], axis=0)
        qg = qg * jnp.asarray(D ** -0.5, BF)
        kc = kcv_ref[0, :, g * D:(g + 1) * D].astype(BF)
        vc = kcv_ref[0, :, (G + g) * D:(G + g + 1) * D].astype(BF)
        s_all = _dot_nt(qg, kc)
        cmask = (lane[:, :nc] * CMP_STRIDE + (CMP_LEN - 1)) <= qpos[:, :nc]
        psum = jnp.zeros((QB, nc), F32)
        o_c = []
        for h in range(HG):
            s = jnp.where(cmask, s_all[h * QB:(h + 1) * QB], NEG)
            e = jnp.where(cmask, jnp.exp(s - jnp.max(s, axis=1, keepdims=True)), 0.0)
            den = jnp.sum(e, axis=1, keepdims=True)
            p = e / jnp.where(den > 0, den, 1.0)
            psum = psum + p
            o_c.append(_dot(p.astype(BF), vc))
        imp = _dot_hl(psum, cover_ref[...])
        sel = _select_topk(imp, qpos // SEL_LEN, n_sel_blocks).astype(BF)
        branch_out = []
        for br, kv_ref in ((0, sel_ref), (1, win_ref)):
            m_ref[...] = jnp.full(m_ref.shape, NEG, F32)
            l_ref[...] = jnp.zeros(l_ref.shape, F32)
            acc_ref[...] = jnp.zeros(acc_ref.shape, F32)

            def chunk(c, carry, kv_ref=kv_ref, br=br):
                k0 = pl.multiple_of(c * QB, QB)
                k = kv_ref[pl.ds(k0, QB), g * D:(g + 1) * D]
                v = kv_ref[pl.ds(k0, QB), (G + g) * D:(G + g + 1) * D]
                s_all = _dot_nt(qg, k)
                d = qpos - (k0 + lane)
                if br == 0:
                    mask = (_dot(sel, exp_ref[c]) > 0.5) & (d >= 0)
                else:
                    mask = (d >= 0) & (d <= WINDOW)
                for h in range(HG):
                    rows = pl.ds(h * QB, QB)
                    _flash_update(s_all[h * QB:(h + 1) * QB], mask, v, m_ref, l_ref, acc_ref, rows)
                return carry

            lo = 0 if br == 0 else jnp.maximum(qb - WINDOW // QB, 0)
            lax.fori_loop(lo, qb + 1, chunk, 0)
            l = l_ref[...]
            branch_out.append(acc_ref[...] / jnp.where(l > 0, l, 1.0))
        for h in range(HG):
            hh = g * HG + h
            rows = slice(h * QB, (h + 1) * QB)
            outs.append(gates[:, 3 * hh:3 * hh + 1] * o_c[h]
                        + gates[:, 3 * hh + 1:3 * hh + 2] * branch_out[0][rows]
                        + gates[:, 3 * hh + 2:3 * hh + 3] * branch_out[1][rows])
    o_ref[...] = jnp.concatenate(outs, axis=1).astype(BF)


def nsa_prompt(q, gi, kvb, kcv, cover, expand, B, T):
    nqb = T // QB
    nc = kcv.shape[1]
    rows = HGQ = NSA_GROUP * QB
    return pl.pallas_call(
        functools.partial(_nsa_prompt_kernel, n_sel_blocks=T // SEL_LEN),
        grid=(B, nqb),
        in_specs=[pl.BlockSpec((QB, NSA_Q_W), lambda b, i: (b * nqb + i, 0)),
                  pl.BlockSpec((QB, LANES), lambda b, i: (b * nqb + i, 0)),
                  pl.BlockSpec((1, nc, 2 * LANES), lambda b, i: (b, 0, 0)),
                  pl.BlockSpec((T, 2 * LANES), lambda b, i: (b, 1)),
                  pl.BlockSpec((T, 2 * LANES), lambda b, i: (b, 2)),
                  _full(cover.shape), _full(expand.shape)],
        out_specs=pl.BlockSpec((QB, NSA_Q_W), lambda b, i: (b * nqb + i, 0)),
        out_shape=jax.ShapeDtypeStruct((B * T, NSA_Q_W), BF),
        scratch_shapes=[pltpu.VMEM((rows, 1), F32), pltpu.VMEM((rows, 1), F32), pltpu.VMEM((rows, NSA_HD), F32)],
        compiler_params=_params("parallel", "arbitrary"),
        name="nsa_prompt",
    )(q, gi, kcv, kvb, kvb, cover, expand)


def _memkv_kernel(x_ref, g_ref, w_ref, kg_ref, seg_ref, o_ref, ob_ref):
    z = _dot(_rms(x_ref[...], g_ref[...]).astype(BF), w_ref[...])
    kw = XA_HEADS * XA_HD
    for j in range(2 * kw // LANES):
        sl = slice(j * LANES, (j + 1) * LANES)
        zc = z[:, sl]
        if j * LANES < kw:
            zc = _head_norm(zc, kg_ref[...], seg_ref[...])
        o_ref[:, sl] = zc
        ob_ref[:, sl] = zc.astype(BF)


def memory_kv(mem, g, w, kg, seg, tm=512):
    M, D = mem.shape
    N = w.shape[1]
    row = lambda wd: pl.BlockSpec((tm, wd), lambda i: (i, 0))
    return pl.pallas_call(
        _memkv_kernel,
        grid=(M // tm,),
        in_specs=[row(D), _full((1, D)), _full(w.shape), _full(kg.shape), _full(seg.shape)],
        out_specs=[row(N), row(N)],
        out_shape=[jax.ShapeDtypeStruct((M, N), F32), jax.ShapeDtypeStruct((M, N), BF)],
        compiler_params=_params("parallel"),
        name="memory_kv",
    )(mem, g.reshape(1, D), w, kg, seg)


def _postmix_kernel(x_ref, a_ref, b_ref, wa_ref, wb_ref, gx_ref, wq_ref, qg_ref, seg_ref, mkv_ref, wo_ref, o_ref,
                    *, rows_per_batch, n_mem):
    x1 = x_ref[...] + _dot(a_ref[...], wa_ref[...]) + _dot(b_ref[...], wb_ref[...])
    xn = _rms(x1, gx_ref[...]).astype(BF)
    zq = _dot(xn, wq_ref[...])
    kw = XA_HEADS * XA_HD
    q = jnp.concatenate([_head_norm(zq[:, j * LANES:(j + 1) * LANES], qg_ref[...], seg_ref[...])
                         for j in range(kw // LANES)], axis=1)
    q = (q * XA_HD ** -0.5).astype(BF)
    tm, nk = x1.shape[0], mkv_ref.shape[1]
    mask = None
    if rows_per_batch < tm:
        r = lax.broadcasted_iota(jnp.int32, (tm, nk), 0) // rows_per_batch
        c = lax.broadcasted_iota(jnp.int32, (tm, nk), 1) // n_mem
        mask = r == c
    outs = []
    for h in range(XA_HEADS):
        k = mkv_ref[0, :, h * XA_HD:(h + 1) * XA_HD]
        v = mkv_ref[0, :, kw + h * XA_HD:kw + (h + 1) * XA_HD]
        s = _dot_nt(q[:, h * XA_HD:(h + 1) * XA_HD], k)
        if mask is not None:
            s = jnp.where(mask, s, NEG)
        e = jnp.exp(s - jnp.max(s, axis=1, keepdims=True))
        outs.append(_dot(e.astype(BF), v) / jnp.sum(e, axis=1, keepdims=True))
    o = jnp.concatenate(outs, axis=1).astype(BF)
    o_ref[...] = x1 + _dot(o, wo_ref[...])


def post_mix(x, a, b, wa, wb, gx, wq, qg, seg, mkvb, wo, tm, rows_per_batch):
    M, D = x.shape
    n_mem = mkvb.shape[1]
    if rows_per_batch >= tm:
        per = rows_per_batch // tm
        mspec = pl.BlockSpec((1, n_mem, mkvb.shape[2]), lambda i: (i // per, 0, 0))
    else:
        assert tm == M
        mkvb = mkvb.reshape(1, -1, mkvb.shape[2])
        mspec = _full(mkvb.shape)
    row = lambda wd: pl.BlockSpec((tm, wd), lambda i: (i, 0))
    return pl.pallas_call(
        functools.partial(_postmix_kernel, rows_per_batch=rows_per_batch, n_mem=n_mem),
        grid=(M // tm,),
        in_specs=[row(D), row(a.shape[1]), row(b.shape[1]), _full(wa.shape), _full(wb.shape), _full((1, D)),
                  _full(wq.shape), _full(qg.shape), _full(seg.shape), mspec, _full(wo.shape)],
        out_specs=row(D),
        out_shape=jax.ShapeDtypeStruct((M, D), F32),
        compiler_params=_params("parallel"),
        name="post_mix",
    )(x, a, b, wa, wb, gx.reshape(1, D), wq, qg, seg, mkvb, wo)


def _mlp_kernel(x_ref, g_ref, w1_ref, w2_ref, o_ref, xn_ref, acc_ref):
    f = pl.program_id(1)

    @pl.when(f == 0)
    def _():
        x = x_ref[...]
        xn_ref[...] = _rms(x, g_ref[...]).astype(BF)
        acc_ref[...] = x

    h = _dot(xn_ref[...], w1_ref[...])
    h = jnp.square(jnp.maximum(h, 0.0)).astype(BF)
    acc_ref[...] += _dot(h, w2_ref[...])

    @pl.when(f == pl.num_programs(1) - 1)
    def _():
        o_ref[...] = acc_ref[...]


def mlp_block(x, g, w1b, w2b, tm, tf=512):
    M, D = x.shape
    FF = w1b.shape[1]
    return pl.pallas_call(
        _mlp_kernel,
        grid=(M // tm, FF // tf),
        in_specs=[pl.BlockSpec((tm, D), lambda i, f: (i, 0)),
                  pl.BlockSpec((1, D), lambda i, f: (0, 0)),
                  pl.BlockSpec((D, tf), lambda i, f: (0, f)),
                  pl.BlockSpec((tf, D), lambda i, f: (f, 0))],
        out_specs=pl.BlockSpec((tm, D), lambda i, f: (i, 0)),
        out_shape=jax.ShapeDtypeStruct((M, D), F32),
        scratch_shapes=[pltpu.VMEM((tm, D), BF), pltpu.VMEM((tm, D), F32)],
        compiler_params=_params("parallel", "arbitrary"),
        name="mlp",
    )(x, g.reshape(1, D), w1b, w2b)


def _rope_tables(pos, rows):
    half = ROPE_DIMS // 2
    freqs = ROPE_THETA ** (-jnp.arange(half, dtype=F32) / half)
    ang = pos.astype(F32)[:, None] * freqs
    cos, sin = jnp.cos(ang), jnp.sin(ang)
    n = pos.shape[0]
    one, zero = jnp.ones((n, NSA_HD - ROPE_DIMS), F32), jnp.zeros((n, NSA_HD - ROPE_DIMS), F32)
    zh = jnp.zeros((n, half), F32)
    c = jnp.concatenate([cos, cos, one], axis=1)
    sa = jnp.concatenate([-sin, zh, zero], axis=1)
    sb = jnp.concatenate([zh, sin, zero], axis=1)
    tile = lambda t: jnp.tile(t, (rows // n, LANES // NSA_HD))
    return tile(c), tile(sa), tile(sb)


def _seg_matrix():
    i = np.arange(LANES)
    return jnp.asarray((i[:, None] // NSA_HD == i[None, :] // NSA_HD) / NSA_HD, BF)


def _cover_matrix(nc_rows, lanes):
    ci = np.arange(nc_rows)[:, None] * CMP_STRIDE
    sj = np.arange(lanes)[None, :]
    return jnp.asarray((ci < (sj + 1) * SEL_LEN) & (ci + CMP_LEN > sj * SEL_LEN), BF)


def _expand_matrix(n_chunks):
    j = np.arange(LANES)[None, :, None]
    c = np.arange(n_chunks)[:, None, None]
    s = np.arange(QB)[None, None, :]
    return jnp.asarray(j == c * (QB // SEL_LEN) + s // SEL_LEN, BF)


def _in_weights(w):
    zq, zkv, zg, zqk, zv, zo, zif = jnp.split(w, np.cumsum(COL_SPLITS)[:-1].tolist(), axis=1)
    pad = jnp.zeros((w.shape[0], LANES - NSA_G_W - 2 * ML_HEADS), w.dtype)
    return tuple(a.astype(BF) for a in (zq, zkv, zqk, zv, zo, jnp.concatenate([zg, zif, pad], axis=1)))


def _compress_weights(cmp_pos, cmp_w1, cmp_w2):
    G, D, S = NSA_KV_HEADS, NSA_HD, CMP_STRIDE
    pos = jnp.tile(cmp_pos[:, :, None, :], (1, 1, G, 1))
    pos = pos.transpose(1, 0, 2, 3).reshape(CMP_LEN, 2 * G * D)
    w1 = cmp_w1.reshape(2, CMP_LEN, D, CMP_HIDDEN)
    z = jnp.zeros_like(w1)
    w1g = jnp.stack([jnp.concatenate([w1, z], axis=2), jnp.concatenate([z, w1], axis=2)], axis=1)
    return (pos[:S], pos[S:], w1g[:, :, :S].astype(BF), w1g[:, :, S:].astype(BF), cmp_w2.astype(BF))


def _masked_softmax(s, mask):
    s = jnp.where(mask, s.astype(F32), -jnp.inf)
    m = jnp.max(s, axis=-1, keepdims=True)
    m = jnp.where(jnp.isfinite(m), m, 0.0)
    e = jnp.exp(s - m)
    den = jnp.sum(e, axis=-1, keepdims=True)
    return e / jnp.where(den > 0, den, 1.0)


def _compress_x(x, pos_emb, w1, w2):
    B, T, G, D = x.shape
    nc = (T - CMP_LEN) // CMP_STRIDE + 1
    r = CMP_LEN // CMP_STRIDE
    sub = x[:, :(nc + r - 1) * CMP_STRIDE].reshape(B, nc + r - 1, CMP_STRIDE, G, D)
    blocks = jnp.concatenate([sub[:, j:j + nc] for j in range(r)], axis=2)
    blocks = blocks + pos_emb[:, None, :]
    flat = blocks.transpose(0, 1, 3, 2, 4).reshape(B, nc, G, CMP_LEN * D)
    return jax.nn.silu(flat @ w1) @ w2


def _sel_blocks(x):
    B, T, G, D = x.shape
    ns = -(-T // SEL_LEN)
    x = jnp.pad(x, ((0, 0), (0, ns * SEL_LEN - T), (0, 0), (0, 0)))
    return x.reshape(B, ns, SEL_LEN, G, D).transpose(0, 3, 1, 2, 4)


def _nsa_attend_x(q, q_pos, gates, kc, vc, c_end, kb, vb, kw, vw, w_pos):
    B, Tq = q.shape[:2]
    G, HG, D = NSA_KV_HEADS, NSA_GROUP, NSA_HD
    scale = D ** -0.5
    qg = q.reshape(B, Tq, G, HG, D)
    s_c = jnp.einsum('btghd,bcgd->btghc', qg, kc) * scale
    m_c = (c_end[None, :] <= q_pos[:, None])[None, :, None, None, :]
    p_c = _masked_softmax(s_c, m_c)
    o_c = jnp.einsum('btghc,bcgd->btghd', p_c.astype(vc.dtype), vc)
    NC, NS = kc.shape[1], kb.shape[2]
    ci = jnp.arange(NC)[:, None] * CMP_STRIDE
    sj = jnp.arange(NS)[None, :]
    cover = ((ci < (sj + 1) * SEL_LEN) & (ci + CMP_LEN > sj * SEL_LEN)).astype(F32)
    imp = jnp.einsum('btghc,cs->btgs', p_c, cover)
    cur = (q_pos // SEL_LEN)[:, None]
    forced = ((sj == cur) | (sj == 0))[None, :, None, :]
    future = (sj > cur)[None, :, None, :]
    imp = jnp.where(forced, jnp.inf, jnp.where(future, -jnp.inf, imp))
    n_top = min(N_SEL, NS)
    top_v, top_i = lax.top_k(imp, n_top)
    gather = jax.vmap(jax.vmap(lambda blk, ix: blk[ix]))
    idx = top_i.transpose(0, 2, 1, 3)
    ks = gather(kb, idx).reshape(B, G, Tq, n_top * SEL_LEN, D)
    vs = gather(vb, idx).reshape(B, G, Tq, n_top * SEL_LEN, D)
    kpos = (top_i[..., None] * SEL_LEN + jnp.arange(SEL_LEN)).reshape(B, Tq, G, n_top * SEL_LEN)
    valid = jnp.repeat(top_v > -jnp.inf, SEL_LEN, axis=-1)
    m_s = (valid & (kpos <= q_pos[None, :, None, None]))[:, :, :, None, :]
    s_s = jnp.einsum('btghd,bgtkd->btghk', qg, ks) * scale
    p_s = _masked_softmax(s_s, m_s)
    o_s = jnp.einsum('btghk,bgtkd->btghd', p_s.astype(vs.dtype), vs)
    dpos = q_pos[:, None] - w_pos[None, :]
    m_w = ((dpos >= 0) & (dpos <= WINDOW) & (w_pos[None, :] >= 0))[None, :, None, None, :]
    s_w = jnp.einsum('btghd,bwgd->btghw', qg, kw) * scale
    p_w = _masked_softmax(s_w, m_w)
    o_w = jnp.einsum('btghw,bwgd->btghd', p_w.astype(vw.dtype), vw)
    g = gates.reshape(B, Tq, G, HG, 3)
    o = g[..., 0:1] * o_c + g[..., 1:2] * o_s + g[..., 2:3] * o_w
    return o.reshape(B, Tq, G * HG * D).astype(q.dtype)


def _nsa_sample_x(q, kvp, gates, pos, past_rows, win_buf, cmp_pos, cmp_w1, cmp_w2):
    DB, DS = q.shape[:2]
    new_rows = kvp[:, :, :2].reshape(DB, DS, 4, NSA_KV_HEADS, NSA_HD)
    full = jnp.concatenate([past_rows, new_rows], axis=1)
    kc = _compress_x(full[:, :, 0], cmp_pos[0], cmp_w1[0], cmp_w2[0])
    vc = _compress_x(full[:, :, 1], cmp_pos[1], cmp_w1[1], cmp_w2[1])
    c_end = jnp.arange(kc.shape[1], dtype=jnp.int32) * CMP_STRIDE + CMP_LEN - 1
    kb, vb = _sel_blocks(full[:, :, 2]), _sel_blocks(full[:, :, 3])
    WB = win_buf.shape[1]
    wfull = jnp.concatenate([win_buf, kvp[:, :, 2]], axis=1)
    w_pos = pos[0] - WB + jnp.arange(WB + DS, dtype=jnp.int32)
    o = _nsa_attend_x(q, pos, gates, kc, vc, c_end, kb, vb, wfull[:, :, 0], wfull[:, :, 1], w_pos)
    return o, new_rows, wfull[:, -WB:]


def _causal_conv_x(x, buf, w, b):
    T = x.shape[1]
    xp = jnp.concatenate([buf, x], axis=1)
    y = sum(xp[:, j:j + T] * w[j] for j in range(CONV_W)) + b
    return y, xp[:, -(CONV_W - 1):]


def _mlstm_prepare_x(z_qk, z_v, z_if, conv_buf, conv_w, conv_b, gate_b):
    B, T = z_qk.shape[:2]
    qk, new_buf = _causal_conv_x(z_qk, conv_buf, conv_w, conv_b)
    qk = jax.nn.silu(qk).astype(F32)
    heads = lambda a: a.reshape(B, T, ML_HEADS, ML_HD).transpose(0, 2, 1, 3)
    q = heads(qk[..., :ML_W])
    k = heads(qk[..., ML_W:]) * (ML_HD ** -0.5)
    v = heads(z_v.astype(F32))
    g = (z_if.astype(F32) + gate_b.astype(F32)).transpose(0, 2, 1)
    li = g[:, :ML_HEADS]
    lf = jax.nn.log_sigmoid(g[:, ML_HEADS:])
    return (q, k, v, li, lf), new_buf


def _mlstm_chunk_x(state, inp):
    C, n, m = state
    q, k, v, li, lf = inp
    L = q.shape[2]
    b = jnp.cumsum(lf, axis=-1)
    causal = jnp.tril(jnp.ones((L, L), dtype=bool))
    Dm = jnp.where(causal, b[..., :, None] - b[..., None, :] + li[..., None, :], -jnp.inf)
    inter = b + m[..., None]
    m_t = jnp.maximum(inter, jnp.max(Dm, axis=-1))
    W = jnp.exp(Dm - m_t[..., None])
    a = jnp.exp(inter - m_t)
    Wqk = W * jnp.einsum('bhtd,bhsd->bhts', q, k)
    num = a[..., None] * jnp.einsum('bhvk,bhtk->bhtv', C, q) + jnp.einsum('bhts,bhsv->bhtv', Wqk, v)
    den = a * jnp.einsum('bhk,bhtk->bht', n, q) + jnp.sum(Wqk, axis=-1)
    h = num / jnp.maximum(jnp.abs(den), jnp.exp(-m_t))[..., None]
    m_new = m_t[..., -1]
    wgt = jnp.exp(b[..., -1:] - b + li - m_new[..., None])
    decay = jnp.exp(b[..., -1] + m - m_new)
    C_new = decay[..., None, None] * C + jnp.einsum('bhs,bhsv,bhsk->bhvk', wgt, v, k)
    n_new = decay[..., None] * n + jnp.einsum('bhs,bhsk->bhk', wgt, k)
    return (C_new, n_new, m_new), h


def _mlstm_sequence_x(inp, state, chunk=64):
    T = inp[0].shape[2]
    nch = T // chunk

    def chunks(a):
        a = a.reshape(a.shape[:2] + (nch, chunk) + a.shape[3:])
        return jnp.moveaxis(a, 2, 0)

    state, h = lax.scan(_mlstm_chunk_x, state, tuple(chunks(a) for a in inp))
    h = jnp.moveaxis(h, 0, 2)
    return state, h.reshape(h.shape[:2] + (T, ML_HD))


def _mlstm_output_x(h, z_o, hn_g):
    B, NH, T, DV = h.shape
    hf = h.transpose(0, 2, 1, 3)
    hf = hf * lax.rsqrt(jnp.mean(hf * hf, axis=-1, keepdims=True) + EPS) * hn_g
    return (hf.reshape(B, T, NH * DV) * jax.nn.sigmoid(z_o)).astype(z_o.dtype)


def kernel(x_prompt, x_sample, mem_prompt, cache_nsa_kv, cache_win_kv, cache_mem_kv, state_mlstm_C, state_mlstm_n, state_mlstm_m, state_conv, page_table, norm_mix_g, w_in, nsa_qn_g, nsa_kn_g, cmp_pos, cmp_w1, cmp_w2, ml_conv_w, ml_conv_b, ml_gate_b, ml_hn_g, w_out, norm_xa_g, norm_mem_g, xa_wq, xa_wkv, xa_qn_g, xa_kn_g, xa_wo, norm_mlp_g, mlp_w1, mlp_w2):
    B, T, D = x_prompt.shape
    DB, DS, _ = x_sample.shape
    depth = w_in.shape[0]
    n_mem = mem_prompt.shape[1]
    P = page_table.shape[1] * cache_nsa_kv.shape[1]
    wkeep = min(WINDOW, T)
    G, HD = NSA_KV_HEADS, NSA_HD
    MP, MS = B * T, DB * DS
    tm_p, tm_s = 512, MS

    pos_p = jnp.arange(T, dtype=jnp.int32)
    pos_s = P + jnp.arange(DS, dtype=jnp.int32)
    tabs_p = _rope_tables(pos_p, T)
    tabs_s = _rope_tables(pos_s, MS)
    seg = _seg_matrix()
    cover = _cover_matrix(T // CMP_STRIDE, LANES)
    expand = _expand_matrix(T // QB)
    tile2 = lambda v: jnp.tile(v.reshape(-1, HD), (1, LANES // HD))

    xp = x_prompt.reshape(MP, D)
    xs = x_sample.reshape(MS, D)
    mem = mem_prompt.reshape(B * n_mem, D)
    nsa_p, nsa_s, win_p, win_s = [], [], [], []
    C_p, C_s, n_p, n_s, m_p, m_s = [], [], [], [], [], []
    cv_p, cv_s, mem_p = [], [], []
    for l in range(depth):
        w_l = _in_weights(w_in[l])
        qg, kg = tile2(nsa_qn_g[l]), tile2(nsa_kn_g[l])
        cw = _compress_weights(cmp_pos[l], cmp_w1[l], cmp_w2[l])
        wa, wb = w_out[l, :NSA_Q_W].astype(BF), w_out[l, NSA_Q_W:].astype(BF)
        xq, xo = xa_wq[l].astype(BF), xa_wo[l].astype(BF)
        xqg = tile2(xa_qn_g[l])
        w1b, w2b = mlp_w1[l].astype(BF), mlp_w2[l].astype(BF)

        q, kv, kvb, zqk, zv, zo, gi = in_proj(xp, norm_mix_g[l], w_l, tabs_p, qg, kg, seg, tm_p)
        kcv = compress_prompt(kv, cw, B, T)
        a_p = nsa_prompt(q, gi, kvb, kcv, cover, expand, B, T)
        kv3 = kv.reshape(B, T, NSA_KV_W)
        nsa_p.append(kv3[:, :, :4 * G * HD].reshape(B, T, 4, G, HD))
        win_p.append(kv3[:, T - wkeep:, 4 * G * HD:].reshape(B, wkeep, 2, G, HD))
        zif = gi[:, NSA_G_W:NSA_G_W + 2 * ML_HEADS]
        inp, buf_p = _mlstm_prepare_x(zqk.reshape(B, T, -1), zv.reshape(B, T, -1), zif.reshape(B, T, -1),
                                      jnp.zeros((B, CONV_W - 1, 2 * ML_W), F32),
                                      ml_conv_w[l], ml_conv_b[l], ml_gate_b[l])
        st0 = (jnp.zeros((B, ML_HEADS, ML_HD, ML_HD), F32), jnp.zeros((B, ML_HEADS, ML_HD), F32),
               jnp.zeros((B, ML_HEADS), F32))
        (Cn, nn, mn), h = _mlstm_sequence_x(inp, st0)
        b_p = _mlstm_output_x(h, zo.reshape(B, T, -1), ml_hn_g[l]).reshape(MP, ML_W).astype(BF)
        C_p.append(Cn); n_p.append(nn); m_p.append(mn); cv_p.append(buf_p)
        mkv, mkvb = memory_kv(mem, norm_mem_g[l], xa_wkv[l].astype(BF), tile2(xa_kn_g[l]), seg)
        mem_p.append(mkv.reshape(B, n_mem, 2, XA_HEADS, XA_HD))
        xp = post_mix(xp, a_p, b_p, wa, wb, norm_xa_g[l], xq, xqg, seg, mkvb.reshape(B, n_mem, -1), xo, tm_p, T)
        xp = mlp_block(xp, norm_mlp_g[l], w1b, w2b, tm=1024)

        q, kv, kvb, zqk, zv, zo, gi = in_proj(xs, norm_mix_g[l], w_l, tabs_s, qg, kg, seg, tm_s)
        kvp = kv.reshape(DB, DS, 3, 2, G, HD)
        gt = jax.nn.sigmoid(gi[:, :NSA_G_W]).reshape(DB, DS, NSA_HEADS, 3)
        past = cache_nsa_kv[page_table, :, l].reshape(DB, P, 4, G, HD)
        a_s, rows_s, wbuf_s = _nsa_sample_x(q.astype(F32).reshape(DB, DS, NSA_HEADS, HD), kvp, gt, pos_s, past,
                                            cache_win_kv[l], cmp_pos[l], cmp_w1[l], cmp_w2[l])
        nsa_s.append(rows_s)
        win_s.append(wbuf_s)
        zif = gi[:, NSA_G_W:NSA_G_W + 2 * ML_HEADS]
        inp, buf_s = _mlstm_prepare_x(zqk.reshape(DB, DS, -1), zv.reshape(DB, DS, -1), zif.reshape(DB, DS, -1),
                                      state_conv[l], ml_conv_w[l], ml_conv_b[l], ml_gate_b[l])
        st = (state_mlstm_C[l], state_mlstm_n[l], state_mlstm_m[l])
        (Cn, nn, mn), h = _mlstm_chunk_x(st, inp)
        b_s = _mlstm_output_x(h, zo.reshape(DB, DS, -1), ml_hn_g[l]).reshape(MS, ML_W).astype(BF)
        C_s.append(Cn); n_s.append(nn); m_s.append(mn); cv_s.append(buf_s)
        mkvb_s = cache_mem_kv[l].reshape(DB, n_mem, -1).astype(BF)
        xs = post_mix(xs, a_s.reshape(MS, -1).astype(BF), b_s, wa, wb, norm_xa_g[l], xq, xqg, seg, mkvb_s, xo, tm_s, DS)
        xs = mlp_block(xs, norm_mlp_g[l], w1b, w2b, tm=MS)
    return (xp.reshape(B, T, D), xs.reshape(DB, DS, D),
            jnp.stack(nsa_p, axis=2), jnp.stack(nsa_s, axis=2),
            jnp.stack(win_p), jnp.stack(win_s),
            jnp.stack(C_p), jnp.stack(C_s),
            jnp.stack(n_p), jnp.stack(n_s),
            jnp.stack(m_p), jnp.stack(m_s),
            jnp.stack(cv_p), jnp.stack(cv_s),
            jnp.stack(mem_p))
```

```python
import functools

import jax
import jax.numpy as jnp
import numpy as np
from jax import lax
from jax.experimental import pallas as pl
from jax.experimental.pallas import tpu as pltpu

F32 = jnp.float32
BF = jnp.bfloat16

EPS = 1e-6
NSA_HEADS, NSA_KV_HEADS, NSA_HD = 8, 2, 64
NSA_GROUP = NSA_HEADS // NSA_KV_HEADS
CMP_LEN, CMP_STRIDE, CMP_HIDDEN = 32, 16, 128
SEL_LEN, N_SEL, WINDOW = 64, 8, 512
ROPE_DIMS, ROPE_THETA = NSA_HD // 4, 500000.0
ML_HEADS, ML_HD, CONV_W = 4, 128, 4
ML_W = ML_HEADS * ML_HD
XA_HEADS, XA_HD = 4, 64
LANES = 128
QB = 128
VMEM_LIMIT = 56 * 1024 * 1024

NSA_Q_W = NSA_HEADS * NSA_HD
NSA_KV_W = 6 * NSA_KV_HEADS * NSA_HD
NSA_G_W = 3 * NSA_HEADS
COL_SPLITS = (NSA_Q_W, NSA_KV_W, NSA_G_W, 2 * ML_W, ML_W, ML_W, 2 * ML_HEADS)
NEG = -1e30


def _dot(a, b):
    return jnp.dot(a, b, preferred_element_type=F32)


def _dot_nt(a, b):
    return lax.dot_general(a, b, (((1,), (1,)), ((), ())), preferred_element_type=F32)


def _dot_hl(a, b):
    hi = a.astype(BF)
    lo = (a - hi.astype(F32)).astype(BF)
    return _dot(hi, b) + _dot(lo, b)


def _rms(x, g):
    return x * lax.rsqrt(jnp.mean(x * x, axis=-1, keepdims=True) + EPS) * g


def _head_norm(z, g, seg):
    return z * lax.rsqrt(_dot_hl(z * z, seg) + EPS) * g


def _rope(y, c, sa, sb):
    half = ROPE_DIMS // 2
    return y * c + pltpu.roll(y, LANES - half, 1) * sa + pltpu.roll(y, half, 1) * sb


def _params(*sem):
    return pltpu.CompilerParams(dimension_semantics=sem, vmem_limit_bytes=VMEM_LIMIT)


def _full(shape):
    n = len(shape)
    return pl.BlockSpec(shape, lambda *_: (0,) * n)


def _inproj_kernel(x_ref, g_ref, wq_ref, wkv_ref, wqk_ref, wv_ref, wo_ref, wgi_ref,
                   rc_ref, rsa_ref, rsb_ref, qg_ref, kg_ref, seg_ref,
                   q_out, kv_out, kvb_out, zqk_out, zv_out, zo_out, gi_out):
    xn = _rms(x_ref[...], g_ref[...]).astype(BF)
    c, sa, sb = rc_ref[...], rsa_ref[...], rsb_ref[...]
    seg = seg_ref[...]
    zq = _dot(xn, wq_ref[...])
    for j in range(NSA_Q_W // LANES):
        sl = slice(j * LANES, (j + 1) * LANES)
        q_out[:, sl] = _rope(_head_norm(zq[:, sl], qg_ref[...], seg), c, sa, sb).astype(BF)
    zkv = _dot(xn, wkv_ref[...])
    for j in range(NSA_KV_W // LANES):
        sl = slice(j * LANES, (j + 1) * LANES)
        z = zkv[:, sl]
        if j % 2 == 0:
            z = _rope(_head_norm(z, kg_ref[j // 2:j // 2 + 1, :], seg), c, sa, sb)
        kv_out[:, sl] = z
        kvb_out[:, sl] = z.astype(BF)
    zqk_out[...] = _dot(xn, wqk_ref[...])
    zv_out[...] = _dot(xn, wv_ref[...])
    zo_out[...] = _dot(xn, wo_ref[...])
    gi_out[...] = _dot(xn, wgi_ref[...])


def in_proj(x, g, w, rope_tabs, qg, kg, seg, tm):
    M, D = x.shape
    rc, rsa, rsb = rope_tabs
    nt = rc.shape[0] // tm
    widths = (NSA_Q_W, NSA_KV_W, NSA_KV_W, 2 * ML_W, ML_W, ML_W, LANES)
    dtypes = (BF, F32, BF, F32, F32, F32, F32)
    row = lambda wd: pl.BlockSpec((tm, wd), lambda i: (i, 0))
    tab = pl.BlockSpec((tm, LANES), lambda i: (i % nt, 0))
    return pl.pallas_call(
        _inproj_kernel,
        grid=(M // tm,),
        in_specs=[row(D), _full((1, D))] + [_full(a.shape) for a in w] + [tab, tab, tab,
                  _full(qg.shape), _full(kg.shape), _full(seg.shape)],
        out_specs=[row(wd) for wd in widths],
        out_shape=[jax.ShapeDtypeStruct((M, wd), dt) for wd, dt in zip(widths, dtypes)],
        compiler_params=_params("parallel"),
        name="in_proj",
    )(x, g.reshape(1, D), *w, rc, rsa, rsb, qg, kg, seg)


def _compress_body(xk_ref, xv_ref, pa_ref, pb_ref, w1a_ref, w1b_ref, w2_ref, sh_ref, nsub):
    G = NSA_KV_HEADS
    acc_a = [jnp.zeros((nsub, CMP_HIDDEN), F32) for _ in range(2 * G)]
    acc_b = [jnp.zeros((nsub, CMP_HIDDEN), F32) for _ in range(2 * G)]
    for u in range(CMP_STRIDE):
        for kind, x_ref in enumerate((xk_ref, xv_ref)):
            sl = slice(kind * LANES, (kind + 1) * LANES)
            xu = x_ref[pl.ds(u, nsub, stride=CMP_STRIDE), :]
            xa = (xu + pa_ref[u:u + 1, sl]).astype(BF)
            xb = (xu + pb_ref[u:u + 1, sl]).astype(BF)
            for g in range(G):
                acc_a[kind * G + g] += _dot(xa, w1a_ref[kind, g, u])
                acc_b[kind * G + g] += _dot(xb, w1b_ref[kind, g, u])
    sh_ref[pl.ds(nsub, 8), :] = jnp.zeros((8, CMP_HIDDEN), F32)
    outs = []
    for kind in range(2):
        for g in range(G):
            sh_ref[pl.ds(0, nsub), :] = acc_b[kind * G + g]
            h = acc_a[kind * G + g] + sh_ref[pl.ds(1, nsub), :]
            h = h * (1.0 / (1.0 + jnp.exp(-h)))
            outs.append(_dot(h.astype(BF), w2_ref[kind]))
    return outs


def _compress_kernel(xk_ref, xv_ref, pa_ref, pb_ref, w1a_ref, w1b_ref, w2_ref, o_ref, sh_ref, *, nsub):
    outs = _compress_body(xk_ref, xv_ref, pa_ref, pb_ref, w1a_ref, w1b_ref, w2_ref, sh_ref, nsub)
    for j, o in enumerate(outs):
        o_ref[0, :, j * NSA_HD:(j + 1) * NSA_HD] = o


def compress_prompt(kv, cw, B, T):
    pa, pb, w1a, w1b, w2 = cw
    nsub = T // CMP_STRIDE
    return pl.pallas_call(
        functools.partial(_compress_kernel, nsub=nsub),
        grid=(B,),
        in_specs=[pl.BlockSpec((T, LANES), lambda b: (b, 0)), pl.BlockSpec((T, LANES), lambda b: (b, 1)),
                  _full(pa.shape), _full(pb.shape), _full(w1a.shape), _full(w1b.shape), _full(w2.shape)],
        out_specs=pl.BlockSpec((1, nsub, 2 * LANES), lambda b: (b, 0, 0)),
        out_shape=jax.ShapeDtypeStruct((B, nsub, 2 * LANES), F32),
        scratch_shapes=[pltpu.VMEM((nsub + 8, CMP_HIDDEN), F32)],
        compiler_params=_params("parallel"),
        name="compress_prompt",
    )(kv, kv, pa, pb, w1a, w1b, w2)


def _select_topk(imp, cur, n_blocks):
    j = lax.broadcasted_iota(jnp.int32, imp.shape, 1)
    forced = (j == cur) | (j == 0)
    dead = (j > cur) | (j >= n_blocks)
    val = jnp.where(forced, jnp.inf, jnp.where(dead, -jnp.inf, imp))
    sel = jnp.zeros(imp.shape, F32)
    for _ in range(N_SEL):
        m = jnp.max(val, axis=1, keepdims=True)
        idx = jnp.min(jnp.where(val == m, j, imp.shape[1]), axis=1, keepdims=True)
        hit = j == idx
        sel = jnp.where(hit & (m > -jnp.inf), 1.0, sel)
        val = jnp.where(hit, -jnp.inf, val)
    return sel


def _nsa_prompt_kernel(q_ref, gi_ref, kcv_ref, sel_ref, win_ref, cover_ref, exp_ref, o_ref,
                       s_ref, mx_ref, l_ref, acc_ref, *, n_sel_blocks):
    G, HG, D = NSA_KV_HEADS, NSA_GROUP, NSA_HD
    qb = pl.program_id(1)
    q0 = qb * QB
    qpos = q0 + lax.broadcasted_iota(jnp.int32, (QB, LANES), 0)
    lane = lax.broadcasted_iota(jnp.int32, (QB, LANES), 1)
    gates = 1.0 / (1.0 + jnp.exp(-gi_ref[...]))
    nc = kcv_ref.shape[1]
    outs = []
    for g in range(G):
        qg = jnp.concatenate([q_ref[:, (g * HG + h) * D:(g * HG + h + 1) * D] for h in range(HG)], axis=0)
        qg = qg * jnp.asarray(D ** -0.5, BF)
        kc = kcv_ref[0, :, g * D:(g + 1) * D].astype(BF)
        vc = kcv_ref[0, :, (G + g) * D:(G + g + 1) * D].astype(BF)
        s_all = _dot_nt(qg, kc)
        cmask = (lane[:, :nc] * CMP_STRIDE + (CMP_LEN - 1)) <= qpos[:, :nc]
        psum = jnp.zeros((QB, nc), F32)
        o_c = []
        for h in range(HG):
            s = jnp.where(cmask, s_all[h * QB:(h + 1) * QB], NEG)
            e = jnp.where(cmask, jnp.exp(s - jnp.max(s, axis=1, keepdims=True)), 0.0)
            den = jnp.sum(e, axis=1, keepdims=True)
            p = e / jnp.where(den > 0, den, 1.0)
            psum = psum + p
            o_c.append(_dot(p.astype(BF), vc))
        imp = _dot_hl(psum, cover_ref[...])
        sel = _select_topk(imp, qpos // SEL_LEN, n_sel_blocks).astype(BF)
        branch_out = []
        for br, kv_ref in ((0, sel_ref), (1, win_ref)):
            mx_ref[...] = jnp.full(mx_ref.shape, NEG, F32)
            l_ref[...] = jnp.zeros(l_ref.shape, F32)
            acc_ref[...] = jnp.zeros(acc_ref.shape, F32)

            def scores(c, carry, kv_ref=kv_ref, br=br):
                k0 = pl.multiple_of(c * QB, QB)
                s_all = _dot_nt(qg, kv_ref[pl.ds(k0, QB), g * D:(g + 1) * D])
                d = qpos - (k0 + lane)
                if br == 0:
                    mask = (_dot(sel, exp_ref[c]) > 0.5) & (d >= 0)
                else:
                    mask = (d >= 0) & (d <= WINDOW)
                for h in range(HG):
                    rows = pl.ds(h * QB, QB)
                    s = jnp.where(mask, s_all[h * QB:(h + 1) * QB], NEG)
                    s_ref[c, rows, :] = s
                    mx_ref[rows, :] = jnp.maximum(mx_ref[rows, :], s)
                return carry

            def values(c, carry, kv_ref=kv_ref):
                k0 = pl.multiple_of(c * QB, QB)
                p = jnp.exp(s_ref[c] - mx_ref[...])
                l_ref[...] += p
                acc_ref[...] += _dot(p.astype(BF), kv_ref[pl.ds(k0, QB), (G + g) * D:(G + g + 1) * D])
                return carry

            lo = 0 if br == 0 else jnp.maximum(qb - WINDOW // QB, 0)
            lax.fori_loop(lo, qb + 1, scores, 0)
            mx_ref[...] = jnp.broadcast_to(jnp.max(mx_ref[...], axis=1, keepdims=True), mx_ref.shape)
            lax.fori_loop(lo, qb + 1, values, 0)
            branch_out.append(acc_ref[...] / jnp.sum(l_ref[...], axis=1, keepdims=True))
        for h in range(HG):
            hh = g * HG + h
            rows = slice(h * QB, (h + 1) * QB)
            outs.append(gates[:, 3 * hh:3 * hh + 1] * o_c[h]
                        + gates[:, 3 * hh + 1:3 * hh + 2] * branch_out[0][rows]
                        + gates[:, 3 * hh + 2:3 * hh + 3] * branch_out[1][rows])
    o_ref[...] = jnp.concatenate(outs, axis=1).astype(BF)


def nsa_prompt(q, gi, kvb, kcv, cover, expand, B, T):
    nqb = T // QB
    nc = kcv.shape[1]
    rows = HGQ = NSA_GROUP * QB
    return pl.pallas_call(
        functools.partial(_nsa_prompt_kernel, n_sel_blocks=T // SEL_LEN),
        grid=(B, nqb),
        in_specs=[pl.BlockSpec((QB, NSA_Q_W), lambda b, i: (b * nqb + i, 0)),
                  pl.BlockSpec((QB, LANES), lambda b, i: (b * nqb + i, 0)),
                  pl.BlockSpec((1, nc, 2 * LANES), lambda b, i: (b, 0, 0)),
                  pl.BlockSpec((T, 2 * LANES), lambda b, i: (b, 1)),
                  pl.BlockSpec((T, 2 * LANES), lambda b, i: (b, 2)),
                  _full(cover.shape), _full(expand.shape)],
        out_specs=pl.BlockSpec((QB, NSA_Q_W), lambda b, i: (b * nqb + i, 0)),
        out_shape=jax.ShapeDtypeStruct((B * T, NSA_Q_W), BF),
        scratch_shapes=[pltpu.VMEM((nqb, rows, QB), F32), pltpu.VMEM((rows, LANES), F32),
                        pltpu.VMEM((rows, LANES), F32), pltpu.VMEM((rows, NSA_HD), F32)],
        compiler_params=_params("parallel", "arbitrary"),
        name="nsa_prompt",
    )(q, gi, kcv, kvb, kvb, cover, expand)


TPAD = 8


def _masked_softmax_parts(parts):
    m = functools.reduce(jnp.maximum, [jnp.max(jnp.where(k, s, NEG), axis=1, keepdims=True) for s, k in parts])
    es = [jnp.where(k, jnp.exp(jnp.where(k, s, NEG) - m), 0.0) for s, k in parts]
    den = functools.reduce(jnp.add, [jnp.sum(e, axis=1, keepdims=True) for e in es])
    return es, jnp.where(den > 0, den, 1.0)


def _nsa_sample_kernel(pt_ref, ct_ref, q_ref, gt_ref, new_ref, wt_ref, pa_ref, pb_ref, w1a_ref, w1b_ref, w2_ref,
                       cover_ref, exp_ref, o_ref, cbuf, sbuf, xk_ref, xv_ref, sh_ref, sem,
                       *, layer, n_pages, page, past_len, n_new):
    G, HG, D = NSA_KV_HEADS, NSA_GROUP, NSA_HD
    b = pl.program_id(0)
    nb = pl.num_programs(0)
    half_rows = 2 * G * D
    bufs = (cbuf, sbuf)

    def page_copy(bb, p, half):
        return pltpu.make_async_copy(
            ct_ref.at[pt_ref[bb, p], layer, pl.ds(half * half_rows, half_rows), :],
            bufs[half].at[:, pl.ds(pl.multiple_of(p * page, page), page)], sem.at[half])

    def start_all(bb, half):
        lax.fori_loop(0, n_pages, lambda p, c: (page_copy(bb, p, half).start(), c)[1], 0)

    def wait_all(bb, half):
        lax.fori_loop(0, n_pages, lambda p, c: (page_copy(bb, p, half).wait(), c)[1], 0)

    @pl.when(b == 0)
    def _():
        start_all(0, 0)
        start_all(0, 1)

    wait_all(b, 0)

    def to_rows(p, c):
        p0 = pl.multiple_of(p * page, page)
        xk_ref[pl.ds(p0, page), :] = cbuf[0:G * D, pl.ds(p0, page)].T
        xv_ref[pl.ds(p0, page), :] = cbuf[G * D:2 * G * D, pl.ds(p0, page)].T
        return c

    lax.fori_loop(0, n_pages, to_rows, 0)

    @pl.when(b + 1 < nb)
    def _():
        start_all(b + 1, 0)

    nsub = past_len // CMP_STRIDE
    kcv = _compress_body(xk_ref, xv_ref, pa_ref, pb_ref, w1a_ref, w1b_ref, w2_ref, sh_ref, nsub)

    rows = HG * TPAD
    t8 = lax.broadcasted_iota(jnp.int32, (TPAD, LANES), 0)
    t_rows = lax.broadcasted_iota(jnp.int32, (rows, LANES), 0) % TPAD
    lane_r = lax.broadcasted_iota(jnp.int32, (rows, LANES), 1)
    new_ok = (lane_r <= t_rows) & (lane_r < n_new)
    tile_h = lambda x: jnp.concatenate([x] * HG, axis=0)

    qs, o_c, sels = [], [], []
    for g in range(G):
        qg = q_ref[0, g] * jnp.asarray(D ** -0.5, BF)
        qs.append(qg)
        kc, vc = kcv[g].astype(BF), kcv[G + g].astype(BF)
        s = _dot_nt(qg, kc)
        c_end = lax.broadcasted_iota(jnp.int32, s.shape, 1) * CMP_STRIDE + (CMP_LEN - 1)
        q_pos = past_len + lax.broadcasted_iota(jnp.int32, s.shape, 0) % TPAD
        (e,), den = _masked_softmax_parts([(s, c_end <= q_pos)])
        p = e / den
        o_c.append(_dot(p.astype(BF), vc))
        psum = functools.reduce(jnp.add, [p[h * TPAD:(h + 1) * TPAD] for h in range(HG)])
        imp = _dot_hl(psum, cover_ref[...])
        cur = (past_len + lax.broadcasted_iota(jnp.int32, imp.shape, 0)) // SEL_LEN
        sels.append(_select_topk(imp, cur, -(-(past_len + n_new) // SEL_LEN)))

    wait_all(b, 1)
    n_past_blocks = past_len // SEL_LEN
    o_s = []
    for g in range(G):
        sel = sels[g]
        selx = _dot(sel[:, :n_past_blocks].astype(BF), exp_ref[...])
        kt = sbuf[g * D:(g + 1) * D, :].astype(BF)
        vt = sbuf[(G + g) * D:(G + g + 1) * D, :].astype(BF)
        s_past = _dot(qs[g], kt)
        s_new = _dot_nt(qs[g], new_ref[0, g, 0])
        in_new = jnp.broadcast_to(sel[:, n_past_blocks:n_past_blocks + 1], (TPAD, LANES)) > 0.5
        (e_past, e_new), den = _masked_softmax_parts(
            [(s_past, tile_h(selx) > 0.5), (s_new, (tile_h(in_new.astype(F32)) > 0.5) & new_ok)])
        o_s.append((_dot_nt(e_past.astype(BF), vt) + _dot(e_new.astype(BF), new_ref[0, g, 1])) / den)

    @pl.when(b + 1 < nb)
    def _():
        start_all(b + 1, 1)

    wb = wt_ref.shape[2]
    for g in range(G):
        kt = wt_ref[0, g * D:(g + 1) * D, :].astype(BF)
        vt = wt_ref[0, (G + g) * D:(G + g + 1) * D, :].astype(BF)
        s_buf = _dot(qs[g], kt)
        i = lax.broadcasted_iota(jnp.int32, s_buf.shape, 1)
        t = lax.broadcasted_iota(jnp.int32, s_buf.shape, 0) % TPAD
        s_new = _dot_nt(qs[g], new_ref[0, g, 2])
        (e_buf, e_new), den = _masked_softmax_parts([(s_buf, wb + t - i <= WINDOW), (s_new, new_ok)])
        o_w = (_dot_nt(e_buf.astype(BF), vt) + _dot(e_new.astype(BF), new_ref[0, g, 3])) / den
        gates = 1.0 / (1.0 + jnp.exp(-gt_ref[0, g]))
        o_ref[0, g] = gates[:, 0:1] * o_c[g] + gates[:, 1:2] * o_s[g] + gates[:, 2:3] * o_w


def nsa_sample(page_table, cache_t, layer, q, gi, kv, win_t, cw, DB, DS):
    G, HG, D = NSA_KV_HEADS, NSA_GROUP, NSA_HD
    n_pages, page = page_table.shape[1], cache_t.shape[3]
    past_len = n_pages * page
    rows = HG * TPAD
    pa, pb, w1a, w1b, w2 = cw
    heads = lambda a, w: a.reshape(DB, DS, G, HG, w).transpose(0, 2, 3, 1, 4)
    padt = lambda a: jnp.pad(a, ((0, 0), (0, 0), (0, 0), (0, TPAD - DS), (0, 0)))
    qs = padt(heads(q, D)).reshape(DB, G, rows, D)
    gts = padt(heads(gi[:, :NSA_G_W], 3)).reshape(DB, G, rows, 3)
    gts = jnp.pad(gts, ((0, 0), (0, 0), (0, 0), (0, LANES - 3)))
    new = kv[:, 2 * G * D:].reshape(DB, DS, 4, G, D).transpose(0, 3, 2, 1, 4)
    new = jnp.pad(new, ((0, 0), (0, 0), (0, 0), (0, LANES - DS), (0, 0))).astype(BF)
    nsub = past_len // CMP_STRIDE
    n_blocks_pad = 2 * LANES
    cover = _cover_matrix(nsub, n_blocks_pad)
    tok = np.arange(past_len)[None, :] // SEL_LEN
    expand = jnp.asarray(np.arange(past_len // SEL_LEN)[:, None] == tok, BF)
    blk = lambda *s: pl.BlockSpec((1,) + s, lambda b, pt: (b,) + (0,) * len(s))
    full = lambda a: pl.BlockSpec(a.shape, lambda b, pt: (0,) * a.ndim)
    out = pl.pallas_call(
        functools.partial(_nsa_sample_kernel, layer=layer, n_pages=n_pages, page=page, past_len=past_len, n_new=DS),
        grid_spec=pltpu.PrefetchScalarGridSpec(
            num_scalar_prefetch=1,
            grid=(DB,),
            in_specs=[pl.BlockSpec(memory_space=pl.ANY), blk(G, rows, D), blk(G, rows, LANES),
                      blk(G, 4, LANES, D), blk(win_t.shape[1], win_t.shape[2]),
                      full(pa), full(pb), full(w1a), full(w1b), full(w2), full(cover), full(expand)],
            out_specs=blk(G, rows, D),
            scratch_shapes=[pltpu.VMEM((2 * G * D, past_len), F32), pltpu.VMEM((2 * G * D, past_len), F32),
                            pltpu.VMEM((past_len, LANES), F32), pltpu.VMEM((past_len, LANES), F32),
                            pltpu.VMEM((nsub + 8, CMP_HIDDEN), F32), pltpu.SemaphoreType.DMA((2,))]),
        out_shape=jax.ShapeDtypeStruct((DB, G, rows, D), F32),
        compiler_params=_params("arbitrary"),
        name="nsa_sample",
    )(page_table, cache_t, qs, gts, new, win_t, pa, pb, w1a, w1b, w2, cover, expand)
    out = out.reshape(DB, G, HG, TPAD, D)[:, :, :, :DS].transpose(0, 3, 1, 2, 4)
    return out.reshape(DB * DS, G * HG * D).astype(BF)


def _memkv_kernel(x_ref, g_ref, w_ref, kg_ref, seg_ref, o_ref, ob_ref):
    z = _dot(_rms(x_ref[...], g_ref[...]).astype(BF), w_ref[...])
    kw = XA_HEADS * XA_HD
    for j in range(2 * kw // LANES):
        sl = slice(j * LANES, (j + 1) * LANES)
        zc = z[:, sl]
        if j * LANES < kw:
            zc = _head_norm(zc, kg_ref[...], seg_ref[...])
        o_ref[:, sl] = zc
        ob_ref[:, sl] = zc.astype(BF)


def memory_kv(mem, g, w, kg, seg, tm=512):
    M, D = mem.shape
    N = w.shape[1]
    row = lambda wd: pl.BlockSpec((tm, wd), lambda i: (i, 0))
    return pl.pallas_call(
        _memkv_kernel,
        grid=(M // tm,),
        in_specs=[row(D), _full((1, D)), _full(w.shape), _full(kg.shape), _full(seg.shape)],
        out_specs=[row(N), row(N)],
        out_shape=[jax.ShapeDtypeStruct((M, N), F32), jax.ShapeDtypeStruct((M, N), BF)],
        compiler_params=_params("parallel"),
        name="memory_kv",
    )(mem, g.reshape(1, D), w, kg, seg)


def _postmix_kernel(x_ref, a_ref, b_ref, wa_ref, wb_ref, gx_ref, wq_ref, qg_ref, seg_ref, mkv_ref, wo_ref, o_ref,
                    *, rows_per_batch, n_mem):
    x1 = x_ref[...] + _dot(a_ref[...], wa_ref[...]) + _dot(b_ref[...], wb_ref[...])
    xn = _rms(x1, gx_ref[...]).astype(BF)
    zq = _dot(xn, wq_ref[...])
    kw = XA_HEADS * XA_HD
    q = jnp.concatenate([_head_norm(zq[:, j * LANES:(j + 1) * LANES], qg_ref[...], seg_ref[...])
                         for j in range(kw // LANES)], axis=1)
    q = (q * XA_HD ** -0.5).astype(BF)
    tm, nk = x1.shape[0], mkv_ref.shape[1]
    mask = None
    if rows_per_batch < tm:
        r = lax.broadcasted_iota(jnp.int32, (tm, nk), 0) // rows_per_batch
        c = lax.broadcasted_iota(jnp.int32, (tm, nk), 1) // n_mem
        mask = r == c
    outs = []
    for h in range(XA_HEADS):
        k = mkv_ref[0, :, h * XA_HD:(h + 1) * XA_HD]
        v = mkv_ref[0, :, kw + h * XA_HD:kw + (h + 1) * XA_HD]
        s = _dot_nt(q[:, h * XA_HD:(h + 1) * XA_HD], k)
        if mask is not None:
            s = jnp.where(mask, s, NEG)
        e = jnp.exp(s - jnp.max(s, axis=1, keepdims=True))
        outs.append(_dot(e.astype(BF), v) / jnp.sum(e, axis=1, keepdims=True))
    o = jnp.concatenate(outs, axis=1).astype(BF)
    o_ref[...] = x1 + _dot(o, wo_ref[...])


def post_mix(x, a, b, wa, wb, gx, wq, qg, seg, mkvb, wo, tm, rows_per_batch):
    M, D = x.shape
    n_mem = mkvb.shape[1]
    if rows_per_batch >= tm:
        per = rows_per_batch // tm
        mspec = pl.BlockSpec((1, n_mem, mkvb.shape[2]), lambda i: (i // per, 0, 0))
    else:
        assert tm == M
        mkvb = mkvb.reshape(1, -1, mkvb.shape[2])
        mspec = _full(mkvb.shape)
    row = lambda wd: pl.BlockSpec((tm, wd), lambda i: (i, 0))
    return pl.pallas_call(
        functools.partial(_postmix_kernel, rows_per_batch=rows_per_batch, n_mem=n_mem),
        grid=(M // tm,),
        in_specs=[row(D), row(a.shape[1]), row(b.shape[1]), _full(wa.shape), _full(wb.shape), _full((1, D)),
                  _full(wq.shape), _full(qg.shape), _full(seg.shape), mspec, _full(wo.shape)],
        out_specs=row(D),
        out_shape=jax.ShapeDtypeStruct((M, D), F32),
        compiler_params=_params("parallel"),
        name="post_mix",
    )(x, a, b, wa, wb, gx.reshape(1, D), wq, qg, seg, mkvb, wo)


def _mlp_kernel(x_ref, g_ref, w1_ref, w2_ref, o_ref, xn_ref, acc_ref):
    f = pl.program_id(1)

    @pl.when(f == 0)
    def _():
        x = x_ref[...]
        xn_ref[...] = _rms(x, g_ref[...]).astype(BF)
        acc_ref[...] = x

    h = _dot(xn_ref[...], w1_ref[...])
    h = jnp.square(jnp.maximum(h, 0.0)).astype(BF)
    acc_ref[...] += _dot(h, w2_ref[...])

    @pl.when(f == pl.num_programs(1) - 1)
    def _():
        o_ref[...] = acc_ref[...]


def mlp_block(x, g, w1b, w2b, tm, tf=512):
    M, D = x.shape
    FF = w1b.shape[1]
    return pl.pallas_call(
        _mlp_kernel,
        grid=(M // tm, FF // tf),
        in_specs=[pl.BlockSpec((tm, D), lambda i, f: (i, 0)),
                  pl.BlockSpec((1, D), lambda i, f: (0, 0)),
                  pl.BlockSpec((D, tf), lambda i, f: (0, f)),
                  pl.BlockSpec((tf, D), lambda i, f: (f, 0))],
        out_specs=pl.BlockSpec((tm, D), lambda i, f: (i, 0)),
        out_shape=jax.ShapeDtypeStruct((M, D), F32),
        scratch_shapes=[pltpu.VMEM((tm, D), BF), pltpu.VMEM((tm, D), F32)],
        compiler_params=_params("parallel", "arbitrary"),
        name="mlp",
    )(x, g.reshape(1, D), w1b, w2b)


ML_CHUNK = 128
CONV_PAD = 8


def _dot3(a, b):
    hi = b.astype(BF)
    r1 = b - hi.astype(F32)
    mid = r1.astype(BF)
    lo = (r1 - mid.astype(F32)).astype(BF)
    return _dot(a, hi) + _dot(a, mid) + _dot(a, lo)


def _mlstm_kernel(zqk_ref, zv_ref, zo_ref, gi_ref, cw_ref, cb_ref, gb_ref, hg_ref, tril_ref,
                  c0_ref, n0_ref, m0_ref, cv0_ref,
                  h_out, c_out, n_out, m_out, cv_out,
                  xin_ref, c_ref, n_ref, m_ref, *, n_valid):
    L = ML_CHUNK
    ci = pl.program_id(1)

    @pl.when(ci == 0)
    def _():
        c_ref[...] = c0_ref[0]
        n_ref[...] = n0_ref[0]
        m_ref[...] = m0_ref[0]
        xin_ref[pl.ds(0, CONV_PAD), :] = cv0_ref[0]

    xin_ref[pl.ds(CONV_PAD, L), :] = zqk_ref[...]
    y = cb_ref[...]
    for j in range(CONV_W):
        y = y + cw_ref[j:j + 1, :] * xin_ref[pl.ds(CONV_PAD - (CONV_W - 1) + j, L), :]
    qk = y * (1.0 / (1.0 + jnp.exp(-y)))
    hist = xin_ref[pl.ds(n_valid, CONV_PAD), :]
    xin_ref[pl.ds(0, CONV_PAD), :] = hist
    cv_out[0] = hist

    row = lax.broadcasted_iota(jnp.int32, (L, L), 0)
    col = lax.broadcasted_iota(jnp.int32, (L, L), 1)
    valid = row < n_valid
    gl = gi_ref[...] + gb_ref[...]
    for h in range(ML_HEADS):
        hs = slice(h * ML_HD, (h + 1) * ML_HD)
        q = qk[:, hs]
        k = qk[:, ML_W + h * ML_HD:ML_W + (h + 1) * ML_HD] * (ML_HD ** -0.5)
        v = zv_ref[:, hs]
        gi_col = jnp.broadcast_to(gl[:, NSA_G_W + h:NSA_G_W + h + 1], (L, L))
        gf_col = jnp.broadcast_to(gl[:, NSA_G_W + ML_HEADS + h:NSA_G_W + ML_HEADS + h + 1], (L, L))
        li = jnp.where(valid, gi_col, NEG)
        lf = jnp.where(valid, jnp.minimum(gf_col, 0.0) - jnp.log(1.0 + jnp.exp(-jnp.abs(gf_col))), 0.0)
        b = _dot3(tril_ref[...], lf)
        m_prev = m_ref[h:h + 1, :]
        dm = jnp.where(col <= row, b - (b - li).T, NEG)
        inter = b + m_prev
        m_t = jnp.maximum(inter, jnp.max(dm, axis=1, keepdims=True))
        w = jnp.exp(dm - m_t)
        a = jnp.exp(inter - m_t)
        qb, kb, vb = q.astype(BF), k.astype(BF), v.astype(BF)
        wqk = w * _dot_nt(qb, kb)
        c_old = c_ref[h]
        n_old = n_ref[h:h + 1, :]
        num = a * _dot_nt(qb, c_old.astype(BF)) + _dot(wqk.astype(BF), vb)
        den = a * jnp.sum(q * n_old, axis=1, keepdims=True) + jnp.sum(wqk, axis=1, keepdims=True)
        hh = num / jnp.maximum(jnp.abs(den), jnp.exp(-m_t))
        m_new = m_t[L - 1:L, :]
        b_last = b[L - 1:L, :]
        wk = jnp.exp(b_last - b + li - m_new) * k
        decay = jnp.exp(b_last + m_prev - m_new)
        c_ref[h] = decay * c_old + _dot(v.T.astype(BF), wk.astype(BF))
        n_ref[h:h + 1, :] = decay * n_old + jnp.sum(wk, axis=0, keepdims=True)
        m_ref[h:h + 1, :] = m_new
        hn = hh * lax.rsqrt(jnp.mean(hh * hh, axis=1, keepdims=True) + EPS) * hg_ref[...]
        h_out[:, hs] = (hn * (1.0 / (1.0 + jnp.exp(-zo_ref[:, hs])))).astype(BF)

    @pl.when(ci == pl.num_programs(1) - 1)
    def _():
        c_out[0] = c_ref[...]
        n_out[0] = n_ref[...]
        m_out[0] = m_ref[...]


def mlstm(zqk, zv, zo, gi, conv_w, conv_b, gate_b, hn_g, state, nb, n_valid):
    M = zqk.shape[0]
    nch = M // nb // ML_CHUNK
    assert n_valid == ML_CHUNK or nch == 1
    C0, n0, m0, cv0 = state
    m0 = jnp.broadcast_to(m0[:, :, None], (nb, ML_HEADS, LANES))
    cv0 = jnp.pad(cv0, ((0, 0), (CONV_PAD - (CONV_W - 1), 0), (0, 0)))
    gb = jnp.zeros((1, LANES), F32).at[0, NSA_G_W:NSA_G_W + 2 * ML_HEADS].set(gate_b)
    tril = jnp.asarray(np.tril(np.ones((ML_CHUNK, ML_CHUNK))), BF)
    row = lambda wd: pl.BlockSpec((ML_CHUNK, wd), lambda b, c: (b * nch + c, 0))
    st = lambda *shape: pl.BlockSpec((1,) + shape, lambda b, c: (b,) + (0,) * len(shape))
    h, Cn, nn, mn, cvn = pl.pallas_call(
        functools.partial(_mlstm_kernel, n_valid=n_valid),
        grid=(nb, nch),
        in_specs=[row(2 * ML_W), row(ML_W), row(ML_W), row(LANES),
                  _full(conv_w.shape), _full((1, 2 * ML_W)), _full((1, LANES)), _full((1, ML_HD)), _full(tril.shape),
                  st(ML_HEADS, ML_HD, ML_HD), st(ML_HEADS, ML_HD), st(ML_HEADS, LANES), st(CONV_PAD, 2 * ML_W)],
        out_specs=[row(ML_W), st(ML_HEADS, ML_HD, ML_HD), st(ML_HEADS, ML_HD), st(ML_HEADS, LANES),
                   st(CONV_PAD, 2 * ML_W)],
        out_shape=[jax.ShapeDtypeStruct((M, ML_W), BF),
                   jax.ShapeDtypeStruct((nb, ML_HEADS, ML_HD, ML_HD), F32),
                   jax.ShapeDtypeStruct((nb, ML_HEADS, ML_HD), F32),
                   jax.ShapeDtypeStruct((nb, ML_HEADS, LANES), F32),
                   jax.ShapeDtypeStruct((nb, CONV_PAD, 2 * ML_W), F32)],
        scratch_shapes=[pltpu.VMEM((CONV_PAD + ML_CHUNK, 2 * ML_W), F32),
                        pltpu.VMEM((ML_HEADS, ML_HD, ML_HD), F32),
                        pltpu.VMEM((ML_HEADS, ML_HD), F32),
                        pltpu.VMEM((ML_HEADS, LANES), F32)],
        compiler_params=_params("parallel", "arbitrary"),
        name="mlstm",
    )(zqk, zv, zo, gi, conv_w, conv_b.reshape(1, -1), gb, hn_g.reshape(1, -1), tril, C0, n0, m0, cv0)
    return h, (Cn, nn, mn[:, :, 0], cvn[:, CONV_PAD - (CONV_W - 1):])


def _rope_tables(pos, rows):
    half = ROPE_DIMS // 2
    freqs = ROPE_THETA ** (-jnp.arange(half, dtype=F32) / half)
    ang = pos.astype(F32)[:, None] * freqs
    cos, sin = jnp.cos(ang), jnp.sin(ang)
    n = pos.shape[0]
    one, zero = jnp.ones((n, NSA_HD - ROPE_DIMS), F32), jnp.zeros((n, NSA_HD - ROPE_DIMS), F32)
    zh = jnp.zeros((n, half), F32)
    c = jnp.concatenate([cos, cos, one], axis=1)
    sa = jnp.concatenate([-sin, zh, zero], axis=1)
    sb = jnp.concatenate([zh, sin, zero], axis=1)
    tile = lambda t: jnp.tile(t, (rows // n, LANES // NSA_HD))
    return tile(c), tile(sa), tile(sb)


def _seg_matrix():
    i = np.arange(LANES)
    return jnp.asarray((i[:, None] // NSA_HD == i[None, :] // NSA_HD) / NSA_HD, BF)


def _cover_matrix(nc_rows, lanes):
    ci = np.arange(nc_rows)[:, None] * CMP_STRIDE
    sj = np.arange(lanes)[None, :]
    return jnp.asarray((ci < (sj + 1) * SEL_LEN) & (ci + CMP_LEN > sj * SEL_LEN), BF)


def _expand_matrix(n_chunks):
    j = np.arange(LANES)[None, :, None]
    c = np.arange(n_chunks)[:, None, None]
    s = np.arange(QB)[None, None, :]
    return jnp.asarray(j == c * (QB // SEL_LEN) + s // SEL_LEN, BF)


def _in_weights(w):
    zq, zkv, zg, zqk, zv, zo, zif = jnp.split(w, np.cumsum(COL_SPLITS)[:-1].tolist(), axis=1)
    pad = jnp.zeros((w.shape[0], LANES - NSA_G_W - 2 * ML_HEADS), w.dtype)
    return tuple(a.astype(BF) for a in (zq, zkv, zqk, zv, zo, jnp.concatenate([zg, zif, pad], axis=1)))


def _compress_weights(cmp_pos, cmp_w1, cmp_w2):
    G, D, S = NSA_KV_HEADS, NSA_HD, CMP_STRIDE
    pos = jnp.tile(cmp_pos[:, :, None, :], (1, 1, G, 1))
    pos = pos.transpose(1, 0, 2, 3).reshape(CMP_LEN, 2 * G * D)
    w1 = cmp_w1.reshape(2, CMP_LEN, D, CMP_HIDDEN)
    z = jnp.zeros_like(w1)
    w1g = jnp.stack([jnp.concatenate([w1, z], axis=2), jnp.concatenate([z, w1], axis=2)], axis=1)
    return (pos[:S], pos[S:], w1g[:, :, :S].astype(BF), w1g[:, :, S:].astype(BF), cmp_w2.astype(BF))


def kernel(x_prompt, x_sample, mem_prompt, cache_nsa_kv, cache_win_kv, cache_mem_kv, state_mlstm_C, state_mlstm_n, state_mlstm_m, state_conv, page_table, norm_mix_g, w_in, nsa_qn_g, nsa_kn_g, cmp_pos, cmp_w1, cmp_w2, ml_conv_w, ml_conv_b, ml_gate_b, ml_hn_g, w_out, norm_xa_g, norm_mem_g, xa_wq, xa_wkv, xa_qn_g, xa_kn_g, xa_wo, norm_mlp_g, mlp_w1, mlp_w2):
    B, T, D = x_prompt.shape
    DB, DS, _ = x_sample.shape
    depth = w_in.shape[0]
    n_mem = mem_prompt.shape[1]
    P = page_table.shape[1] * cache_nsa_kv.shape[1]
    wkeep = min(WINDOW, T)
    G, HD = NSA_KV_HEADS, NSA_HD
    MP, MS = B * T, DB * DS
    tm_p, tm_s = 512, MS

    pos_p = jnp.arange(T, dtype=jnp.int32)
    pos_s = P + jnp.arange(DS, dtype=jnp.int32)
    tabs_p = _rope_tables(pos_p, T)
    tabs_s = _rope_tables(pos_s, MS)
    seg = _seg_matrix()
    cover = _cover_matrix(T // CMP_STRIDE, LANES)
    expand = _expand_matrix(T // QB)
    tile2 = lambda v: jnp.tile(v.reshape(-1, HD), (1, LANES // HD))
    pad_chunk = lambda a: jnp.pad(a.reshape(DB, DS, -1), ((0, 0), (0, ML_CHUNK - DS), (0, 0))).reshape(DB * ML_CHUNK, -1)

    n_pool, page = cache_nsa_kv.shape[:2]
    cache_t = cache_nsa_kv.transpose(0, 2, 3, 4, 5, 1).reshape(n_pool, depth, 4 * G * HD, page)
    win_t = cache_win_kv.transpose(0, 1, 3, 4, 5, 2).reshape(depth, DB, 2 * G * HD, cache_win_kv.shape[2])

    xp = x_prompt.reshape(MP, D)
    xs = x_sample.reshape(MS, D)
    mem = mem_prompt.reshape(B * n_mem, D)
    nsa_p, nsa_s, win_p, win_s = [], [], [], []
    C_p, C_s, n_p, n_s, m_p, m_s = [], [], [], [], [], []
    cv_p, cv_s, mem_p = [], [], []
    for l in range(depth):
        w_l = _in_weights(w_in[l])
        qg, kg = tile2(nsa_qn_g[l]), tile2(nsa_kn_g[l])
        cw = _compress_weights(cmp_pos[l], cmp_w1[l], cmp_w2[l])
        wa, wb = w_out[l, :NSA_Q_W].astype(BF), w_out[l, NSA_Q_W:].astype(BF)
        xq, xo = xa_wq[l].astype(BF), xa_wo[l].astype(BF)
        xqg = tile2(xa_qn_g[l])
        w1b, w2b = mlp_w1[l].astype(BF), mlp_w2[l].astype(BF)

        q, kv, kvb, zqk, zv, zo, gi = in_proj(xp, norm_mix_g[l], w_l, tabs_p, qg, kg, seg, tm_p)
        kcv = compress_prompt(kv, cw, B, T)
        a_p = nsa_prompt(q, gi, kvb, kcv, cover, expand, B, T)
        kv3 = kv.reshape(B, T, NSA_KV_W)
        nsa_p.append(kv3[:, :, :4 * G * HD].reshape(B, T, 4, G, HD))
        win_p.append(kv3[:, T - wkeep:, 4 * G * HD:].reshape(B, wkeep, 2, G, HD))
        st0 = (jnp.zeros((B, ML_HEADS, ML_HD, ML_HD), F32), jnp.zeros((B, ML_HEADS, ML_HD), F32),
               jnp.zeros((B, ML_HEADS), F32), jnp.zeros((B, CONV_W - 1, 2 * ML_W), F32))
        b_p, (Cn, nn, mn, buf_p) = mlstm(zqk, zv, zo, gi, ml_conv_w[l], ml_conv_b[l], ml_gate_b[l], ml_hn_g[l],
                                         st0, B, ML_CHUNK)
        C_p.append(Cn); n_p.append(nn); m_p.append(mn); cv_p.append(buf_p)
        mkv, mkvb = memory_kv(mem, norm_mem_g[l], xa_wkv[l].astype(BF), tile2(xa_kn_g[l]), seg)
        mem_p.append(mkv.reshape(B, n_mem, 2, XA_HEADS, XA_HD))
        xp = post_mix(xp, a_p, b_p, wa, wb, norm_xa_g[l], xq, xqg, seg, mkvb.reshape(B, n_mem, -1), xo, tm_p, T)
        xp = mlp_block(xp, norm_mlp_g[l], w1b, w2b, tm=1024)

        q, kv, kvb, zqk, zv, zo, gi = in_proj(xs, norm_mix_g[l], w_l, tabs_s, qg, kg, seg, tm_s)
        kvp = kv.reshape(DB, DS, 3, 2, G, HD)
        a_s = nsa_sample(page_table, cache_t, l, q, gi, kv, win_t[l], cw, DB, DS)
        nsa_s.append(kvp[:, :, :2].reshape(DB, DS, 4, G, HD))
        win_s.append(jnp.concatenate([cache_win_kv[l], kvp[:, :, 2]], axis=1)[:, -cache_win_kv.shape[2]:])
        st = (state_mlstm_C[l], state_mlstm_n[l], state_mlstm_m[l], state_conv[l])
        b_s, (Cn, nn, mn, buf_s) = mlstm(*(pad_chunk(a) for a in (zqk, zv, zo, gi)), ml_conv_w[l], ml_conv_b[l],
                                         ml_gate_b[l], ml_hn_g[l], st, DB, DS)
        b_s = b_s.reshape(DB, ML_CHUNK, ML_W)[:, :DS].reshape(MS, ML_W)
        C_s.append(Cn); n_s.append(nn); m_s.append(mn); cv_s.append(buf_s)
        mkvb_s = cache_mem_kv[l].reshape(DB, n_mem, -1).astype(BF)
        xs = post_mix(xs, a_s, b_s, wa, wb, norm_xa_g[l], xq, xqg, seg, mkvb_s, xo, tm_s, DS)
        xs = mlp_block(xs, norm_mlp_g[l], w1b, w2b, tm=MS)
    return (xp.reshape(B, T, D), xs.reshape(DB, DS, D),
            jnp.stack(nsa_p, axis=2), jnp.stack(nsa_s, axis=2),
            jnp.stack(win_p), jnp.stack(win_s),
            jnp.stack(C_p), jnp.stack(C_s),
            jnp.stack(n_p), jnp.stack(n_s),
            jnp.stack(m_p), jnp.stack(m_s),
            jnp.stack(cv_p), jnp.stack(cv_s),
            jnp.stack(mem_p))
```

```python
import functools

import jax
import jax.numpy as jnp
import numpy as np
from jax import lax
from jax.experimental import pallas as pl
from jax.experimental.pallas import tpu as pltpu

F32 = jnp.float32
BF = jnp.bfloat16

EPS = 1e-6
NSA_HEADS, NSA_KV_HEADS, NSA_HD = 8, 2, 64
NSA_GROUP = NSA_HEADS // NSA_KV_HEADS
CMP_LEN, CMP_STRIDE, CMP_HIDDEN = 32, 16, 128
SEL_LEN, N_SEL, WINDOW = 64, 8, 512
ROPE_DIMS, ROPE_THETA = NSA_HD // 4, 500000.0
ML_HEADS, ML_HD, CONV_W = 4, 128, 4
ML_W = ML_HEADS * ML_HD
XA_HEADS, XA_HD = 4, 64
LANES = 128
QB = 128
VMEM_LIMIT = 56 * 1024 * 1024

NSA_Q_W = NSA_HEADS * NSA_HD
NSA_KV_W = 6 * NSA_KV_HEADS * NSA_HD
NSA_G_W = 3 * NSA_HEADS
COL_SPLITS = (NSA_Q_W, NSA_KV_W, NSA_G_W, 2 * ML_W, ML_W, ML_W, 2 * ML_HEADS)
NEG = -1e30


def _dot(a, b):
    return jnp.dot(a, b, preferred_element_type=F32)


def _dot_nt(a, b):
    return lax.dot_general(a, b, (((1,), (1,)), ((), ())), preferred_element_type=F32)


def _dot_hl(a, b):
    hi = a.astype(BF)
    lo = (a - hi.astype(F32)).astype(BF)
    return _dot(hi, b) + _dot(lo, b)


def _rms(x, g):
    return x * lax.rsqrt(jnp.mean(x * x, axis=-1, keepdims=True) + EPS) * g


def _head_norm(z, g, seg):
    return z * lax.rsqrt(_dot_hl(z * z, seg) + EPS) * g


def _rope(y, c, sa, sb):
    half = ROPE_DIMS // 2
    return y * c + pltpu.roll(y, LANES - half, 1) * sa + pltpu.roll(y, half, 1) * sb


def _params(*sem):
    return pltpu.CompilerParams(dimension_semantics=sem, vmem_limit_bytes=VMEM_LIMIT)


def _full(shape):
    n = len(shape)
    return pl.BlockSpec(shape, lambda *_: (0,) * n)


def _inproj_kernel(x_ref, g_ref, wq_ref, wkv_ref, wqk_ref, wv_ref, wo_ref, wgi_ref,
                   rc_ref, rsa_ref, rsb_ref, qg_ref, kg_ref, seg_ref,
                   q_out, kv_out, kb_out, va_out, zqk_out, zv_out, zo_out, gi_out):
    D = NSA_HD
    xn = _rms(x_ref[...], g_ref[...]).astype(BF)
    c, sa, sb = rc_ref[...], rsa_ref[...], rsb_ref[...]
    seg = seg_ref[...]
    lane = lax.broadcasted_iota(jnp.int32, c.shape, 1)
    low = lane < D
    swap = lambda z: pltpu.roll(z, D, 1)
    zq = _dot(xn, wq_ref[...])
    for j in range(NSA_Q_W // LANES):
        y = _rope(_head_norm(zq[:, j * LANES:(j + 1) * LANES], qg_ref[...], seg), c, sa, sb) * D ** -0.5
        ys = swap(y)
        for i in range(2):
            h = 2 * j + i
            want_low = h // NSA_GROUP == 0
            src = y if want_low == (i == 0) else ys
            q_out[:, h * LANES:(h + 1) * LANES] = jnp.where(low if want_low else ~low, src, 0.0).astype(BF)
    zkv = _dot(xn, wkv_ref[...])
    for j in range(NSA_KV_W // LANES):
        sl = slice(j * LANES, (j + 1) * LANES)
        z = zkv[:, sl]
        br = j // 2
        if j % 2 == 0:
            z = _rope(_head_norm(z, kg_ref[br:br + 1, :], seg), c, sa, sb)
            if br > 0:
                kb_out[:, (br - 1) * LANES:br * LANES] = z.astype(BF)
        elif br > 0:
            one = jnp.where(lane == D, 1.0, 0.0)
            for g, src in enumerate((z, swap(z))):
                col = ((br - 1) * NSA_KV_HEADS + g) * LANES
                va_out[:, col:col + LANES] = jnp.where(low, src, one).astype(BF)
        kv_out[:, sl] = z
    zqk_out[...] = _dot(xn, wqk_ref[...])
    zv_out[...] = _dot(xn, wv_ref[...])
    zo_out[...] = _dot(xn, wo_ref[...])
    gi_out[...] = _dot(xn, wgi_ref[...])


def in_proj(x, g, w, rope_tabs, qg, kg, seg, tm):
    M, D = x.shape
    rc, rsa, rsb = rope_tabs
    nt = rc.shape[0] // tm
    widths = (NSA_HEADS * LANES, NSA_KV_W, 2 * LANES, 4 * LANES, 2 * ML_W, ML_W, ML_W, LANES)
    dtypes = (BF, F32, BF, BF, F32, F32, F32, F32)
    row = lambda wd: pl.BlockSpec((tm, wd), lambda i: (i, 0))
    tab = pl.BlockSpec((tm, LANES), lambda i: (i % nt, 0))
    return pl.pallas_call(
        _inproj_kernel,
        grid=(M // tm,),
        in_specs=[row(D), _full((1, D))] + [_full(a.shape) for a in w] + [tab, tab, tab,
                  _full(qg.shape), _full(kg.shape), _full(seg.shape)],
        out_specs=[row(wd) for wd in widths],
        out_shape=[jax.ShapeDtypeStruct((M, wd), dt) for wd, dt in zip(widths, dtypes)],
        compiler_params=_params("parallel"),
        name="in_proj",
    )(x, g.reshape(1, D), *w, rc, rsa, rsb, qg, kg, seg)


def _compress_body(xk_ref, xv_ref, pa_ref, pb_ref, w1a_ref, w1b_ref, w2_ref, sh_ref, nsub):
    G = NSA_KV_HEADS
    acc_a = [jnp.zeros((nsub, CMP_HIDDEN), F32) for _ in range(2 * G)]
    acc_b = [jnp.zeros((nsub, CMP_HIDDEN), F32) for _ in range(2 * G)]
    for u in range(CMP_STRIDE):
        for kind, x_ref in enumerate((xk_ref, xv_ref)):
            sl = slice(kind * LANES, (kind + 1) * LANES)
            xu = x_ref[pl.ds(u, nsub, stride=CMP_STRIDE), :]
            xa = (xu + pa_ref[u:u + 1, sl]).astype(BF)
            xb = (xu + pb_ref[u:u + 1, sl]).astype(BF)
            for g in range(G):
                acc_a[kind * G + g] += _dot(xa, w1a_ref[kind, g, u])
                acc_b[kind * G + g] += _dot(xb, w1b_ref[kind, g, u])
    sh_ref[pl.ds(nsub, 8), :] = jnp.zeros((8, CMP_HIDDEN), F32)
    outs = []
    for kind in range(2):
        for g in range(G):
            sh_ref[pl.ds(0, nsub), :] = acc_b[kind * G + g]
            h = acc_a[kind * G + g] + sh_ref[pl.ds(1, nsub), :]
            h = h * (1.0 / (1.0 + jnp.exp(-h)))
            outs.append(_dot(h.astype(BF), w2_ref[kind]))
    return outs


def _compress_kernel(xk_ref, xv_ref, pa_ref, pb_ref, w1a_ref, w1b_ref, w2_ref, o_ref, sh_ref, *, nsub):
    kc0, kc1, vc0, vc1 = _compress_body(xk_ref, xv_ref, pa_ref, pb_ref, w1a_ref, w1b_ref, w2_ref, sh_ref, nsub)
    zero = jnp.zeros_like(vc0)
    for j, o in enumerate((kc0, kc1, vc0, zero, vc1, zero)):
        o_ref[0, :, j * NSA_HD:(j + 1) * NSA_HD] = o


def compress_prompt(kv, cw, B, T):
    pa, pb, w1a, w1b, w2 = cw
    nsub = T // CMP_STRIDE
    return pl.pallas_call(
        functools.partial(_compress_kernel, nsub=nsub),
        grid=(B,),
        in_specs=[pl.BlockSpec((T, LANES), lambda b: (b, 0)), pl.BlockSpec((T, LANES), lambda b: (b, 1)),
                  _full(pa.shape), _full(pb.shape), _full(w1a.shape), _full(w1b.shape), _full(w2.shape)],
        out_specs=pl.BlockSpec((1, nsub, 3 * LANES), lambda b: (b, 0, 0)),
        out_shape=jax.ShapeDtypeStruct((B, nsub, 3 * LANES), F32),
        scratch_shapes=[pltpu.VMEM((nsub + 8, CMP_HIDDEN), F32)],
        compiler_params=_params("parallel"),
        name="compress_prompt",
    )(kv, kv, pa, pb, w1a, w1b, w2)


def _select_topk(imp, cur, n_blocks):
    j = lax.broadcasted_iota(jnp.int32, imp.shape, 1)
    jf = j.astype(F32)
    forced = (j == cur) | (j == 0)
    dead = (j > cur) | (j >= n_blocks)
    val = jnp.where(forced, jnp.inf, jnp.where(dead, -jnp.inf, imp))
    sel = jnp.zeros(imp.shape, F32)
    for _ in range(N_SEL):
        m = jnp.max(val, axis=1, keepdims=True)
        idx = jnp.min(jnp.where(val == m, jf, float(imp.shape[1])), axis=1, keepdims=True)
        hit = jf == idx
        sel = jnp.where(hit & (m > -jnp.inf), 1.0, sel)
        val = jnp.where(hit, -jnp.inf, val)
    return sel


KCH = 2 * QB


def _nsa_prompt_kernel(q_ref, gi_ref, kcv_ref, kb_ref, va_ref, cover_ref, exp_ref, o_ref,
                       s_ref, mx_ref, acc_ref, *, n_sel_blocks):
    G, HG, D, H = NSA_KV_HEADS, NSA_GROUP, NSA_HD, NSA_HEADS
    qb = pl.program_id(1)
    q0 = qb * QB
    qpos = q0 + lax.broadcasted_iota(jnp.int32, (QB, LANES), 0)
    lane = lax.broadcasted_iota(jnp.int32, (QB, LANES), 1)
    gates = 1.0 / (1.0 + jnp.exp(-gi_ref[...]))
    nc = kcv_ref.shape[1]
    qp = jnp.concatenate([q_ref[:, h * LANES:(h + 1) * LANES] for h in range(H)], axis=0)

    s_all = _dot_nt(qp, kcv_ref[0, :, 0:LANES].astype(BF))
    cmask = (lane[:, :nc] * CMP_STRIDE + (CMP_LEN - 1)) <= qpos[:, :nc]
    o_c, psums = [], []
    for g in range(G):
        vc = kcv_ref[0, :, (1 + g) * LANES:(2 + g) * LANES].astype(BF)
        psum = jnp.zeros((QB, nc), F32)
        for h in range(HG):
            hh = g * HG + h
            s = jnp.where(cmask, s_all[hh * QB:(hh + 1) * QB], NEG)
            e = jnp.where(cmask, jnp.exp(s - jnp.max(s, axis=1, keepdims=True)), 0.0)
            den = jnp.sum(e, axis=1, keepdims=True)
            p = e / jnp.where(den > 0, den, 1.0)
            psum = psum + p
            o_c.append(_dot(p.astype(BF), vc))
        psums.append(psum)
    imp = _dot_hl(jnp.concatenate(psums, axis=0), cover_ref[...])
    sel_all = _select_topk(imp, jnp.concatenate([qpos // SEL_LEN] * G, axis=0), n_sel_blocks).astype(BF)
    sels = [sel_all[g * QB:(g + 1) * QB] for g in range(G)]

    qpos2 = q0 + lax.broadcasted_iota(jnp.int32, (QB, KCH), 0)
    lane2 = lax.broadcasted_iota(jnp.int32, (QB, KCH), 1)
    branch_out = []
    for br in range(2):
        mx_ref[...] = jnp.full(mx_ref.shape, NEG, F32)
        acc_ref[...] = jnp.zeros(acc_ref.shape, F32)

        def scores(c, carry, br=br):
            k0 = pl.multiple_of(c * KCH, KCH)
            s_all = _dot_nt(qp, kb_ref[pl.ds(k0, KCH), br * LANES:(br + 1) * LANES])
            d = qpos2 - (k0 + lane2)
            if br == 0:
                masks = [(_dot(sels[g], exp_ref[c]) > 0.5) & (d >= 0) for g in range(G)]
            else:
                masks = [(d >= 0) & (d <= WINDOW)] * G
            for hh in range(H):
                rows = pl.ds(hh * QB, QB)
                s = jnp.where(masks[hh // HG], s_all[hh * QB:(hh + 1) * QB], NEG)
                s_ref[c, rows, :] = s
                mx_ref[rows, :] = jnp.maximum(mx_ref[rows, :], jnp.maximum(s[:, :LANES], s[:, LANES:]))
            return carry

        def values(c, carry, br=br):
            k0 = pl.multiple_of(c * KCH, KCH)
            for g in range(G):
                rows = pl.ds(g * HG * QB, HG * QB)
                mx = mx_ref[rows, :]
                p = jnp.concatenate([jnp.exp(s_ref[c, rows, pl.ds(i * LANES, LANES)] - mx)
                                     for i in range(KCH // LANES)], axis=1).astype(BF)
                acc_ref[rows, :] += _dot(p, va_ref[pl.ds(k0, KCH), (br * G + g) * LANES:(br * G + g + 1) * LANES])
            return carry

        lo = 0 if br == 0 else jnp.maximum(q0 - WINDOW, 0) // KCH
        hi = (q0 + QB - 1) // KCH + 1
        lax.fori_loop(lo, hi, scores, 0)
        mx_ref[...] = jnp.broadcast_to(jnp.max(mx_ref[...], axis=1, keepdims=True), mx_ref.shape)
        lax.fori_loop(lo, hi, values, 0)
        r = acc_ref[...]
        branch_out.append(r / r[:, D:D + 1])

    for j in range(H // 2):
        tiles = []
        for hh in (2 * j, 2 * j + 1):
            rows = slice(hh * QB, (hh + 1) * QB)
            tiles.append(gates[:, 3 * hh:3 * hh + 1] * o_c[hh]
                         + gates[:, 3 * hh + 1:3 * hh + 2] * branch_out[0][rows]
                         + gates[:, 3 * hh + 2:3 * hh + 3] * branch_out[1][rows])
        o_ref[:, j * LANES:(j + 1) * LANES] = jnp.where(lane < D, tiles[0], pltpu.roll(tiles[1], D, 1)).astype(BF)


def nsa_prompt(q, gi, kb, va, kcv, cover, expand, B, T):
    nqb = T // QB
    nc = kcv.shape[1]
    rows = NSA_HEADS * QB
    return pl.pallas_call(
        functools.partial(_nsa_prompt_kernel, n_sel_blocks=T // SEL_LEN),
        grid=(B, nqb),
        in_specs=[pl.BlockSpec((QB, q.shape[1]), lambda b, i: (b * nqb + i, 0)),
                  pl.BlockSpec((QB, LANES), lambda b, i: (b * nqb + i, 0)),
                  pl.BlockSpec((1, nc, kcv.shape[2]), lambda b, i: (b, 0, 0)),
                  pl.BlockSpec((T, kb.shape[1]), lambda b, i: (b, 0)),
                  pl.BlockSpec((T, va.shape[1]), lambda b, i: (b, 0)),
                  _full(cover.shape), _full(expand.shape)],
        out_specs=pl.BlockSpec((QB, NSA_Q_W), lambda b, i: (b * nqb + i, 0)),
        out_shape=jax.ShapeDtypeStruct((B * T, NSA_Q_W), BF),
        scratch_shapes=[pltpu.VMEM((T // KCH, rows, KCH), F32), pltpu.VMEM((rows, LANES), F32),
                        pltpu.VMEM((rows, LANES), F32)],
        compiler_params=_params("parallel", "arbitrary"),
        name="nsa_prompt",
    )(q, gi, kcv, kb, va, cover, expand)


TPAD = 8


def _masked_softmax_parts(parts):
    m = functools.reduce(jnp.maximum, [jnp.max(jnp.where(k, s, NEG), axis=1, keepdims=True) for s, k in parts])
    es = [jnp.where(k, jnp.exp(jnp.where(k, s, NEG) - m), 0.0) for s, k in parts]
    den = functools.reduce(jnp.add, [jnp.sum(e, axis=1, keepdims=True) for e in es])
    return es, jnp.where(den > 0, den, 1.0)


def _nsa_sample_kernel(pt_ref, ct_ref, q_ref, gt_ref, new_ref, wt_ref, pa_ref, pb_ref, w1a_ref, w1b_ref, w2_ref,
                       cover_ref, exp_ref, o_ref, cbuf, sbuf, xk_ref, xv_ref, sh_ref, sem,
                       *, layer, n_pages, page, past_len, n_new):
    G, HG, D = NSA_KV_HEADS, NSA_GROUP, NSA_HD
    b = pl.program_id(0)
    nb = pl.num_programs(0)
    half_rows = 2 * G * D
    bufs = (cbuf, sbuf)

    def page_copy(bb, p, half):
        return pltpu.make_async_copy(
            ct_ref.at[pt_ref[bb, p], layer, pl.ds(half * half_rows, half_rows), :],
            bufs[half].at[:, pl.ds(pl.multiple_of(p * page, page), page)], sem.at[half])

    def start_all(bb, half):
        lax.fori_loop(0, n_pages, lambda p, c: (page_copy(bb, p, half).start(), c)[1], 0)

    def wait_all(bb, half):
        lax.fori_loop(0, n_pages, lambda p, c: (page_copy(bb, p, half).wait(), c)[1], 0)

    @pl.when(b == 0)
    def _():
        start_all(0, 0)
        start_all(0, 1)

    wait_all(b, 0)

    def to_rows(p, c):
        p0 = pl.multiple_of(p * page, page)
        xk_ref[pl.ds(p0, page), :] = cbuf[0:G * D, pl.ds(p0, page)].T
        xv_ref[pl.ds(p0, page), :] = cbuf[G * D:2 * G * D, pl.ds(p0, page)].T
        return c

    lax.fori_loop(0, n_pages, to_rows, 0)

    @pl.when(b + 1 < nb)
    def _():
        start_all(b + 1, 0)

    nsub = past_len // CMP_STRIDE
    kcv = _compress_body(xk_ref, xv_ref, pa_ref, pb_ref, w1a_ref, w1b_ref, w2_ref, sh_ref, nsub)

    rows = HG * TPAD
    t8 = lax.broadcasted_iota(jnp.int32, (TPAD, LANES), 0)
    t_rows = lax.broadcasted_iota(jnp.int32, (rows, LANES), 0) % TPAD
    lane_r = lax.broadcasted_iota(jnp.int32, (rows, LANES), 1)
    new_ok = (lane_r <= t_rows) & (lane_r < n_new)
    tile_h = lambda x: jnp.concatenate([x] * HG, axis=0)

    qs, o_c, sels = [], [], []
    for g in range(G):
        qg = q_ref[0, g]
        qs.append(qg)
        kc, vc = kcv[g].astype(BF), kcv[G + g].astype(BF)
        s = _dot_nt(qg, kc)
        c_end = lax.broadcasted_iota(jnp.int32, s.shape, 1) * CMP_STRIDE + (CMP_LEN - 1)
        q_pos = past_len + lax.broadcasted_iota(jnp.int32, s.shape, 0) % TPAD
        (e,), den = _masked_softmax_parts([(s, c_end <= q_pos)])
        p = e / den
        o_c.append(_dot(p.astype(BF), vc))
        psum = functools.reduce(jnp.add, [p[h * TPAD:(h + 1) * TPAD] for h in range(HG)])
        imp = _dot_hl(psum, cover_ref[...])
        cur = (past_len + lax.broadcasted_iota(jnp.int32, imp.shape, 0)) // SEL_LEN
        sels.append(_select_topk(imp, cur, -(-(past_len + n_new) // SEL_LEN)))

    wait_all(b, 1)
    n_past_blocks = past_len // SEL_LEN
    o_s = []
    for g in range(G):
        sel = sels[g]
        selx = _dot(sel[:, :n_past_blocks].astype(BF), exp_ref[...])
        kt = sbuf[g * D:(g + 1) * D, :].astype(BF)
        vt = sbuf[(G + g) * D:(G + g + 1) * D, :].astype(BF)
        s_past = _dot(qs[g], kt)
        s_new = _dot_nt(qs[g], new_ref[0, g, 0])
        in_new = jnp.broadcast_to(sel[:, n_past_blocks:n_past_blocks + 1], (TPAD, LANES)) > 0.5
        (e_past, e_new), den = _masked_softmax_parts(
            [(s_past, tile_h(selx) > 0.5), (s_new, (tile_h(in_new.astype(F32)) > 0.5) & new_ok)])
        o_s.append((_dot_nt(e_past.astype(BF), vt) + _dot(e_new.astype(BF), new_ref[0, g, 1])) / den)

    @pl.when(b + 1 < nb)
    def _():
        start_all(b + 1, 1)

    wb = wt_ref.shape[2]
    for g in range(G):
        kt = wt_ref[0, g * D:(g + 1) * D, :].astype(BF)
        vt = wt_ref[0, (G + g) * D:(G + g + 1) * D, :].astype(BF)
        s_buf = _dot(qs[g], kt)
        i = lax.broadcasted_iota(jnp.int32, s_buf.shape, 1)
        t = lax.broadcasted_iota(jnp.int32, s_buf.shape, 0) % TPAD
        s_new = _dot_nt(qs[g], new_ref[0, g, 2])
        (e_buf, e_new), den = _masked_softmax_parts([(s_buf, wb + t - i <= WINDOW), (s_new, new_ok)])
        o_w = (_dot_nt(e_buf.astype(BF), vt) + _dot(e_new.astype(BF), new_ref[0, g, 3])) / den
        gates = 1.0 / (1.0 + jnp.exp(-gt_ref[0, g]))
        o_ref[0, g] = gates[:, 0:1] * o_c[g] + gates[:, 1:2] * o_s[g] + gates[:, 2:3] * o_w


def nsa_sample(page_table, cache_t, layer, q, gi, kv, win_t, cw, DB, DS):
    G, HG, D = NSA_KV_HEADS, NSA_GROUP, NSA_HD
    n_pages, page = page_table.shape[1], cache_t.shape[3]
    past_len = n_pages * page
    rows = HG * TPAD
    pa, pb, w1a, w1b, w2 = cw
    heads = lambda a, w: a.reshape(DB, DS, G, HG, w).transpose(0, 2, 3, 1, 4)
    padt = lambda a: jnp.pad(a, ((0, 0), (0, 0), (0, 0), (0, TPAD - DS), (0, 0)))
    q = jnp.concatenate([q[:, h * LANES + (h // HG) * D:h * LANES + (h // HG + 1) * D] for h in range(G * HG)], axis=1)
    qs = padt(heads(q, D)).reshape(DB, G, rows, D)
    gts = padt(heads(gi[:, :NSA_G_W], 3)).reshape(DB, G, rows, 3)
    gts = jnp.pad(gts, ((0, 0), (0, 0), (0, 0), (0, LANES - 3)))
    new = kv[:, 2 * G * D:].reshape(DB, DS, 4, G, D).transpose(0, 3, 2, 1, 4)
    new = jnp.pad(new, ((0, 0), (0, 0), (0, 0), (0, LANES - DS), (0, 0))).astype(BF)
    nsub = past_len // CMP_STRIDE
    n_blocks_pad = 2 * LANES
    cover = _cover_matrix(nsub, n_blocks_pad)
    tok = np.arange(past_len)[None, :] // SEL_LEN
    expand = jnp.asarray(np.arange(past_len // SEL_LEN)[:, None] == tok, BF)
    blk = lambda *s: pl.BlockSpec((1,) + s, lambda b, pt: (b,) + (0,) * len(s))
    full = lambda a: pl.BlockSpec(a.shape, lambda b, pt: (0,) * a.ndim)
    out = pl.pallas_call(
        functools.partial(_nsa_sample_kernel, layer=layer, n_pages=n_pages, page=page, past_len=past_len, n_new=DS),
        grid_spec=pltpu.PrefetchScalarGridSpec(
            num_scalar_prefetch=1,
            grid=(DB,),
            in_specs=[pl.BlockSpec(memory_space=pl.ANY), blk(G, rows, D), blk(G, rows, LANES),
                      blk(G, 4, LANES, D), blk(win_t.shape[1], win_t.shape[2]),
                      full(pa), full(pb), full(w1a), full(w1b), full(w2), full(cover), full(expand)],
            out_specs=blk(G, rows, D),
            scratch_shapes=[pltpu.VMEM((2 * G * D, past_len), F32), pltpu.VMEM((2 * G * D, past_len), F32),
                            pltpu.VMEM((past_len, LANES), F32), pltpu.VMEM((past_len, LANES), F32),
                            pltpu.VMEM((nsub + 8, CMP_HIDDEN), F32), pltpu.SemaphoreType.DMA((2,))]),
        out_shape=jax.ShapeDtypeStruct((DB, G, rows, D), F32),
        compiler_params=_params("arbitrary"),
        name="nsa_sample",
    )(page_table, cache_t, qs, gts, new, win_t, pa, pb, w1a, w1b, w2, cover, expand)
    out = out.reshape(DB, G, HG, TPAD, D)[:, :, :, :DS].transpose(0, 3, 1, 2, 4)
    return out.reshape(DB * DS, G * HG * D).astype(BF)


def _memkv_kernel(x_ref, g_ref, w_ref, kg_ref, seg_ref, o_ref, ob_ref):
    z = _dot(_rms(x_ref[...], g_ref[...]).astype(BF), w_ref[...])
    kw = XA_HEADS * XA_HD
    for j in range(2 * kw // LANES):
        sl = slice(j * LANES, (j + 1) * LANES)
        zc = z[:, sl]
        if j * LANES < kw:
            zc = _head_norm(zc, kg_ref[...], seg_ref[...])
        o_ref[:, sl] = zc
        ob_ref[:, sl] = zc.astype(BF)


def memory_kv(mem, g, w, kg, seg, tm=512):
    M, D = mem.shape
    N = w.shape[1]
    row = lambda wd: pl.BlockSpec((tm, wd), lambda i: (i, 0))
    return pl.pallas_call(
        _memkv_kernel,
        grid=(M // tm,),
        in_specs=[row(D), _full((1, D)), _full(w.shape), _full(kg.shape), _full(seg.shape)],
        out_specs=[row(N), row(N)],
        out_shape=[jax.ShapeDtypeStruct((M, N), F32), jax.ShapeDtypeStruct((M, N), BF)],
        compiler_params=_params("parallel"),
        name="memory_kv",
    )(mem, g.reshape(1, D), w, kg, seg)


def _postmix_kernel(x_ref, a_ref, b_ref, wa_ref, wb_ref, gx_ref, wq_ref, qg_ref, seg_ref, mkv_ref, wo_ref, o_ref,
                    *, rows_per_batch, n_mem):
    x1 = x_ref[...] + _dot(a_ref[...], wa_ref[...]) + _dot(b_ref[...], wb_ref[...])
    xn = _rms(x1, gx_ref[...]).astype(BF)
    zq = _dot(xn, wq_ref[...])
    kw = XA_HEADS * XA_HD
    q = jnp.concatenate([_head_norm(zq[:, j * LANES:(j + 1) * LANES], qg_ref[...], seg_ref[...])
                         for j in range(kw // LANES)], axis=1)
    q = (q * XA_HD ** -0.5).astype(BF)
    tm, nk = x1.shape[0], mkv_ref.shape[1]
    mask = None
    if rows_per_batch < tm:
        r = lax.broadcasted_iota(jnp.int32, (tm, nk), 0) // rows_per_batch
        c = lax.broadcasted_iota(jnp.int32, (tm, nk), 1) // n_mem
        mask = r == c
    outs = []
    for h in range(XA_HEADS):
        k = mkv_ref[0, :, h * XA_HD:(h + 1) * XA_HD]
        v = mkv_ref[0, :, kw + h * XA_HD:kw + (h + 1) * XA_HD]
        s = _dot_nt(q[:, h * XA_HD:(h + 1) * XA_HD], k)
        if mask is not None:
            s = jnp.where(mask, s, NEG)
        e = jnp.exp(s - jnp.max(s, axis=1, keepdims=True))
        outs.append(_dot(e.astype(BF), v) / jnp.sum(e, axis=1, keepdims=True))
    o = jnp.concatenate(outs, axis=1).astype(BF)
    o_ref[...] = x1 + _dot(o, wo_ref[...])


def post_mix(x, a, b, wa, wb, gx, wq, qg, seg, mkvb, wo, tm, rows_per_batch):
    M, D = x.shape
    n_mem = mkvb.shape[1]
    if rows_per_batch >= tm:
        per = rows_per_batch // tm
        mspec = pl.BlockSpec((1, n_mem, mkvb.shape[2]), lambda i: (i // per, 0, 0))
    else:
        assert tm == M
        mkvb = mkvb.reshape(1, -1, mkvb.shape[2])
        mspec = _full(mkvb.shape)
    row = lambda wd: pl.BlockSpec((tm, wd), lambda i: (i, 0))
    return pl.pallas_call(
        functools.partial(_postmix_kernel, rows_per_batch=rows_per_batch, n_mem=n_mem),
        grid=(M // tm,),
        in_specs=[row(D), row(a.shape[1]), row(b.shape[1]), _full(wa.shape), _full(wb.shape), _full((1, D)),
                  _full(wq.shape), _full(qg.shape), _full(seg.shape), mspec, _full(wo.shape)],
        out_specs=row(D),
        out_shape=jax.ShapeDtypeStruct((M, D), F32),
        compiler_params=_params("parallel"),
        name="post_mix",
    )(x, a, b, wa, wb, gx.reshape(1, D), wq, qg, seg, mkvb, wo)


def _mlp_kernel(x_ref, g_ref, w1_ref, w2_ref, o_ref, xn_ref, acc_ref):
    f = pl.program_id(1)

    @pl.when(f == 0)
    def _():
        x = x_ref[...]
        xn_ref[...] = _rms(x, g_ref[...]).astype(BF)
        acc_ref[...] = x

    h = _dot(xn_ref[...], w1_ref[...])
    h = jnp.square(jnp.maximum(h, 0.0)).astype(BF)
    acc_ref[...] += _dot(h, w2_ref[...])

    @pl.when(f == pl.num_programs(1) - 1)
    def _():
        o_ref[...] = acc_ref[...]


def mlp_block(x, g, w1b, w2b, tm, tf=512):
    M, D = x.shape
    FF = w1b.shape[1]
    return pl.pallas_call(
        _mlp_kernel,
        grid=(M // tm, FF // tf),
        in_specs=[pl.BlockSpec((tm, D), lambda i, f: (i, 0)),
                  pl.BlockSpec((1, D), lambda i, f: (0, 0)),
                  pl.BlockSpec((D, tf), lambda i, f: (0, f)),
                  pl.BlockSpec((tf, D), lambda i, f: (f, 0))],
        out_specs=pl.BlockSpec((tm, D), lambda i, f: (i, 0)),
        out_shape=jax.ShapeDtypeStruct((M, D), F32),
        scratch_shapes=[pltpu.VMEM((tm, D), BF), pltpu.VMEM((tm, D), F32)],
        compiler_params=_params("parallel", "arbitrary"),
        name="mlp",
    )(x, g.reshape(1, D), w1b, w2b)


ML_CHUNK = 128
CONV_PAD = 8


def _dot3(a, b):
    hi = b.astype(BF)
    r1 = b - hi.astype(F32)
    mid = r1.astype(BF)
    lo = (r1 - mid.astype(F32)).astype(BF)
    return _dot(a, hi) + _dot(a, mid) + _dot(a, lo)


def _mlstm_kernel(zqk_ref, zv_ref, zo_ref, gi_ref, cw_ref, cb_ref, gb_ref, hg_ref, tril_ref,
                  c0_ref, n0_ref, m0_ref, cv0_ref,
                  h_out, c_out, n_out, m_out, cv_out,
                  xin_ref, c_ref, n_ref, m_ref, *, n_valid):
    L = ML_CHUNK
    ci = pl.program_id(1)

    @pl.when(ci == 0)
    def _():
        c_ref[...] = c0_ref[0]
        n_ref[...] = n0_ref[0]
        m_ref[...] = m0_ref[0]
        xin_ref[pl.ds(0, CONV_PAD), :] = cv0_ref[0]

    xin_ref[pl.ds(CONV_PAD, L), :] = zqk_ref[...]
    y = cb_ref[...]
    for j in range(CONV_W):
        y = y + cw_ref[j:j + 1, :] * xin_ref[pl.ds(CONV_PAD - (CONV_W - 1) + j, L), :]
    qk = y * (1.0 / (1.0 + jnp.exp(-y)))
    hist = xin_ref[pl.ds(n_valid, CONV_PAD), :]
    xin_ref[pl.ds(0, CONV_PAD), :] = hist
    cv_out[0] = hist

    row = lax.broadcasted_iota(jnp.int32, (L, L), 0)
    col = lax.broadcasted_iota(jnp.int32, (L, L), 1)
    valid = row < n_valid
    gl = gi_ref[...] + gb_ref[...]
    for h in range(ML_HEADS):
        hs = slice(h * ML_HD, (h + 1) * ML_HD)
        q = qk[:, hs]
        k = qk[:, ML_W + h * ML_HD:ML_W + (h + 1) * ML_HD] * (ML_HD ** -0.5)
        v = zv_ref[:, hs]
        gi_col = jnp.broadcast_to(gl[:, NSA_G_W + h:NSA_G_W + h + 1], (L, L))
        gf_col = jnp.broadcast_to(gl[:, NSA_G_W + ML_HEADS + h:NSA_G_W + ML_HEADS + h + 1], (L, L))
        li = jnp.where(valid, gi_col, NEG)
        lf = jnp.where(valid, jnp.minimum(gf_col, 0.0) - jnp.log(1.0 + jnp.exp(-jnp.abs(gf_col))), 0.0)
        b = _dot3(tril_ref[...], lf)
        m_prev = m_ref[h:h + 1, :]
        dm = jnp.where(col <= row, b - (b - li).T, NEG)
        inter = b + m_prev
        m_t = jnp.maximum(inter, jnp.max(dm, axis=1, keepdims=True))
        w = jnp.exp(dm - m_t)
        a = jnp.exp(inter - m_t)
        qb, kb, vb = q.astype(BF), k.astype(BF), v.astype(BF)
        wqk = w * _dot_nt(qb, kb)
        c_old = c_ref[h]
        n_old = n_ref[h:h + 1, :]
        num = a * _dot_nt(qb, c_old.astype(BF)) + _dot(wqk.astype(BF), vb)
        den = a * jnp.sum(q * n_old, axis=1, keepdims=True) + jnp.sum(wqk, axis=1, keepdims=True)
        hh = num / jnp.maximum(jnp.abs(den), jnp.exp(-m_t))
        m_new = m_t[L - 1:L, :]
        b_last = b[L - 1:L, :]
        wk = jnp.exp(b_last - b + li - m_new) * k
        decay = jnp.exp(b_last + m_prev - m_new)
        c_ref[h] = decay * c_old + _dot(v.T.astype(BF), wk.astype(BF))
        n_ref[h:h + 1, :] = decay * n_old + jnp.sum(wk, axis=0, keepdims=True)
        m_ref[h:h + 1, :] = m_new
        hn = hh * lax.rsqrt(jnp.mean(hh * hh, axis=1, keepdims=True) + EPS) * hg_ref[...]
        h_out[:, hs] = (hn * (1.0 / (1.0 + jnp.exp(-zo_ref[:, hs])))).astype(BF)

    @pl.when(ci == pl.num_programs(1) - 1)
    def _():
        c_out[0] = c_ref[...]
        n_out[0] = n_ref[...]
        m_out[0] = m_ref[...]


def mlstm(zqk, zv, zo, gi, conv_w, conv_b, gate_b, hn_g, state, nb, n_valid):
    M = zqk.shape[0]
    nch = M // nb // ML_CHUNK
    assert n_valid == ML_CHUNK or nch == 1
    C0, n0, m0, cv0 = state
    m0 = jnp.broadcast_to(m0[:, :, None], (nb, ML_HEADS, LANES))
    cv0 = jnp.pad(cv0, ((0, 0), (CONV_PAD - (CONV_W - 1), 0), (0, 0)))
    gb = jnp.zeros((1, LANES), F32).at[0, NSA_G_W:NSA_G_W + 2 * ML_HEADS].set(gate_b)
    tril = jnp.asarray(np.tril(np.ones((ML_CHUNK, ML_CHUNK))), BF)
    row = lambda wd: pl.BlockSpec((ML_CHUNK, wd), lambda b, c: (b * nch + c, 0))
    st = lambda *shape: pl.BlockSpec((1,) + shape, lambda b, c: (b,) + (0,) * len(shape))
    h, Cn, nn, mn, cvn = pl.pallas_call(
        functools.partial(_mlstm_kernel, n_valid=n_valid),
        grid=(nb, nch),
        in_specs=[row(2 * ML_W), row(ML_W), row(ML_W), row(LANES),
                  _full(conv_w.shape), _full((1, 2 * ML_W)), _full((1, LANES)), _full((1, ML_HD)), _full(tril.shape),
                  st(ML_HEADS, ML_HD, ML_HD), st(ML_HEADS, ML_HD), st(ML_HEADS, LANES), st(CONV_PAD, 2 * ML_W)],
        out_specs=[row(ML_W), st(ML_HEADS, ML_HD, ML_HD), st(ML_HEADS, ML_HD), st(ML_HEADS, LANES),
                   st(CONV_PAD, 2 * ML_W)],
        out_shape=[jax.ShapeDtypeStruct((M, ML_W), BF),
                   jax.ShapeDtypeStruct((nb, ML_HEADS, ML_HD, ML_HD), F32),
                   jax.ShapeDtypeStruct((nb, ML_HEADS, ML_HD), F32),
                   jax.ShapeDtypeStruct((nb, ML_HEADS, LANES), F32),
                   jax.ShapeDtypeStruct((nb, CONV_PAD, 2 * ML_W), F32)],
        scratch_shapes=[pltpu.VMEM((CONV_PAD + ML_CHUNK, 2 * ML_W), F32),
                        pltpu.VMEM((ML_HEADS, ML_HD, ML_HD), F32),
                        pltpu.VMEM((ML_HEADS, ML_HD), F32),
                        pltpu.VMEM((ML_HEADS, LANES), F32)],
        compiler_params=_params("parallel", "arbitrary"),
        name="mlstm",
    )(zqk, zv, zo, gi, conv_w, conv_b.reshape(1, -1), gb, hn_g.reshape(1, -1), tril, C0, n0, m0, cv0)
    return h, (Cn, nn, mn[:, :, 0], cvn[:, CONV_PAD - (CONV_W - 1):])


def _rope_tables(pos, rows):
    half = ROPE_DIMS // 2
    freqs = ROPE_THETA ** (-jnp.arange(half, dtype=F32) / half)
    ang = pos.astype(F32)[:, None] * freqs
    cos, sin = jnp.cos(ang), jnp.sin(ang)
    n = pos.shape[0]
    one, zero = jnp.ones((n, NSA_HD - ROPE_DIMS), F32), jnp.zeros((n, NSA_HD - ROPE_DIMS), F32)
    zh = jnp.zeros((n, half), F32)
    c = jnp.concatenate([cos, cos, one], axis=1)
    sa = jnp.concatenate([-sin, zh, zero], axis=1)
    sb = jnp.concatenate([zh, sin, zero], axis=1)
    tile = lambda t: jnp.tile(t, (rows // n, LANES // NSA_HD))
    return tile(c), tile(sa), tile(sb)


def _seg_matrix():
    i = np.arange(LANES)
    return jnp.asarray((i[:, None] // NSA_HD == i[None, :] // NSA_HD) / NSA_HD, BF)


def _cover_matrix(nc_rows, lanes):
    ci = np.arange(nc_rows)[:, None] * CMP_STRIDE
    sj = np.arange(lanes)[None, :]
    return jnp.asarray((ci < (sj + 1) * SEL_LEN) & (ci + CMP_LEN > sj * SEL_LEN), BF)


def _expand_matrix(n_chunks):
    j = np.arange(LANES)[None, :, None]
    c = np.arange(n_chunks)[:, None, None]
    s = np.arange(KCH)[None, None, :]
    return jnp.asarray(j == c * (KCH // SEL_LEN) + s // SEL_LEN, BF)


def _in_weights(w):
    zq, zkv, zg, zqk, zv, zo, zif = jnp.split(w, np.cumsum(COL_SPLITS)[:-1].tolist(), axis=1)
    pad = jnp.zeros((w.shape[0], LANES - NSA_G_W - 2 * ML_HEADS), w.dtype)
    return tuple(a.astype(BF) for a in (zq, zkv, zqk, zv, zo, jnp.concatenate([zg, zif, pad], axis=1)))


def _compress_weights(cmp_pos, cmp_w1, cmp_w2):
    G, D, S = NSA_KV_HEADS, NSA_HD, CMP_STRIDE
    pos = jnp.tile(cmp_pos[:, :, None, :], (1, 1, G, 1))
    pos = pos.transpose(1, 0, 2, 3).reshape(CMP_LEN, 2 * G * D)
    w1 = cmp_w1.reshape(2, CMP_LEN, D, CMP_HIDDEN)
    z = jnp.zeros_like(w1)
    w1g = jnp.stack([jnp.concatenate([w1, z], axis=2), jnp.concatenate([z, w1], axis=2)], axis=1)
    return (pos[:S], pos[S:], w1g[:, :, :S].astype(BF), w1g[:, :, S:].astype(BF), cmp_w2.astype(BF))


def kernel(x_prompt, x_sample, mem_prompt, cache_nsa_kv, cache_win_kv, cache_mem_kv, state_mlstm_C, state_mlstm_n, state_mlstm_m, state_conv, page_table, norm_mix_g, w_in, nsa_qn_g, nsa_kn_g, cmp_pos, cmp_w1, cmp_w2, ml_conv_w, ml_conv_b, ml_gate_b, ml_hn_g, w_out, norm_xa_g, norm_mem_g, xa_wq, xa_wkv, xa_qn_g, xa_kn_g, xa_wo, norm_mlp_g, mlp_w1, mlp_w2):
    B, T, D = x_prompt.shape
    DB, DS, _ = x_sample.shape
    depth = w_in.shape[0]
    n_mem = mem_prompt.shape[1]
    P = page_table.shape[1] * cache_nsa_kv.shape[1]
    wkeep = min(WINDOW, T)
    G, HD = NSA_KV_HEADS, NSA_HD
    MP, MS = B * T, DB * DS
    tm_p, tm_s = 512, MS

    pos_p = jnp.arange(T, dtype=jnp.int32)
    pos_s = P + jnp.arange(DS, dtype=jnp.int32)
    tabs_p = _rope_tables(pos_p, T)
    tabs_s = _rope_tables(pos_s, MS)
    seg = _seg_matrix()
    cover = _cover_matrix(T // CMP_STRIDE, LANES)
    expand = _expand_matrix(T // KCH)
    tile2 = lambda v: jnp.tile(v.reshape(-1, HD), (1, LANES // HD))
    pad_chunk = lambda a: jnp.pad(a.reshape(DB, DS, -1), ((0, 0), (0, ML_CHUNK - DS), (0, 0))).reshape(DB * ML_CHUNK, -1)

    n_pool, page = cache_nsa_kv.shape[:2]
    cache_t = cache_nsa_kv.transpose(0, 2, 3, 4, 5, 1).reshape(n_pool, depth, 4 * G * HD, page)
    win_t = cache_win_kv.transpose(0, 1, 3, 4, 5, 2).reshape(depth, DB, 2 * G * HD, cache_win_kv.shape[2])

    xp = x_prompt.reshape(MP, D)
    xs = x_sample.reshape(MS, D)
    mem = mem_prompt.reshape(B * n_mem, D)
    nsa_p, nsa_s, win_p, win_s = [], [], [], []
    C_p, C_s, n_p, n_s, m_p, m_s = [], [], [], [], [], []
    cv_p, cv_s, mem_p = [], [], []
    for l in range(depth):
        w_l = _in_weights(w_in[l])
        qg, kg = tile2(nsa_qn_g[l]), tile2(nsa_kn_g[l])
        cw = _compress_weights(cmp_pos[l], cmp_w1[l], cmp_w2[l])
        wa, wb = w_out[l, :NSA_Q_W].astype(BF), w_out[l, NSA_Q_W:].astype(BF)
        xq, xo = xa_wq[l].astype(BF), xa_wo[l].astype(BF)
        xqg = tile2(xa_qn_g[l])
        w1b, w2b = mlp_w1[l].astype(BF), mlp_w2[l].astype(BF)

        q, kv, kb, va, zqk, zv, zo, gi = in_proj(xp, norm_mix_g[l], w_l, tabs_p, qg, kg, seg, tm_p)
        kcv = compress_prompt(kv, cw, B, T)
        a_p = nsa_prompt(q, gi, kb, va, kcv, cover, expand, B, T)
        kv3 = kv.reshape(B, T, NSA_KV_W)
        nsa_p.append(kv3[:, :, :4 * G * HD].reshape(B, T, 4, G, HD))
        win_p.append(kv3[:, T - wkeep:, 4 * G * HD:].reshape(B, wkeep, 2, G, HD))
        st0 = (jnp.zeros((B, ML_HEADS, ML_HD, ML_HD), F32), jnp.zeros((B, ML_HEADS, ML_HD), F32),
               jnp.zeros((B, ML_HEADS), F32), jnp.zeros((B, CONV_W - 1, 2 * ML_W), F32))
        b_p, (Cn, nn, mn, buf_p) = mlstm(zqk, zv, zo, gi, ml_conv_w[l], ml_conv_b[l], ml_gate_b[l], ml_hn_g[l],
                                         st0, B, ML_CHUNK)
        C_p.append(Cn); n_p.append(nn); m_p.append(mn); cv_p.append(buf_p)
        mkv, mkvb = memory_kv(mem, norm_mem_g[l], xa_wkv[l].astype(BF), tile2(xa_kn_g[l]), seg)
        mem_p.append(mkv.reshape(B, n_mem, 2, XA_HEADS, XA_HD))
        xp = post_mix(xp, a_p, b_p, wa, wb, norm_xa_g[l], xq, xqg, seg, mkvb.reshape(B, n_mem, -1), xo, tm_p, T)
        xp = mlp_block(xp, norm_mlp_g[l], w1b, w2b, tm=1024)

        q, kv, _, _, zqk, zv, zo, gi = in_proj(xs, norm_mix_g[l], w_l, tabs_s, qg, kg, seg, tm_s)
        kvp = kv.reshape(DB, DS, 3, 2, G, HD)
        a_s = nsa_sample(page_table, cache_t, l, q, gi, kv, win_t[l], cw, DB, DS)
        nsa_s.append(kvp[:, :, :2].reshape(DB, DS, 4, G, HD))
        win_s.append(jnp.concatenate([cache_win_kv[l], kvp[:, :, 2]], axis=1)[:, -cache_win_kv.shape[2]:])
        st = (state_mlstm_C[l], state_mlstm_n[l], state_mlstm_m[l], state_conv[l])
        b_s, (Cn, nn, mn, buf_s) = mlstm(*(pad_chunk(a) for a in (zqk, zv, zo, gi)), ml_conv_w[l], ml_conv_b[l],
                                         ml_gate_b[l], ml_hn_g[l], st, DB, DS)
        b_s = b_s.reshape(DB, ML_CHUNK, ML_W)[:, :DS].reshape(MS, ML_W)
        C_s.append(Cn); n_s.append(nn); m_s.append(mn); cv_s.append(buf_s)
        mkvb_s = cache_mem_kv[l].reshape(DB, n_mem, -1).astype(BF)
        xs = post_mix(xs, a_s, b_s, wa, wb, norm_xa_g[l], xq, xqg, seg, mkvb_s, xo, tm_s, DS)
        xs = mlp_block(xs, norm_mlp_g[l], w1b, w2b, tm=MS)
    return (xp.reshape(B, T, D), xs.reshape(DB, DS, D),
            jnp.stack(nsa_p, axis=2), jnp.stack(nsa_s, axis=2),
            jnp.stack(win_p), jnp.stack(win_s),
            jnp.stack(C_p), jnp.stack(C_s),
            jnp.stack(n_p), jnp.stack(n_s),
            jnp.stack(m_p), jnp.stack(m_s),
            jnp.stack(cv_p), jnp.stack(cv_s),
            jnp.stack(mem_p))
```

```python
import functools

import jax
import jax.numpy as jnp
import numpy as np
from jax import lax
from jax.experimental import pallas as pl
from jax.experimental.pallas import tpu as pltpu

F32 = jnp.float32
BF = jnp.bfloat16

EPS = 1e-6
NSA_HEADS, NSA_KV_HEADS, NSA_HD = 8, 2, 64
NSA_GROUP = NSA_HEADS // NSA_KV_HEADS
CMP_LEN, CMP_STRIDE, CMP_HIDDEN = 32, 16, 128
SEL_LEN, N_SEL, WINDOW = 64, 8, 512
ROPE_DIMS, ROPE_THETA = NSA_HD // 4, 500000.0
ML_HEADS, ML_HD, CONV_W = 4, 128, 4
ML_W = ML_HEADS * ML_HD
XA_HEADS, XA_HD = 4, 64
LANES = 128
QB = 128
VMEM_LIMIT = 56 * 1024 * 1024

NSA_Q_W = NSA_HEADS * NSA_HD
NSA_KV_W = 6 * NSA_KV_HEADS * NSA_HD
NSA_G_W = 3 * NSA_HEADS
COL_SPLITS = (NSA_Q_W, NSA_KV_W, NSA_G_W, 2 * ML_W, ML_W, ML_W, 2 * ML_HEADS)
NEG = -1e30


def _dot(a, b):
    return jnp.dot(a, b, preferred_element_type=F32)


def _dot_nt(a, b):
    return lax.dot_general(a, b, (((1,), (1,)), ((), ())), preferred_element_type=F32)


def _dot_hl(a, b):
    hi = a.astype(BF)
    lo = (a - hi.astype(F32)).astype(BF)
    return _dot(hi, b) + _dot(lo, b)


def _rms(x, g):
    return x * lax.rsqrt(jnp.mean(x * x, axis=-1, keepdims=True) + EPS) * g


def _head_norm(z, g, seg):
    return z * lax.rsqrt(_dot_hl(z * z, seg) + EPS) * g


def _rope(y, c, sa, sb):
    half = ROPE_DIMS // 2
    return y * c + pltpu.roll(y, LANES - half, 1) * sa + pltpu.roll(y, half, 1) * sb


def _params(*sem):
    return pltpu.CompilerParams(dimension_semantics=sem, vmem_limit_bytes=VMEM_LIMIT)


def _full(shape):
    n = len(shape)
    return pl.BlockSpec(shape, lambda *_: (0,) * n)


def _inproj_kernel(x_ref, g_ref, wq_ref, wkv_ref, wqk_ref, wv_ref, wo_ref, wgi_ref,
                   rc_ref, rsa_ref, rsb_ref, qg_ref, kg_ref, seg_ref,
                   q_out, kv_out, kb_out, va_out, zqk_out, zv_out, zo_out, gi_out):
    D = NSA_HD
    xn = _rms(x_ref[...], g_ref[...]).astype(BF)
    c, sa, sb = rc_ref[...], rsa_ref[...], rsb_ref[...]
    seg = seg_ref[...]
    lane = lax.broadcasted_iota(jnp.int32, c.shape, 1)
    low = lane < D
    swap = lambda z: pltpu.roll(z, D, 1)
    zq = _dot(xn, wq_ref[...])
    for j in range(NSA_Q_W // LANES):
        y = _rope(_head_norm(zq[:, j * LANES:(j + 1) * LANES], qg_ref[...], seg), c, sa, sb) * D ** -0.5
        ys = swap(y)
        for i in range(2):
            h = 2 * j + i
            want_low = h // NSA_GROUP == 0
            src = y if want_low == (i == 0) else ys
            q_out[:, h * LANES:(h + 1) * LANES] = jnp.where(low if want_low else ~low, src, 0.0).astype(BF)
    zkv = _dot(xn, wkv_ref[...])
    for j in range(NSA_KV_W // LANES):
        sl = slice(j * LANES, (j + 1) * LANES)
        z = zkv[:, sl]
        br = j // 2
        if j % 2 == 0:
            z = _rope(_head_norm(z, kg_ref[br:br + 1, :], seg), c, sa, sb)
            if br > 0:
                kb_out[:, (br - 1) * LANES:br * LANES] = z.astype(BF)
        elif br > 0:
            one = jnp.where(lane == D, 1.0, 0.0)
            for g, src in enumerate((z, swap(z))):
                col = ((br - 1) * NSA_KV_HEADS + g) * LANES
                va_out[:, col:col + LANES] = jnp.where(low, src, one).astype(BF)
        kv_out[:, sl] = z
    zqk_out[...] = _dot(xn, wqk_ref[...])
    zv_out[...] = _dot(xn, wv_ref[...])
    zo_out[...] = _dot(xn, wo_ref[...])
    gi_out[...] = _dot(xn, wgi_ref[...])


def in_proj(x, g, w, rope_tabs, qg, kg, seg, tm):
    M, D = x.shape
    rc, rsa, rsb = rope_tabs
    nt = rc.shape[0] // tm
    widths = (NSA_HEADS * LANES, NSA_KV_W, 2 * LANES, 4 * LANES, 2 * ML_W, ML_W, ML_W, LANES)
    dtypes = (BF, F32, BF, BF, F32, F32, F32, F32)
    row = lambda wd: pl.BlockSpec((tm, wd), lambda i: (i, 0))
    tab = pl.BlockSpec((tm, LANES), lambda i: (i % nt, 0))
    return pl.pallas_call(
        _inproj_kernel,
        grid=(M // tm,),
        in_specs=[row(D), _full((1, D))] + [_full(a.shape) for a in w] + [tab, tab, tab,
                  _full(qg.shape), _full(kg.shape), _full(seg.shape)],
        out_specs=[row(wd) for wd in widths],
        out_shape=[jax.ShapeDtypeStruct((M, wd), dt) for wd, dt in zip(widths, dtypes)],
        compiler_params=_params("parallel"),
        name="in_proj",
    )(x, g.reshape(1, D), *w, rc, rsa, rsb, qg, kg, seg)


def _compress_body(xk_ref, xv_ref, pa_ref, pb_ref, w1a_ref, w1b_ref, w2_ref, sh_ref, nsub):
    G = NSA_KV_HEADS
    acc_a = [jnp.zeros((nsub, CMP_HIDDEN), F32) for _ in range(2 * G)]
    acc_b = [jnp.zeros((nsub, CMP_HIDDEN), F32) for _ in range(2 * G)]
    for u in range(CMP_STRIDE):
        for kind, x_ref in enumerate((xk_ref, xv_ref)):
            sl = slice(kind * LANES, (kind + 1) * LANES)
            xu = x_ref[pl.ds(u, nsub, stride=CMP_STRIDE), :]
            xa = (xu + pa_ref[u:u + 1, sl]).astype(BF)
            xb = (xu + pb_ref[u:u + 1, sl]).astype(BF)
            for g in range(G):
                acc_a[kind * G + g] += _dot(xa, w1a_ref[kind, g, u])
                acc_b[kind * G + g] += _dot(xb, w1b_ref[kind, g, u])
    sh_ref[pl.ds(nsub, 8), :] = jnp.zeros((8, CMP_HIDDEN), F32)
    outs = []
    for kind in range(2):
        for g in range(G):
            sh_ref[pl.ds(0, nsub), :] = acc_b[kind * G + g]
            h = acc_a[kind * G + g] + sh_ref[pl.ds(1, nsub), :]
            h = h * (1.0 / (1.0 + jnp.exp(-h)))
            outs.append(_dot(h.astype(BF), w2_ref[kind]))
    return outs


def _compress_kernel(xk_ref, xv_ref, pa_ref, pb_ref, w1a_ref, w1b_ref, w2_ref, o_ref, sh_ref, *, nsub):
    kc0, kc1, vc0, vc1 = _compress_body(xk_ref, xv_ref, pa_ref, pb_ref, w1a_ref, w1b_ref, w2_ref, sh_ref, nsub)
    zero = jnp.zeros_like(vc0)
    for j, o in enumerate((kc0, kc1, vc0, zero, vc1, zero)):
        o_ref[0, :, j * NSA_HD:(j + 1) * NSA_HD] = o


def compress_prompt(kv, cw, B, T):
    pa, pb, w1a, w1b, w2 = cw
    nsub = T // CMP_STRIDE
    return pl.pallas_call(
        functools.partial(_compress_kernel, nsub=nsub),
        grid=(B,),
        in_specs=[pl.BlockSpec((T, LANES), lambda b: (b, 0)), pl.BlockSpec((T, LANES), lambda b: (b, 1)),
                  _full(pa.shape), _full(pb.shape), _full(w1a.shape), _full(w1b.shape), _full(w2.shape)],
        out_specs=pl.BlockSpec((1, nsub, 3 * LANES), lambda b: (b, 0, 0)),
        out_shape=jax.ShapeDtypeStruct((B, nsub, 3 * LANES), F32),
        scratch_shapes=[pltpu.VMEM((nsub + 8, CMP_HIDDEN), F32)],
        compiler_params=_params("parallel"),
        name="compress_prompt",
    )(kv, kv, pa, pb, w1a, w1b, w2)


def _select_topk(imp, cur, n_blocks):
    j = lax.broadcasted_iota(jnp.int32, imp.shape, 1)
    jf = j.astype(F32)
    forced = (j == cur) | (j == 0)
    dead = (j > cur) | (j >= n_blocks)
    val = jnp.where(forced, jnp.inf, jnp.where(dead, -jnp.inf, imp))
    sel = jnp.zeros(imp.shape, F32)
    for _ in range(N_SEL):
        m = jnp.max(val, axis=1, keepdims=True)
        idx = jnp.min(jnp.where(val == m, jf, float(imp.shape[1])), axis=1, keepdims=True)
        hit = jf == idx
        sel = jnp.where(hit & (m > -jnp.inf), 1.0, sel)
        val = jnp.where(hit, -jnp.inf, val)
    return sel


KCH = 2 * QB


def _nsa_prompt_kernel(q_ref, gi_ref, kcv_ref, kb_ref, va_ref, cover_ref, exp_ref, o_ref,
                       s_ref, mx_ref, acc_ref, *, n_sel_blocks):
    G, HG, D, H = NSA_KV_HEADS, NSA_GROUP, NSA_HD, NSA_HEADS
    qb = pl.program_id(1)
    q0 = qb * QB
    qpos = q0 + lax.broadcasted_iota(jnp.int32, (QB, LANES), 0)
    lane = lax.broadcasted_iota(jnp.int32, (QB, LANES), 1)
    gates = 1.0 / (1.0 + jnp.exp(-gi_ref[...]))
    nc = kcv_ref.shape[1]
    qp = jnp.concatenate([q_ref[:, h * LANES:(h + 1) * LANES] for h in range(H)], axis=0)

    s_all = _dot_nt(qp, kcv_ref[0, :, 0:LANES].astype(BF))
    cmask = (lane[:, :nc] * CMP_STRIDE + (CMP_LEN - 1)) <= qpos[:, :nc]
    o_c, psums = [], []
    for g in range(G):
        vc = kcv_ref[0, :, (1 + g) * LANES:(2 + g) * LANES].astype(BF)
        psum = jnp.zeros((QB, nc), F32)
        for h in range(HG):
            hh = g * HG + h
            s = jnp.where(cmask, s_all[hh * QB:(hh + 1) * QB], NEG)
            e = jnp.where(cmask, jnp.exp(s - jnp.max(s, axis=1, keepdims=True)), 0.0)
            den = jnp.sum(e, axis=1, keepdims=True)
            p = e / jnp.where(den > 0, den, 1.0)
            psum = psum + p
            o_c.append(_dot(p.astype(BF), vc))
        psums.append(psum)
    imp = _dot_hl(jnp.concatenate(psums, axis=0), cover_ref[...])
    sel_all = _select_topk(imp, jnp.concatenate([qpos // SEL_LEN] * G, axis=0), n_sel_blocks).astype(BF)
    sels = [sel_all[g * QB:(g + 1) * QB] for g in range(G)]

    qpos2 = q0 + lax.broadcasted_iota(jnp.int32, (QB, KCH), 0)
    lane2 = lax.broadcasted_iota(jnp.int32, (QB, KCH), 1)
    branch_out = []
    for br in range(2):
        mx_ref[...] = jnp.full(mx_ref.shape, NEG, F32)
        acc_ref[...] = jnp.zeros(acc_ref.shape, F32)

        def scores(c, carry, br=br):
            k0 = pl.multiple_of(c * KCH, KCH)
            s_all = _dot_nt(qp, kb_ref[pl.ds(k0, KCH), br * LANES:(br + 1) * LANES])
            d = qpos2 - (k0 + lane2)
            if br == 0:
                masks = [(_dot(sels[g], exp_ref[c]) > 0.5) & (d >= 0) for g in range(G)]
            else:
                masks = [(d >= 0) & (d <= WINDOW)] * G
            for hh in range(H):
                rows = pl.ds(hh * QB, QB)
                s = jnp.where(masks[hh // HG], s_all[hh * QB:(hh + 1) * QB], NEG)
                s_ref[c, rows, :] = s
                mx_ref[rows, :] = functools.reduce(
                    jnp.maximum, [mx_ref[rows, :]] + [s[:, i * LANES:(i + 1) * LANES] for i in range(KCH // LANES)])
            return carry

        def values(c, carry, br=br):
            k0 = pl.multiple_of(c * KCH, KCH)
            for g in range(G):
                rows = pl.ds(g * HG * QB, HG * QB)
                mx = mx_ref[rows, :]
                p = jnp.concatenate([jnp.exp(s_ref[c, rows, pl.ds(i * LANES, LANES)] - mx)
                                     for i in range(KCH // LANES)], axis=1).astype(BF)
                acc_ref[rows, :] += _dot(p, va_ref[pl.ds(k0, KCH), (br * G + g) * LANES:(br * G + g + 1) * LANES])
            return carry

        lo = 0 if br == 0 else jnp.maximum(q0 - WINDOW, 0) // KCH
        hi = (q0 + QB - 1) // KCH + 1
        lax.fori_loop(lo, hi, scores, 0)
        mx_ref[...] = jnp.broadcast_to(jnp.max(mx_ref[...], axis=1, keepdims=True), mx_ref.shape)
        lax.fori_loop(lo, hi, values, 0)
        r = acc_ref[...]
        branch_out.append(r / r[:, D:D + 1])

    for j in range(H // 2):
        tiles = []
        for hh in (2 * j, 2 * j + 1):
            rows = slice(hh * QB, (hh + 1) * QB)
            tiles.append(gates[:, 3 * hh:3 * hh + 1] * o_c[hh]
                         + gates[:, 3 * hh + 1:3 * hh + 2] * branch_out[0][rows]
                         + gates[:, 3 * hh + 2:3 * hh + 3] * branch_out[1][rows])
        o_ref[:, j * LANES:(j + 1) * LANES] = jnp.where(lane < D, tiles[0], pltpu.roll(tiles[1], D, 1)).astype(BF)


def nsa_prompt(q, gi, kb, va, kcv, cover, expand, B, T):
    nqb = T // QB
    nc = kcv.shape[1]
    rows = NSA_HEADS * QB
    return pl.pallas_call(
        functools.partial(_nsa_prompt_kernel, n_sel_blocks=T // SEL_LEN),
        grid=(B, nqb),
        in_specs=[pl.BlockSpec((QB, q.shape[1]), lambda b, i: (b * nqb + i, 0)),
                  pl.BlockSpec((QB, LANES), lambda b, i: (b * nqb + i, 0)),
                  pl.BlockSpec((1, nc, kcv.shape[2]), lambda b, i: (b, 0, 0)),
                  pl.BlockSpec((T, kb.shape[1]), lambda b, i: (b, 0)),
                  pl.BlockSpec((T, va.shape[1]), lambda b, i: (b, 0)),
                  _full(cover.shape), _full(expand.shape)],
        out_specs=pl.BlockSpec((QB, NSA_Q_W), lambda b, i: (b * nqb + i, 0)),
        out_shape=jax.ShapeDtypeStruct((B * T, NSA_Q_W), BF),
        scratch_shapes=[pltpu.VMEM((T // KCH, rows, KCH), F32), pltpu.VMEM((rows, LANES), F32),
                        pltpu.VMEM((rows, LANES), F32)],
        compiler_params=_params("parallel", "arbitrary"),
        name="nsa_prompt",
    )(q, gi, kcv, kb, va, cover, expand)


TPAD = 8


def _masked_softmax_parts(parts):
    m = functools.reduce(jnp.maximum, [jnp.max(jnp.where(k, s, NEG), axis=1, keepdims=True) for s, k in parts])
    es = [jnp.where(k, jnp.exp(jnp.where(k, s, NEG) - m), 0.0) for s, k in parts]
    den = functools.reduce(jnp.add, [jnp.sum(e, axis=1, keepdims=True) for e in es])
    return es, jnp.where(den > 0, den, 1.0)


def _nsa_sample_kernel(pt_ref, ct_ref, q_ref, gt_ref, new_ref, wt_ref, perm_ref, pos_ref, wk_ref, wv_ref, w2_ref,
                       cover_ref, exp_ref, o_ref, cbuf, sbuf, xk_ref, xv_ref, sh_ref, sem,
                       *, layer, n_pages, page, past_len, n_new):
    G, HG, D = NSA_KV_HEADS, NSA_GROUP, NSA_HD
    b = pl.program_id(0)
    nb = pl.num_programs(0)
    half_rows = 2 * G * D
    bufs = (cbuf, sbuf)
    sub_per_page = page // CMP_STRIDE

    def page_copy(bb, p, half):
        return pltpu.make_async_copy(
            ct_ref.at[pt_ref[bb, p], layer, pl.ds(half * half_rows, half_rows), :],
            bufs[half].at[:, pl.ds(pl.multiple_of(p * page, page), page)], sem.at[half])

    def start_all(bb, half):
        lax.fori_loop(0, n_pages, lambda p, c: (page_copy(bb, p, half).start(), c)[1], 0)

    def wait_all(bb, half):
        lax.fori_loop(0, n_pages, lambda p, c: (page_copy(bb, p, half).wait(), c)[1], 0)

    @pl.when(b == 0)
    def _():
        start_all(0, 0)
        start_all(0, 1)

    wait_all(b, 0)

    def to_rows(p, c):
        p0 = pl.multiple_of(p * page, page)
        z = _dot_nt(perm_ref[...], cbuf[:, pl.ds(p0, page)].astype(BF))
        r0 = pl.multiple_of(p * sub_per_page, sub_per_page)
        for u in range(CMP_STRIDE):
            zu = z[u * sub_per_page:(u + 1) * sub_per_page]
            xk_ref[pl.ds(r0, sub_per_page), u * LANES:(u + 1) * LANES] = zu[:, 0:G * D]
            xv_ref[pl.ds(r0, sub_per_page), u * LANES:(u + 1) * LANES] = zu[:, G * D:2 * G * D]
        return c

    lax.fori_loop(0, n_pages, to_rows, 0, unroll=4)

    @pl.when(b + 1 < nb)
    def _():
        start_all(b + 1, 0)

    nsub = past_len // CMP_STRIDE
    half_w = G * CMP_HIDDEN
    sh_ref[pl.ds(nsub, 8), :] = jnp.zeros((8, half_w), F32)
    kcv = []
    for kind, (x_ref, w_ref) in enumerate(((xk_ref, wk_ref), (xv_ref, wv_ref))):
        bias = _dot(pos_ref[kind], w_ref[...])
        h_all = _dot(x_ref[...].astype(BF), w_ref[...])
        sh_ref[pl.ds(0, nsub), :] = h_all[:, half_w:] + bias[1:2, half_w:]
        h = h_all[:, :half_w] + bias[0:1, :half_w] + sh_ref[pl.ds(1, nsub), :]
        h = h * (1.0 / (1.0 + jnp.exp(-h)))
        kcv.append(_dot(h.astype(BF), w2_ref[kind]).astype(BF))
    kc, vc = kcv

    rows = G * HG * TPAD
    q = q_ref[0]
    t_rows = lax.broadcasted_iota(jnp.int32, (rows, LANES), 0) % TPAD
    lane_r = lax.broadcasted_iota(jnp.int32, (rows, LANES), 1)
    new_ok = (lane_r <= t_rows) & (lane_r < n_new)
    tile_gh = lambda xs: jnp.concatenate([x for x in xs for _ in range(HG)], axis=0)

    s = _dot_nt(q, kc)
    c_end = lax.broadcasted_iota(jnp.int32, s.shape, 1) * CMP_STRIDE + (CMP_LEN - 1)
    q_pos = past_len + lax.broadcasted_iota(jnp.int32, s.shape, 0) % TPAD
    (e,), den = _masked_softmax_parts([(s, c_end <= q_pos)])
    p = e / den
    o_c = _dot(p.astype(BF), vc)
    psum = jnp.concatenate([functools.reduce(jnp.add, [p[(g * HG + h) * TPAD:(g * HG + h + 1) * TPAD]
                                                       for h in range(HG)]) for g in range(G)], axis=0)
    imp = _dot_hl(psum, cover_ref[...])
    cur = (past_len + lax.broadcasted_iota(jnp.int32, imp.shape, 0) % TPAD) // SEL_LEN
    sel = _select_topk(imp, cur, -(-(past_len + n_new) // SEL_LEN))

    wait_all(b, 1)
    n_past_blocks = past_len // SEL_LEN
    selx = _dot(sel[:, :n_past_blocks].astype(BF), exp_ref[...])
    in_new = jnp.broadcast_to(sel[:, n_past_blocks:n_past_blocks + 1], (G * TPAD, LANES))
    split_g = lambda x: [x[g * TPAD:(g + 1) * TPAD] for g in range(G)]
    s_past = _dot(q, sbuf[0:G * D, :].astype(BF))
    s_new = _dot_nt(q, new_ref[0, 0])
    (e_past, e_new), den = _masked_softmax_parts(
        [(s_past, tile_gh(split_g(selx)) > 0.5), (s_new, (tile_gh(split_g(in_new)) > 0.5) & new_ok)])
    o_t = _dot_nt(sbuf[G * D:2 * G * D, :].astype(BF), e_past.astype(BF))
    o_s = (o_t.T + _dot(e_new.astype(BF), new_ref[0, 1])) / den

    @pl.when(b + 1 < nb)
    def _():
        start_all(b + 1, 1)

    wb = wt_ref.shape[2]
    s_buf = _dot(q, wt_ref[0, 0:G * D, :].astype(BF))
    i = lax.broadcasted_iota(jnp.int32, s_buf.shape, 1)
    t = lax.broadcasted_iota(jnp.int32, s_buf.shape, 0) % TPAD
    s_new = _dot_nt(q, new_ref[0, 2])
    (e_buf, e_new), den = _masked_softmax_parts([(s_buf, wb + t - i <= WINDOW), (s_new, new_ok)])
    o_t = _dot_nt(wt_ref[0, G * D:2 * G * D, :].astype(BF), e_buf.astype(BF))
    o_w = (o_t.T + _dot(e_new.astype(BF), new_ref[0, 3])) / den
    gates = 1.0 / (1.0 + jnp.exp(-gt_ref[0]))
    o_ref[0] = gates[:, 0:1] * o_c + gates[:, 1:2] * o_s + gates[:, 2:3] * o_w


def _sample_compress_weights(cmp_pos, cmp_w1, cmp_w2):
    G, D, S, Hd = NSA_KV_HEADS, NSA_HD, CMP_STRIDE, CMP_HIDDEN
    w1 = cmp_w1.reshape(2, 2, S, D, Hd)
    eye = jnp.eye(G, dtype=w1.dtype)
    wk = jnp.einsum('khudn,gj->kugdhjn', w1, eye).reshape(2, S * G * D, 2 * G * Hd).astype(BF)
    pos = cmp_pos.reshape(2, 2, S, 1, D)
    pos = jnp.broadcast_to(pos, (2, 2, S, G, D)).reshape(2, 2, S * G * D)
    pos = jnp.pad(pos, ((0, 0), (0, 8 - 2), (0, 0))).astype(BF)
    w2 = jnp.einsum('knd,gj->kgnjd', cmp_w2, eye).reshape(2, G * Hd, G * D).astype(BF)
    return wk[0], wk[1], pos, w2


def nsa_sample(page_table, cache_t, layer, q, gi, kv, win_t, cw, DB, DS):
    G, HG, D, H = NSA_KV_HEADS, NSA_GROUP, NSA_HD, NSA_HEADS
    n_pages, page = page_table.shape[1], cache_t.shape[3]
    past_len = n_pages * page
    rows = H * TPAD
    wk, wv, pos, w2 = cw
    heads = lambda a, w: a.reshape(DB, DS, H, w).transpose(0, 2, 1, 3)
    padt = lambda a: jnp.pad(a, ((0, 0), (0, 0), (0, TPAD - DS), (0, 0)))
    qs = padt(heads(q, LANES)).reshape(DB, rows, LANES)
    gts = padt(heads(gi[:, :NSA_G_W], 3)).reshape(DB, rows, 3)
    gts = jnp.pad(gts, ((0, 0), (0, 0), (0, LANES - 3)))
    new = kv[:, 2 * G * D:].reshape(DB, DS, 4, G * D).transpose(0, 2, 1, 3)
    new = jnp.pad(new, ((0, 0), (0, 0), (0, LANES - DS), (0, 0))).astype(BF)
    nsub = past_len // CMP_STRIDE
    sub_per_page = page // CMP_STRIDE
    cover = _cover_matrix(nsub, 2 * LANES)
    tok = np.arange(past_len)[None, :] // SEL_LEN
    expand = jnp.asarray(np.arange(past_len // SEL_LEN)[:, None] == tok, BF)
    r = np.arange(page)
    perm = jnp.asarray((r % sub_per_page)[:, None] * CMP_STRIDE + (r // sub_per_page)[:, None] == r[None, :], BF)
    blk = lambda *s: pl.BlockSpec((1,) + s, lambda b, pt: (b,) + (0,) * len(s))
    full = lambda a: pl.BlockSpec(a.shape, lambda b, pt: (0,) * a.ndim)
    out = pl.pallas_call(
        functools.partial(_nsa_sample_kernel, layer=layer, n_pages=n_pages, page=page, past_len=past_len, n_new=DS),
        grid_spec=pltpu.PrefetchScalarGridSpec(
            num_scalar_prefetch=1,
            grid=(DB,),
            in_specs=[pl.BlockSpec(memory_space=pl.ANY), blk(rows, LANES), blk(rows, LANES),
                      blk(4, LANES, G * D), blk(win_t.shape[1], win_t.shape[2]),
                      full(perm), full(pos), full(wk), full(wv), full(w2), full(cover), full(expand)],
            out_specs=blk(rows, LANES),
            scratch_shapes=[pltpu.VMEM((2 * G * D, past_len), F32), pltpu.VMEM((2 * G * D, past_len), F32),
                            pltpu.VMEM((nsub, CMP_STRIDE * LANES), F32), pltpu.VMEM((nsub, CMP_STRIDE * LANES), F32),
                            pltpu.VMEM((nsub + 8, G * CMP_HIDDEN), F32), pltpu.SemaphoreType.DMA((2,))]),
        out_shape=jax.ShapeDtypeStruct((DB, rows, LANES), F32),
        compiler_params=_params("arbitrary"),
        name="nsa_sample",
    )(page_table, cache_t, qs, gts, new, win_t, perm, pos, wk, wv, w2, cover, expand)
    out = out.reshape(DB, G, HG, TPAD, G, D)[:, :, :, :DS]
    out = jnp.stack([out[:, g, :, :, g] for g in range(G)], axis=1)
    return out.transpose(0, 3, 1, 2, 4).reshape(DB * DS, H * D).astype(BF)


def _memkv_kernel(x_ref, g_ref, w_ref, kg_ref, seg_ref, o_ref, ob_ref):
    z = _dot(_rms(x_ref[...], g_ref[...]).astype(BF), w_ref[...])
    kw = XA_HEADS * XA_HD
    for j in range(2 * kw // LANES):
        sl = slice(j * LANES, (j + 1) * LANES)
        zc = z[:, sl]
        if j * LANES < kw:
            zc = _head_norm(zc, kg_ref[...], seg_ref[...])
        o_ref[:, sl] = zc
        ob_ref[:, sl] = zc.astype(BF)


def memory_kv(mem, g, w, kg, seg, tm=512):
    M, D = mem.shape
    N = w.shape[1]
    row = lambda wd: pl.BlockSpec((tm, wd), lambda i: (i, 0))
    return pl.pallas_call(
        _memkv_kernel,
        grid=(M // tm,),
        in_specs=[row(D), _full((1, D)), _full(w.shape), _full(kg.shape), _full(seg.shape)],
        out_specs=[row(N), row(N)],
        out_shape=[jax.ShapeDtypeStruct((M, N), F32), jax.ShapeDtypeStruct((M, N), BF)],
        compiler_params=_params("parallel"),
        name="memory_kv",
    )(mem, g.reshape(1, D), w, kg, seg)


def _postmix_kernel(x_ref, a_ref, b_ref, wa_ref, wb_ref, gx_ref, wq_ref, qg_ref, seg_ref, mkv_ref, wo_ref, o_ref,
                    *, rows_per_batch, n_mem):
    x1 = x_ref[...] + _dot(a_ref[...], wa_ref[...]) + _dot(b_ref[...], wb_ref[...])
    xn = _rms(x1, gx_ref[...]).astype(BF)
    zq = _dot(xn, wq_ref[...])
    kw = XA_HEADS * XA_HD
    q = jnp.concatenate([_head_norm(zq[:, j * LANES:(j + 1) * LANES], qg_ref[...], seg_ref[...])
                         for j in range(kw // LANES)], axis=1)
    q = (q * XA_HD ** -0.5).astype(BF)
    tm, nk = x1.shape[0], mkv_ref.shape[1]
    mask = None
    if rows_per_batch < tm:
        r = lax.broadcasted_iota(jnp.int32, (tm, nk), 0) // rows_per_batch
        c = lax.broadcasted_iota(jnp.int32, (tm, nk), 1) // n_mem
        mask = r == c
    outs = []
    for h in range(XA_HEADS):
        k = mkv_ref[0, :, h * XA_HD:(h + 1) * XA_HD]
        v = mkv_ref[0, :, kw + h * XA_HD:kw + (h + 1) * XA_HD]
        s = _dot_nt(q[:, h * XA_HD:(h + 1) * XA_HD], k)
        if mask is not None:
            s = jnp.where(mask, s, NEG)
        e = jnp.exp(s - jnp.max(s, axis=1, keepdims=True))
        outs.append(_dot(e.astype(BF), v) / jnp.sum(e, axis=1, keepdims=True))
    o = jnp.concatenate(outs, axis=1).astype(BF)
    o_ref[...] = x1 + _dot(o, wo_ref[...])


def post_mix(x, a, b, wa, wb, gx, wq, qg, seg, mkvb, wo, tm, rows_per_batch):
    M, D = x.shape
    n_mem = mkvb.shape[1]
    if rows_per_batch >= tm:
        per = rows_per_batch // tm
        mspec = pl.BlockSpec((1, n_mem, mkvb.shape[2]), lambda i: (i // per, 0, 0))
    else:
        assert tm == M
        mkvb = mkvb.reshape(1, -1, mkvb.shape[2])
        mspec = _full(mkvb.shape)
    row = lambda wd: pl.BlockSpec((tm, wd), lambda i: (i, 0))
    return pl.pallas_call(
        functools.partial(_postmix_kernel, rows_per_batch=rows_per_batch, n_mem=n_mem),
        grid=(M // tm,),
        in_specs=[row(D), row(a.shape[1]), row(b.shape[1]), _full(wa.shape), _full(wb.shape), _full((1, D)),
                  _full(wq.shape), _full(qg.shape), _full(seg.shape), mspec, _full(wo.shape)],
        out_specs=row(D),
        out_shape=jax.ShapeDtypeStruct((M, D), F32),
        compiler_params=_params("parallel"),
        name="post_mix",
    )(x, a, b, wa, wb, gx.reshape(1, D), wq, qg, seg, mkvb, wo)


def _mlp_kernel(x_ref, g_ref, w1_ref, w2_ref, o_ref, xn_ref, acc_ref):
    f = pl.program_id(1)

    @pl.when(f == 0)
    def _():
        x = x_ref[...]
        xn_ref[...] = _rms(x, g_ref[...]).astype(BF)
        acc_ref[...] = x

    h = _dot(xn_ref[...], w1_ref[...].astype(BF))
    h = jnp.square(jnp.maximum(h, 0.0)).astype(BF)
    acc_ref[...] += _dot(h, w2_ref[...].astype(BF))

    @pl.when(f == pl.num_programs(1) - 1)
    def _():
        o_ref[...] = acc_ref[...]


def mlp_block(x, g, w1b, w2b, layer, tm, tf=512):
    M, D = x.shape
    FF = w1b.shape[2]
    return pl.pallas_call(
        _mlp_kernel,
        grid=(M // tm, FF // tf),
        in_specs=[pl.BlockSpec((tm, D), lambda i, f: (i, 0)),
                  pl.BlockSpec((1, D), lambda i, f: (0, 0)),
                  pl.BlockSpec((None, D, tf), lambda i, f: (layer, 0, f)),
                  pl.BlockSpec((None, tf, D), lambda i, f: (layer, f, 0))],
        out_specs=pl.BlockSpec((tm, D), lambda i, f: (i, 0)),
        out_shape=jax.ShapeDtypeStruct((M, D), F32),
        scratch_shapes=[pltpu.VMEM((tm, D), BF), pltpu.VMEM((tm, D), F32)],
        compiler_params=_params("parallel", "arbitrary"),
        name="mlp",
    )(x, g.reshape(1, D), w1b, w2b)


ML_CHUNK = 128
CONV_PAD = 8


def _dot3(a, b):
    hi = b.astype(BF)
    r1 = b - hi.astype(F32)
    mid = r1.astype(BF)
    lo = (r1 - mid.astype(F32)).astype(BF)
    return _dot(a, hi) + _dot(a, mid) + _dot(a, lo)


def _mlstm_kernel(zqk_ref, zv_ref, zo_ref, gi_ref, cw_ref, cb_ref, gb_ref, hg_ref, tril_ref,
                  c0_ref, n0_ref, m0_ref, cv0_ref,
                  h_out, c_out, n_out, m_out, cv_out,
                  xin_ref, c_ref, n_ref, m_ref, *, n_valid):
    L = ML_CHUNK
    ci = pl.program_id(1)

    @pl.when(ci == 0)
    def _():
        c_ref[...] = c0_ref[0]
        n_ref[...] = n0_ref[0]
        m_ref[...] = m0_ref[0]
        xin_ref[pl.ds(0, CONV_PAD), :] = cv0_ref[0]

    xin_ref[pl.ds(CONV_PAD, L), :] = zqk_ref[...]
    y = cb_ref[...]
    for j in range(CONV_W):
        y = y + cw_ref[j:j + 1, :] * xin_ref[pl.ds(CONV_PAD - (CONV_W - 1) + j, L), :]
    qk = y * (1.0 / (1.0 + jnp.exp(-y)))
    hist = xin_ref[pl.ds(n_valid, CONV_PAD), :]
    xin_ref[pl.ds(0, CONV_PAD), :] = hist
    cv_out[0] = hist

    row = lax.broadcasted_iota(jnp.int32, (L, L), 0)
    col = lax.broadcasted_iota(jnp.int32, (L, L), 1)
    valid = row < n_valid
    gl = gi_ref[...] + gb_ref[...]
    for h in range(ML_HEADS):
        hs = slice(h * ML_HD, (h + 1) * ML_HD)
        q = qk[:, hs]
        k = qk[:, ML_W + h * ML_HD:ML_W + (h + 1) * ML_HD] * (ML_HD ** -0.5)
        v = zv_ref[:, hs]
        gi_col = jnp.broadcast_to(gl[:, NSA_G_W + h:NSA_G_W + h + 1], (L, L))
        gf_col = jnp.broadcast_to(gl[:, NSA_G_W + ML_HEADS + h:NSA_G_W + ML_HEADS + h + 1], (L, L))
        li = jnp.where(valid, gi_col, NEG)
        lf = jnp.where(valid, jnp.minimum(gf_col, 0.0) - jnp.log(1.0 + jnp.exp(-jnp.abs(gf_col))), 0.0)
        b = _dot3(tril_ref[...], lf)
        m_prev = m_ref[h:h + 1, :]
        dm = jnp.where(col <= row, b - (b - li).T, NEG)
        inter = b + m_prev
        m_t = jnp.maximum(inter, jnp.max(dm, axis=1, keepdims=True))
        w = jnp.exp(dm - m_t)
        a = jnp.exp(inter - m_t)
        qb, kb, vb = q.astype(BF), k.astype(BF), v.astype(BF)
        wqk = w * _dot_nt(qb, kb)
        c_old = c_ref[h]
        n_old = n_ref[h:h + 1, :]
        num = a * _dot_nt(qb, c_old.astype(BF)) + _dot(wqk.astype(BF), vb)
        den = a * jnp.sum(q * n_old, axis=1, keepdims=True) + jnp.sum(wqk, axis=1, keepdims=True)
        hh = num / jnp.maximum(jnp.abs(den), jnp.exp(-m_t))
        m_new = m_t[L - 1:L, :]
        b_last = b[L - 1:L, :]
        wk = jnp.exp(b_last - b + li - m_new) * k
        decay = jnp.exp(b_last + m_prev - m_new)
        c_ref[h] = decay * c_old + _dot(v.T.astype(BF), wk.astype(BF))
        n_ref[h:h + 1, :] = decay * n_old + jnp.sum(wk, axis=0, keepdims=True)
        m_ref[h:h + 1, :] = m_new
        hn = hh * lax.rsqrt(jnp.mean(hh * hh, axis=1, keepdims=True) + EPS) * hg_ref[...]
        h_out[:, hs] = (hn * (1.0 / (1.0 + jnp.exp(-zo_ref[:, hs])))).astype(BF)

    @pl.when(ci == pl.num_programs(1) - 1)
    def _():
        c_out[0] = c_ref[...]
        n_out[0] = n_ref[...]
        m_out[0] = m_ref[...]


def mlstm(zqk, zv, zo, gi, conv_w, conv_b, gate_b, hn_g, state, nb, n_valid):
    M = zqk.shape[0]
    nch = M // nb // ML_CHUNK
    assert n_valid == ML_CHUNK or nch == 1
    C0, n0, m0, cv0 = state
    m0 = jnp.broadcast_to(m0[:, :, None], (nb, ML_HEADS, LANES))
    cv0 = jnp.pad(cv0, ((0, 0), (CONV_PAD - (CONV_W - 1), 0), (0, 0)))
    gb = jnp.zeros((1, LANES), F32).at[0, NSA_G_W:NSA_G_W + 2 * ML_HEADS].set(gate_b)
    tril = jnp.asarray(np.tril(np.ones((ML_CHUNK, ML_CHUNK))), BF)
    row = lambda wd: pl.BlockSpec((ML_CHUNK, wd), lambda b, c: (b * nch + c, 0))
    st = lambda *shape: pl.BlockSpec((1,) + shape, lambda b, c: (b,) + (0,) * len(shape))
    h, Cn, nn, mn, cvn = pl.pallas_call(
        functools.partial(_mlstm_kernel, n_valid=n_valid),
        grid=(nb, nch),
        in_specs=[row(2 * ML_W), row(ML_W), row(ML_W), row(LANES),
                  _full(conv_w.shape), _full((1, 2 * ML_W)), _full((1, LANES)), _full((1, ML_HD)), _full(tril.shape),
                  st(ML_HEADS, ML_HD, ML_HD), st(ML_HEADS, ML_HD), st(ML_HEADS, LANES), st(CONV_PAD, 2 * ML_W)],
        out_specs=[row(ML_W), st(ML_HEADS, ML_HD, ML_HD), st(ML_HEADS, ML_HD), st(ML_HEADS, LANES),
                   st(CONV_PAD, 2 * ML_W)],
        out_shape=[jax.ShapeDtypeStruct((M, ML_W), BF),
                   jax.ShapeDtypeStruct((nb, ML_HEADS, ML_HD, ML_HD), F32),
                   jax.ShapeDtypeStruct((nb, ML_HEADS, ML_HD), F32),
                   jax.ShapeDtypeStruct((nb, ML_HEADS, LANES), F32),
                   jax.ShapeDtypeStruct((nb, CONV_PAD, 2 * ML_W), F32)],
        scratch_shapes=[pltpu.VMEM((CONV_PAD + ML_CHUNK, 2 * ML_W), F32),
                        pltpu.VMEM((ML_HEADS, ML_HD, ML_HD), F32),
                        pltpu.VMEM((ML_HEADS, ML_HD), F32),
                        pltpu.VMEM((ML_HEADS, LANES), F32)],
        compiler_params=_params("parallel", "arbitrary"),
        name="mlstm",
    )(zqk, zv, zo, gi, conv_w, conv_b.reshape(1, -1), gb, hn_g.reshape(1, -1), tril, C0, n0, m0, cv0)
    return h, (Cn, nn, mn[:, :, 0], cvn[:, CONV_PAD - (CONV_W - 1):])


def _rope_tables(pos, rows):
    half = ROPE_DIMS // 2
    freqs = ROPE_THETA ** (-jnp.arange(half, dtype=F32) / half)
    ang = pos.astype(F32)[:, None] * freqs
    cos, sin = jnp.cos(ang), jnp.sin(ang)
    n = pos.shape[0]
    one, zero = jnp.ones((n, NSA_HD - ROPE_DIMS), F32), jnp.zeros((n, NSA_HD - ROPE_DIMS), F32)
    zh = jnp.zeros((n, half), F32)
    c = jnp.concatenate([cos, cos, one], axis=1)
    sa = jnp.concatenate([-sin, zh, zero], axis=1)
    sb = jnp.concatenate([zh, sin, zero], axis=1)
    tile = lambda t: jnp.tile(t, (rows // n, LANES // NSA_HD))
    return tile(c), tile(sa), tile(sb)


def _seg_matrix():
    i = np.arange(LANES)
    return jnp.asarray((i[:, None] // NSA_HD == i[None, :] // NSA_HD) / NSA_HD, BF)


def _cover_matrix(nc_rows, lanes):
    ci = np.arange(nc_rows)[:, None] * CMP_STRIDE
    sj = np.arange(lanes)[None, :]
    return jnp.asarray((ci < (sj + 1) * SEL_LEN) & (ci + CMP_LEN > sj * SEL_LEN), BF)


def _expand_matrix(n_chunks):
    j = np.arange(LANES)[None, :, None]
    c = np.arange(n_chunks)[:, None, None]
    s = np.arange(KCH)[None, None, :]
    return jnp.asarray(j == c * (KCH // SEL_LEN) + s // SEL_LEN, BF)


def _in_weights(w):
    zq, zkv, zg, zqk, zv, zo, zif = jnp.split(w, np.cumsum(COL_SPLITS)[:-1].tolist(), axis=1)
    pad = jnp.zeros((w.shape[0], LANES - NSA_G_W - 2 * ML_HEADS), w.dtype)
    return tuple(a.astype(BF) for a in (zq, zkv, zqk, zv, zo, jnp.concatenate([zg, zif, pad], axis=1)))


def _compress_weights(cmp_pos, cmp_w1, cmp_w2):
    G, D, S = NSA_KV_HEADS, NSA_HD, CMP_STRIDE
    pos = jnp.tile(cmp_pos[:, :, None, :], (1, 1, G, 1))
    pos = pos.transpose(1, 0, 2, 3).reshape(CMP_LEN, 2 * G * D)
    w1 = cmp_w1.reshape(2, CMP_LEN, D, CMP_HIDDEN)
    z = jnp.zeros_like(w1)
    w1g = jnp.stack([jnp.concatenate([w1, z], axis=2), jnp.concatenate([z, w1], axis=2)], axis=1)
    return (pos[:S], pos[S:], w1g[:, :, :S].astype(BF), w1g[:, :, S:].astype(BF), cmp_w2.astype(BF))


def kernel(x_prompt, x_sample, mem_prompt, cache_nsa_kv, cache_win_kv, cache_mem_kv, state_mlstm_C, state_mlstm_n, state_mlstm_m, state_conv, page_table, norm_mix_g, w_in, nsa_qn_g, nsa_kn_g, cmp_pos, cmp_w1, cmp_w2, ml_conv_w, ml_conv_b, ml_gate_b, ml_hn_g, w_out, norm_xa_g, norm_mem_g, xa_wq, xa_wkv, xa_qn_g, xa_kn_g, xa_wo, norm_mlp_g, mlp_w1, mlp_w2):
    B, T, D = x_prompt.shape
    DB, DS, _ = x_sample.shape
    depth = w_in.shape[0]
    n_mem = mem_prompt.shape[1]
    P = page_table.shape[1] * cache_nsa_kv.shape[1]
    wkeep = min(WINDOW, T)
    G, HD = NSA_KV_HEADS, NSA_HD
    MP, MS = B * T, DB * DS
    tm_p, tm_s = 512, MS

    pos_p = jnp.arange(T, dtype=jnp.int32)
    pos_s = P + jnp.arange(DS, dtype=jnp.int32)
    tabs_p = _rope_tables(pos_p, T)
    tabs_s = _rope_tables(pos_s, MS)
    seg = _seg_matrix()
    cover = _cover_matrix(T // CMP_STRIDE, LANES)
    expand = _expand_matrix(T // KCH)
    tile2 = lambda v: jnp.tile(v.reshape(-1, HD), (1, LANES // HD))
    pad_chunk = lambda a: jnp.pad(a.reshape(DB, DS, -1), ((0, 0), (0, ML_CHUNK - DS), (0, 0))).reshape(DB * ML_CHUNK, -1)

    n_pool, page = cache_nsa_kv.shape[:2]
    cache_t = cache_nsa_kv.transpose(0, 2, 3, 4, 5, 1).reshape(n_pool, depth, 4 * G * HD, page)
    win_t = cache_win_kv.transpose(0, 1, 3, 4, 5, 2).reshape(depth, DB, 2 * G * HD, cache_win_kv.shape[2])

    xp = x_prompt.reshape(MP, D)
    xs = x_sample.reshape(MS, D)
    mem = mem_prompt.reshape(B * n_mem, D)
    nsa_p, nsa_s, win_p, win_s = [], [], [], []
    C_p, C_s, n_p, n_s, m_p, m_s = [], [], [], [], [], []
    cv_p, cv_s, mem_p = [], [], []
    for l in range(depth):
        w_l = _in_weights(w_in[l])
        qg, kg = tile2(nsa_qn_g[l]), tile2(nsa_kn_g[l])
        cw = _compress_weights(cmp_pos[l], cmp_w1[l], cmp_w2[l])
        wa, wb = w_out[l, :NSA_Q_W].astype(BF), w_out[l, NSA_Q_W:].astype(BF)
        xq, xo = xa_wq[l].astype(BF), xa_wo[l].astype(BF)
        xqg = tile2(xa_qn_g[l])
        w1b, w2b = mlp_w1, mlp_w2

        q, kv, kb, va, zqk, zv, zo, gi = in_proj(xp, norm_mix_g[l], w_l, tabs_p, qg, kg, seg, tm_p)
        kcv = compress_prompt(kv, cw, B, T)
        a_p = nsa_prompt(q, gi, kb, va, kcv, cover, expand, B, T)
        kv3 = kv.reshape(B, T, NSA_KV_W)
        nsa_p.append(kv3[:, :, :4 * G * HD].reshape(B, T, 4, G, HD))
        win_p.append(kv3[:, T - wkeep:, 4 * G * HD:].reshape(B, wkeep, 2, G, HD))
        st0 = (jnp.zeros((B, ML_HEADS, ML_HD, ML_HD), F32), jnp.zeros((B, ML_HEADS, ML_HD), F32),
               jnp.zeros((B, ML_HEADS), F32), jnp.zeros((B, CONV_W - 1, 2 * ML_W), F32))
        b_p, (Cn, nn, mn, buf_p) = mlstm(zqk, zv, zo, gi, ml_conv_w[l], ml_conv_b[l], ml_gate_b[l], ml_hn_g[l],
                                         st0, B, ML_CHUNK)
        C_p.append(Cn); n_p.append(nn); m_p.append(mn); cv_p.append(buf_p)
        mkv, mkvb = memory_kv(mem, norm_mem_g[l], xa_wkv[l].astype(BF), tile2(xa_kn_g[l]), seg)
        mem_p.append(mkv.reshape(B, n_mem, 2, XA_HEADS, XA_HD))
        xp = post_mix(xp, a_p, b_p, wa, wb, norm_xa_g[l], xq, xqg, seg, mkvb.reshape(B, n_mem, -1), xo, tm_p, T)
        xp = mlp_block(xp, norm_mlp_g[l], w1b, w2b, l, tm=1024)

        q, kv, _, _, zqk, zv, zo, gi = in_proj(xs, norm_mix_g[l], w_l, tabs_s, qg, kg, seg, tm_s)
        kvp = kv.reshape(DB, DS, 3, 2, G, HD)
        a_s = nsa_sample(page_table, cache_t, l, q, gi, kv, win_t[l],
                         _sample_compress_weights(cmp_pos[l], cmp_w1[l], cmp_w2[l]), DB, DS)
        nsa_s.append(kvp[:, :, :2].reshape(DB, DS, 4, G, HD))
        win_s.append(kvp[:, :, 2])
        st = (state_mlstm_C[l], state_mlstm_n[l], state_mlstm_m[l], state_conv[l])
        b_s, (Cn, nn, mn, buf_s) = mlstm(*(pad_chunk(a) for a in (zqk, zv, zo, gi)), ml_conv_w[l], ml_conv_b[l],
                                         ml_gate_b[l], ml_hn_g[l], st, DB, DS)
        b_s = b_s.reshape(DB, ML_CHUNK, ML_W)[:, :DS].reshape(MS, ML_W)
        C_s.append(Cn); n_s.append(nn); m_s.append(mn); cv_s.append(buf_s)
        mkvb_s = cache_mem_kv[l].reshape(DB, n_mem, -1).astype(BF)
        xs = post_mix(xs, a_s, b_s, wa, wb, norm_xa_g[l], xq, xqg, seg, mkvb_s, xo, tm_s, DS)
        xs = mlp_block(xs, norm_mlp_g[l], w1b, w2b, l, tm=MS)
    return (xp.reshape(B, T, D), xs.reshape(DB, DS, D),
            jnp.stack(nsa_p, axis=2), jnp.stack(nsa_s, axis=2),
            jnp.stack(win_p),
            jnp.concatenate([cache_win_kv, jnp.stack(win_s)], axis=2)[:, :, -cache_win_kv.shape[2]:],
            jnp.stack(C_p), jnp.stack(C_s),
            jnp.stack(n_p), jnp.stack(n_s),
            jnp.stack(m_p), jnp.stack(m_s),
            jnp.stack(cv_p), jnp.stack(cv_s),
            jnp.stack(mem_p))
```

```python
import functools

import jax
import jax.numpy as jnp
import numpy as np
from jax import lax
from jax.experimental import pallas as pl
from jax.experimental.pallas import tpu as pltpu

F32 = jnp.float32
BF = jnp.bfloat16

EPS = 1e-6
NSA_HEADS, NSA_KV_HEADS, NSA_HD = 8, 2, 64
NSA_GROUP = NSA_HEADS // NSA_KV_HEADS
CMP_LEN, CMP_STRIDE, CMP_HIDDEN = 32, 16, 128
SEL_LEN, N_SEL, WINDOW = 64, 8, 512
ROPE_DIMS, ROPE_THETA = NSA_HD // 4, 500000.0
ML_HEADS, ML_HD, CONV_W = 4, 128, 4
ML_W = ML_HEADS * ML_HD
XA_HEADS, XA_HD = 4, 64
LANES = 128
QB = 256
VMEM_LIMIT = 56 * 1024 * 1024

NSA_Q_W = NSA_HEADS * NSA_HD
NSA_KV_W = 6 * NSA_KV_HEADS * NSA_HD
NSA_G_W = 3 * NSA_HEADS
COL_SPLITS = (NSA_Q_W, NSA_KV_W, NSA_G_W, 2 * ML_W, ML_W, ML_W, 2 * ML_HEADS)
NEG = -1e30


def _dot(a, b):
    return jnp.dot(a, b, preferred_element_type=F32)


def _dot_nt(a, b):
    return lax.dot_general(a, b, (((1,), (1,)), ((), ())), preferred_element_type=F32)


def _dot_hl(a, b):
    hi = a.astype(BF)
    lo = (a - hi.astype(F32)).astype(BF)
    return _dot(hi, b) + _dot(lo, b)


def _rms(x, g):
    return x * lax.rsqrt(jnp.mean(x * x, axis=-1, keepdims=True) + EPS) * g


def _head_norm(z, g, seg):
    return z * lax.rsqrt(_dot_hl(z * z, seg) + EPS) * g


def _rope(y, c, sa, sb):
    half = ROPE_DIMS // 2
    return y * c + pltpu.roll(y, LANES - half, 1) * sa + pltpu.roll(y, half, 1) * sb


def _params(*sem):
    return pltpu.CompilerParams(dimension_semantics=sem, vmem_limit_bytes=VMEM_LIMIT)


def _full(shape):
    n = len(shape)
    return pl.BlockSpec(shape, lambda *_: (0,) * n)


def _inproj_kernel(x_ref, g_ref, wq_ref, wkv_ref, wqk_ref, wv_ref, wo_ref, wgi_ref,
                   rc_ref, rsa_ref, rsb_ref, qg_ref, kg_ref, seg_ref,
                   q_out, kvc_out, kvw_out, kb_out, va_out, zqk_out, zv_out, zo_out, gi_out):
    D = NSA_HD
    xn = _rms(x_ref[...], g_ref[...]).astype(BF)
    c, sa, sb = rc_ref[...], rsa_ref[...], rsb_ref[...]
    seg = seg_ref[...]
    lane = lax.broadcasted_iota(jnp.int32, c.shape, 1)
    low = lane < D
    swap = lambda z: pltpu.roll(z, D, 1)
    zq = _dot(xn, wq_ref[...])
    for j in range(NSA_Q_W // LANES):
        y = _rope(_head_norm(zq[:, j * LANES:(j + 1) * LANES], qg_ref[...], seg), c, sa, sb) * D ** -0.5
        ys = swap(y)
        for i in range(2):
            h = 2 * j + i
            want_low = h // NSA_GROUP == 0
            src = y if want_low == (i == 0) else ys
            q_out[:, h * LANES:(h + 1) * LANES] = jnp.where(low if want_low else ~low, src, 0.0).astype(BF)
    zkv = _dot(xn, wkv_ref[...])
    for j in range(NSA_KV_W // LANES):
        sl = slice(j * LANES, (j + 1) * LANES)
        z = zkv[:, sl]
        br = j // 2
        if j % 2 == 0:
            z = _rope(_head_norm(z, kg_ref[br:br + 1, :], seg), c, sa, sb)
            if br > 0:
                kb_out[:, (br - 1) * LANES:br * LANES] = z.astype(BF)
        elif br > 0:
            one = jnp.where(lane == D, 1.0, 0.0)
            for g, src in enumerate((z, swap(z))):
                col = ((br - 1) * NSA_KV_HEADS + g) * LANES
                va_out[:, col:col + LANES] = jnp.where(low, src, one).astype(BF)
        if br < 2:
            kvc_out[:, sl] = z
        else:
            kvw_out[:, (j - 4) * LANES:(j - 3) * LANES] = z
    zqk_out[...] = _dot(xn, wqk_ref[...])
    zv_out[...] = _dot(xn, wv_ref[...])
    zo_out[...] = _dot(xn, wo_ref[...])
    gi_out[...] = _dot(xn, wgi_ref[...])


def in_proj(x, g, w, rope_tabs, qg, kg, seg, tm):
    M, D = x.shape
    rc, rsa, rsb = rope_tabs
    nt = rc.shape[0] // tm
    widths = (NSA_HEADS * LANES, 4 * LANES, 2 * LANES, 2 * LANES, 4 * LANES, 2 * ML_W, ML_W, ML_W, LANES)
    dtypes = (BF, F32, F32, BF, BF, F32, F32, F32, F32)
    row = lambda wd: pl.BlockSpec((tm, wd), lambda i: (i, 0))
    tab = pl.BlockSpec((tm, LANES), lambda i: (i % nt, 0))
    return pl.pallas_call(
        _inproj_kernel,
        grid=(M // tm,),
        in_specs=[row(D), _full((1, D))] + [_full(a.shape) for a in w] + [tab, tab, tab,
                  _full(qg.shape), _full(kg.shape), _full(seg.shape)],
        out_specs=[row(wd) for wd in widths],
        out_shape=[jax.ShapeDtypeStruct((M, wd), dt) for wd, dt in zip(widths, dtypes)],
        compiler_params=_params("parallel"),
        name="in_proj",
    )(x, g.reshape(1, D), *w, rc, rsa, rsb, qg, kg, seg)


def _compress_body(xk_ref, xv_ref, pa_ref, pb_ref, w1a_ref, w1b_ref, w2_ref, sh_ref, nsub):
    G = NSA_KV_HEADS
    acc_a = [jnp.zeros((nsub, CMP_HIDDEN), F32) for _ in range(2 * G)]
    acc_b = [jnp.zeros((nsub, CMP_HIDDEN), F32) for _ in range(2 * G)]
    for u in range(CMP_STRIDE):
        for kind, x_ref in enumerate((xk_ref, xv_ref)):
            sl = slice(kind * LANES, (kind + 1) * LANES)
            xu = x_ref[pl.ds(u, nsub, stride=CMP_STRIDE), :]
            xa = (xu + pa_ref[u:u + 1, sl]).astype(BF)
            xb = (xu + pb_ref[u:u + 1, sl]).astype(BF)
            for g in range(G):
                acc_a[kind * G + g] += _dot(xa, w1a_ref[kind, g, u])
                acc_b[kind * G + g] += _dot(xb, w1b_ref[kind, g, u])
    sh_ref[pl.ds(nsub, 8), :] = jnp.zeros((8, CMP_HIDDEN), F32)
    outs = []
    for kind in range(2):
        for g in range(G):
            sh_ref[pl.ds(0, nsub), :] = acc_b[kind * G + g]
            h = acc_a[kind * G + g] + sh_ref[pl.ds(1, nsub), :]
            h = h * (1.0 / (1.0 + jnp.exp(-h)))
            outs.append(_dot(h.astype(BF), w2_ref[kind]))
    return outs


def _compress_kernel(xk_ref, xv_ref, pa_ref, pb_ref, w1a_ref, w1b_ref, w2_ref, o_ref, sh_ref, *, nsub):
    kc0, kc1, vc0, vc1 = _compress_body(xk_ref, xv_ref, pa_ref, pb_ref, w1a_ref, w1b_ref, w2_ref, sh_ref, nsub)
    zero = jnp.zeros_like(vc0)
    for j, o in enumerate((kc0, kc1, vc0, zero, vc1, zero)):
        o_ref[0, :, j * NSA_HD:(j + 1) * NSA_HD] = o


def compress_prompt(kv, cw, B, T):
    pa, pb, w1a, w1b, w2 = cw
    nsub = T // CMP_STRIDE
    return pl.pallas_call(
        functools.partial(_compress_kernel, nsub=nsub),
        grid=(B,),
        in_specs=[pl.BlockSpec((T, LANES), lambda b: (b, 0)), pl.BlockSpec((T, LANES), lambda b: (b, 1)),
                  _full(pa.shape), _full(pb.shape), _full(w1a.shape), _full(w1b.shape), _full(w2.shape)],
        out_specs=pl.BlockSpec((1, nsub, 3 * LANES), lambda b: (b, 0, 0)),
        out_shape=jax.ShapeDtypeStruct((B, nsub, 3 * LANES), F32),
        scratch_shapes=[pltpu.VMEM((nsub + 8, CMP_HIDDEN), F32)],
        compiler_params=_params("parallel"),
        name="compress_prompt",
    )(kv, kv, pa, pb, w1a, w1b, w2)


def _select_topk(imp, cur, n_blocks):
    j = lax.broadcasted_iota(jnp.int32, imp.shape, 1)
    jf = j.astype(F32)
    forced = (j == cur) | (j == 0)
    dead = (j > cur) | (j >= n_blocks)
    val = jnp.where(forced, jnp.inf, jnp.where(dead, -jnp.inf, imp))
    sel = jnp.zeros(imp.shape, F32)
    for _ in range(N_SEL):
        m = jnp.max(val, axis=1, keepdims=True)
        idx = jnp.min(jnp.where(val == m, jf, float(imp.shape[1])), axis=1, keepdims=True)
        hit = jf == idx
        sel = jnp.where(hit & (m > -jnp.inf), 1.0, sel)
        val = jnp.where(hit, -jnp.inf, val)
    return sel


KCH = 2 * LANES


def _nsa_prompt_kernel(q_ref, gi_ref, kcv_ref, kb_ref, va_ref, cover_ref, exp_ref, o_ref,
                       s_ref, mx_ref, acc_ref, *, n_sel_blocks):
    G, HG, D, H = NSA_KV_HEADS, NSA_GROUP, NSA_HD, NSA_HEADS
    qb = pl.program_id(1)
    q0 = qb * QB
    qpos = q0 + lax.broadcasted_iota(jnp.int32, (QB, LANES), 0)
    lane = lax.broadcasted_iota(jnp.int32, (QB, LANES), 1)
    gates = 1.0 / (1.0 + jnp.exp(-gi_ref[...]))
    nc = kcv_ref.shape[1]
    qp = jnp.concatenate([q_ref[:, h * LANES:(h + 1) * LANES] for h in range(H)], axis=0)

    s_all = _dot_nt(qp, kcv_ref[0, :, 0:LANES].astype(BF))
    cmask = (lane[:, :nc] * CMP_STRIDE + (CMP_LEN - 1)) <= qpos[:, :nc]
    o_c, psums = [], []
    for g in range(G):
        vc = kcv_ref[0, :, (1 + g) * LANES:(2 + g) * LANES].astype(BF)
        psum = jnp.zeros((QB, nc), F32)
        for h in range(HG):
            hh = g * HG + h
            s = jnp.where(cmask, s_all[hh * QB:(hh + 1) * QB], NEG)
            e = jnp.where(cmask, jnp.exp(s - jnp.max(s, axis=1, keepdims=True)), 0.0)
            den = jnp.sum(e, axis=1, keepdims=True)
            p = e / jnp.where(den > 0, den, 1.0)
            psum = psum + p
            o_c.append(_dot(p.astype(BF), vc))
        psums.append(psum)
    imp = _dot_hl(jnp.concatenate(psums, axis=0), cover_ref[...])
    sel_all = _select_topk(imp, jnp.concatenate([qpos // SEL_LEN] * G, axis=0), n_sel_blocks).astype(BF)
    sels = [sel_all[g * QB:(g + 1) * QB] for g in range(G)]

    qpos2 = q0 + lax.broadcasted_iota(jnp.int32, (QB, KCH), 0)
    lane2 = lax.broadcasted_iota(jnp.int32, (QB, KCH), 1)
    branch_out = []
    for br in range(2):
        mx_ref[...] = jnp.full(mx_ref.shape, NEG, F32)
        acc_ref[...] = jnp.zeros(acc_ref.shape, F32)

        def scores(c, carry, br=br):
            k0 = pl.multiple_of(c * KCH, KCH)
            s_all = _dot_nt(qp, kb_ref[pl.ds(k0, KCH), br * LANES:(br + 1) * LANES])
            d = qpos2 - (k0 + lane2)
            if br == 0:
                masks = [(_dot(sels[g], exp_ref[c]) > 0.5) & (d >= 0) for g in range(G)]
            else:
                masks = [(d >= 0) & (d <= WINDOW)] * G
            for hh in range(H):
                rows = pl.ds(hh * QB, QB)
                s = jnp.where(masks[hh // HG], s_all[hh * QB:(hh + 1) * QB], NEG)
                s_ref[c, rows, :] = s
                mx_ref[rows, :] = functools.reduce(
                    jnp.maximum, [mx_ref[rows, :]] + [s[:, i * LANES:(i + 1) * LANES] for i in range(KCH // LANES)])
            return carry

        def values(c, carry, br=br):
            k0 = pl.multiple_of(c * KCH, KCH)
            for g in range(G):
                rows = pl.ds(g * HG * QB, HG * QB)
                mx = mx_ref[rows, :]
                p = jnp.concatenate([jnp.exp(s_ref[c, rows, pl.ds(i * LANES, LANES)] - mx)
                                     for i in range(KCH // LANES)], axis=1).astype(BF)
                acc_ref[rows, :] += _dot(p, va_ref[pl.ds(k0, KCH), (br * G + g) * LANES:(br * G + g + 1) * LANES])
            return carry

        lo = 0 if br == 0 else jnp.maximum(q0 - WINDOW, 0) // KCH
        hi = (q0 + QB - 1) // KCH + 1
        lax.fori_loop(lo, hi, scores, 0)
        mx_ref[...] = jnp.broadcast_to(jnp.max(mx_ref[...], axis=1, keepdims=True), mx_ref.shape)
        lax.fori_loop(lo, hi, values, 0)
        r = acc_ref[...]
        branch_out.append(r / r[:, D:D + 1])

    for j in range(H // 2):
        tiles = []
        for hh in (2 * j, 2 * j + 1):
            rows = slice(hh * QB, (hh + 1) * QB)
            tiles.append(gates[:, 3 * hh:3 * hh + 1] * o_c[hh]
                         + gates[:, 3 * hh + 1:3 * hh + 2] * branch_out[0][rows]
                         + gates[:, 3 * hh + 2:3 * hh + 3] * branch_out[1][rows])
        o_ref[:, j * LANES:(j + 1) * LANES] = jnp.where(lane < D, tiles[0], pltpu.roll(tiles[1], D, 1)).astype(BF)


def nsa_prompt(q, gi, kb, va, kcv, cover, expand, B, T):
    nqb = T // QB
    nc = kcv.shape[1]
    rows = NSA_HEADS * QB
    return pl.pallas_call(
        functools.partial(_nsa_prompt_kernel, n_sel_blocks=T // SEL_LEN),
        grid=(B, nqb),
        in_specs=[pl.BlockSpec((QB, q.shape[1]), lambda b, i: (b * nqb + i, 0)),
                  pl.BlockSpec((QB, LANES), lambda b, i: (b * nqb + i, 0)),
                  pl.BlockSpec((1, nc, kcv.shape[2]), lambda b, i: (b, 0, 0)),
                  pl.BlockSpec((T, kb.shape[1]), lambda b, i: (b, 0)),
                  pl.BlockSpec((T, va.shape[1]), lambda b, i: (b, 0)),
                  _full(cover.shape), _full(expand.shape)],
        out_specs=pl.BlockSpec((QB, NSA_Q_W), lambda b, i: (b * nqb + i, 0)),
        out_shape=jax.ShapeDtypeStruct((B * T, NSA_Q_W), BF),
        scratch_shapes=[pltpu.VMEM((T // KCH, rows, KCH), F32), pltpu.VMEM((rows, LANES), F32),
                        pltpu.VMEM((rows, LANES), F32)],
        compiler_params=_params("parallel", "arbitrary"),
        name="nsa_prompt",
    )(q, gi, kcv, kb, va, cover, expand)


TPAD = 8


def _masked_softmax_parts(parts):
    m = functools.reduce(jnp.maximum, [jnp.max(jnp.where(k, s, NEG), axis=1, keepdims=True) for s, k in parts])
    es = [jnp.where(k, jnp.exp(jnp.where(k, s, NEG) - m), 0.0) for s, k in parts]
    den = functools.reduce(jnp.add, [jnp.sum(e, axis=1, keepdims=True) for e in es])
    return es, jnp.where(den > 0, den, 1.0)


def _nsa_sample_kernel(pt_ref, ct_ref, q_ref, gt_ref, new_ref, wt_ref, perm_ref, pos_ref, wk_ref, wv_ref, w2_ref,
                       cover_ref, exp_ref, o_ref, cbuf, sbuf, xk_ref, xv_ref, sh_ref, sem,
                       *, layer, n_pages, page, past_len, n_new):
    G, HG, D = NSA_KV_HEADS, NSA_GROUP, NSA_HD
    b = pl.program_id(0)
    nb = pl.num_programs(0)
    half_rows = 2 * G * D
    bufs = (cbuf, sbuf)
    sub_per_page = page // CMP_STRIDE

    def page_copy(bb, p, half):
        return pltpu.make_async_copy(
            ct_ref.at[pt_ref[bb, p], layer, pl.ds(half * half_rows, half_rows), :],
            bufs[half].at[:, pl.ds(pl.multiple_of(p * page, page), page)], sem.at[half])

    def start_all(bb, half):
        lax.fori_loop(0, n_pages, lambda p, c: (page_copy(bb, p, half).start(), c)[1], 0)

    def wait_all(bb, half):
        lax.fori_loop(0, n_pages, lambda p, c: (page_copy(bb, p, half).wait(), c)[1], 0)

    @pl.when(b == 0)
    def _():
        start_all(0, 0)
        start_all(0, 1)

    wait_all(b, 0)

    def to_rows(p, c):
        p0 = pl.multiple_of(p * page, page)
        z = _dot_nt(perm_ref[...], cbuf[:, pl.ds(p0, page)].astype(BF))
        r0 = pl.multiple_of(p * sub_per_page, sub_per_page)
        for u in range(CMP_STRIDE):
            zu = z[u * sub_per_page:(u + 1) * sub_per_page]
            xk_ref[pl.ds(r0, sub_per_page), u * LANES:(u + 1) * LANES] = zu[:, 0:G * D]
            xv_ref[pl.ds(r0, sub_per_page), u * LANES:(u + 1) * LANES] = zu[:, G * D:2 * G * D]
        return c

    lax.fori_loop(0, n_pages, to_rows, 0, unroll=4)

    @pl.when(b + 1 < nb)
    def _():
        start_all(b + 1, 0)

    nsub = past_len // CMP_STRIDE
    half_w = G * CMP_HIDDEN
    sh_ref[pl.ds(nsub, 8), :] = jnp.zeros((8, half_w), F32)
    kcv = []
    for kind, (x_ref, w_ref) in enumerate(((xk_ref, wk_ref), (xv_ref, wv_ref))):
        bias = _dot(pos_ref[kind], w_ref[...])
        h_all = _dot(x_ref[...].astype(BF), w_ref[...])
        sh_ref[pl.ds(0, nsub), :] = h_all[:, half_w:] + bias[1:2, half_w:]
        h = h_all[:, :half_w] + bias[0:1, :half_w] + sh_ref[pl.ds(1, nsub), :]
        h = h * (1.0 / (1.0 + jnp.exp(-h)))
        kcv.append(_dot(h.astype(BF), w2_ref[kind]).astype(BF))
    kc, vc = kcv

    rows = G * HG * TPAD
    q = q_ref[0]
    t_rows = lax.broadcasted_iota(jnp.int32, (rows, LANES), 0) % TPAD
    lane_r = lax.broadcasted_iota(jnp.int32, (rows, LANES), 1)
    new_ok = (lane_r <= t_rows) & (lane_r < n_new)
    tile_gh = lambda xs: jnp.concatenate([x for x in xs for _ in range(HG)], axis=0)

    s = _dot_nt(q, kc)
    c_end = lax.broadcasted_iota(jnp.int32, s.shape, 1) * CMP_STRIDE + (CMP_LEN - 1)
    q_pos = past_len + lax.broadcasted_iota(jnp.int32, s.shape, 0) % TPAD
    (e,), den = _masked_softmax_parts([(s, c_end <= q_pos)])
    p = e / den
    o_c = _dot(p.astype(BF), vc)
    psum = jnp.concatenate([functools.reduce(jnp.add, [p[(g * HG + h) * TPAD:(g * HG + h + 1) * TPAD]
                                                       for h in range(HG)]) for g in range(G)], axis=0)
    imp = _dot_hl(psum, cover_ref[...])
    cur = (past_len + lax.broadcasted_iota(jnp.int32, imp.shape, 0) % TPAD) // SEL_LEN
    sel = _select_topk(imp, cur, -(-(past_len + n_new) // SEL_LEN))

    wait_all(b, 1)
    n_past_blocks = past_len // SEL_LEN
    selx = _dot(sel[:, :n_past_blocks].astype(BF), exp_ref[...])
    in_new = jnp.broadcast_to(sel[:, n_past_blocks:n_past_blocks + 1], (G * TPAD, LANES))
    split_g = lambda x: [x[g * TPAD:(g + 1) * TPAD] for g in range(G)]
    s_past = _dot(q, sbuf[0:G * D, :].astype(BF))
    s_new = _dot_nt(q, new_ref[0, 0])
    (e_past, e_new), den = _masked_softmax_parts(
        [(s_past, tile_gh(split_g(selx)) > 0.5), (s_new, (tile_gh(split_g(in_new)) > 0.5) & new_ok)])
    o_t = _dot_nt(sbuf[G * D:2 * G * D, :].astype(BF), e_past.astype(BF))
    o_s = (o_t.T + _dot(e_new.astype(BF), new_ref[0, 1])) / den

    @pl.when(b + 1 < nb)
    def _():
        start_all(b + 1, 1)

    wb = wt_ref.shape[2]
    s_buf = _dot(q, wt_ref[0, 0:G * D, :].astype(BF))
    i = lax.broadcasted_iota(jnp.int32, s_buf.shape, 1)
    t = lax.broadcasted_iota(jnp.int32, s_buf.shape, 0) % TPAD
    s_new = _dot_nt(q, new_ref[0, 2])
    (e_buf, e_new), den = _masked_softmax_parts([(s_buf, wb + t - i <= WINDOW), (s_new, new_ok)])
    o_t = _dot_nt(wt_ref[0, G * D:2 * G * D, :].astype(BF), e_buf.astype(BF))
    o_w = (o_t.T + _dot(e_new.astype(BF), new_ref[0, 3])) / den
    gates = 1.0 / (1.0 + jnp.exp(-gt_ref[0]))
    o_ref[0] = gates[:, 0:1] * o_c + gates[:, 1:2] * o_s + gates[:, 2:3] * o_w


def _sample_compress_weights(cmp_pos, cmp_w1, cmp_w2):
    G, D, S, Hd = NSA_KV_HEADS, NSA_HD, CMP_STRIDE, CMP_HIDDEN
    w1 = cmp_w1.reshape(2, 2, S, D, Hd)
    eye = jnp.eye(G, dtype=w1.dtype)
    wk = jnp.einsum('khudn,gj->kugdhjn', w1, eye).reshape(2, S * G * D, 2 * G * Hd).astype(BF)
    pos = cmp_pos.reshape(2, 2, S, 1, D)
    pos = jnp.broadcast_to(pos, (2, 2, S, G, D)).reshape(2, 2, S * G * D)
    pos = jnp.pad(pos, ((0, 0), (0, 8 - 2), (0, 0))).astype(BF)
    w2 = jnp.einsum('knd,gj->kgnjd', cmp_w2, eye).reshape(2, G * Hd, G * D).astype(BF)
    return wk[0], wk[1], pos, w2


def nsa_sample(page_table, cache_t, layer, q, gi, kv, win_t, cw, DB, DS):
    G, HG, D, H = NSA_KV_HEADS, NSA_GROUP, NSA_HD, NSA_HEADS
    n_pages, page = page_table.shape[1], cache_t.shape[3]
    past_len = n_pages * page
    rows = H * TPAD
    wk, wv, pos, w2 = cw
    heads = lambda a, w: a.reshape(DB, DS, H, w).transpose(0, 2, 1, 3)
    padt = lambda a: jnp.pad(a, ((0, 0), (0, 0), (0, TPAD - DS), (0, 0)))
    qs = padt(heads(q, LANES)).reshape(DB, rows, LANES)
    gts = padt(heads(gi[:, :NSA_G_W], 3)).reshape(DB, rows, 3)
    gts = jnp.pad(gts, ((0, 0), (0, 0), (0, LANES - 3)))
    new = kv[:, 2 * G * D:].reshape(DB, DS, 4, G * D).transpose(0, 2, 1, 3)
    new = jnp.pad(new, ((0, 0), (0, 0), (0, LANES - DS), (0, 0))).astype(BF)
    nsub = past_len // CMP_STRIDE
    sub_per_page = page // CMP_STRIDE
    cover = _cover_matrix(nsub, 2 * LANES)
    tok = np.arange(past_len)[None, :] // SEL_LEN
    expand = jnp.asarray(np.arange(past_len // SEL_LEN)[:, None] == tok, BF)
    r = np.arange(page)
    perm = jnp.asarray((r % sub_per_page)[:, None] * CMP_STRIDE + (r // sub_per_page)[:, None] == r[None, :], BF)
    blk = lambda *s: pl.BlockSpec((1,) + s, lambda b, pt: (b,) + (0,) * len(s))
    full = lambda a: pl.BlockSpec(a.shape, lambda b, pt: (0,) * a.ndim)
    out = pl.pallas_call(
        functools.partial(_nsa_sample_kernel, layer=layer, n_pages=n_pages, page=page, past_len=past_len, n_new=DS),
        grid_spec=pltpu.PrefetchScalarGridSpec(
            num_scalar_prefetch=1,
            grid=(DB,),
            in_specs=[pl.BlockSpec(memory_space=pl.ANY), blk(rows, LANES), blk(rows, LANES),
                      blk(4, LANES, G * D), blk(win_t.shape[1], win_t.shape[2]),
                      full(perm), full(pos), full(wk), full(wv), full(w2), full(cover), full(expand)],
            out_specs=blk(rows, LANES),
            scratch_shapes=[pltpu.VMEM((2 * G * D, past_len), F32), pltpu.VMEM((2 * G * D, past_len), F32),
                            pltpu.VMEM((nsub, CMP_STRIDE * LANES), F32), pltpu.VMEM((nsub, CMP_STRIDE * LANES), F32),
                            pltpu.VMEM((nsub + 8, G * CMP_HIDDEN), F32), pltpu.SemaphoreType.DMA((2,))]),
        out_shape=jax.ShapeDtypeStruct((DB, rows, LANES), F32),
        compiler_params=_params("arbitrary"),
        name="nsa_sample",
    )(page_table, cache_t, qs, gts, new, win_t, perm, pos, wk, wv, w2, cover, expand)
    out = out.reshape(DB, G, HG, TPAD, G, D)[:, :, :, :DS]
    out = jnp.stack([out[:, g, :, :, g] for g in range(G)], axis=1)
    return out.transpose(0, 3, 1, 2, 4).reshape(DB * DS, H * D).astype(BF)


def _memkv_kernel(x_ref, g_ref, w_ref, kg_ref, seg_ref, o_ref, ob_ref):
    z = _dot(_rms(x_ref[...], g_ref[...]).astype(BF), w_ref[...])
    kw = XA_HEADS * XA_HD
    for j in range(2 * kw // LANES):
        sl = slice(j * LANES, (j + 1) * LANES)
        zc = z[:, sl]
        if j * LANES < kw:
            zc = _head_norm(zc, kg_ref[...], seg_ref[...])
        o_ref[:, sl] = zc
        ob_ref[:, sl] = zc.astype(BF)


def memory_kv(mem, g, w, kg, seg, tm=512):
    M, D = mem.shape
    N = w.shape[1]
    row = lambda wd: pl.BlockSpec((tm, wd), lambda i: (i, 0))
    return pl.pallas_call(
        _memkv_kernel,
        grid=(M // tm,),
        in_specs=[row(D), _full((1, D)), _full(w.shape), _full(kg.shape), _full(seg.shape)],
        out_specs=[row(N), row(N)],
        out_shape=[jax.ShapeDtypeStruct((M, N), F32), jax.ShapeDtypeStruct((M, N), BF)],
        compiler_params=_params("parallel"),
        name="memory_kv",
    )(mem, g.reshape(1, D), w, kg, seg)


def _postmix_kernel(x_ref, a_ref, b_ref, wa_ref, wb_ref, gx_ref, wq_ref, qg_ref, seg_ref, mkv_ref, wo_ref, o_ref,
                    *, rows_per_batch, n_mem):
    x1 = x_ref[...] + _dot(a_ref[...], wa_ref[...]) + _dot(b_ref[...], wb_ref[...])
    xn = _rms(x1, gx_ref[...]).astype(BF)
    zq = _dot(xn, wq_ref[...])
    kw = XA_HEADS * XA_HD
    q = jnp.concatenate([_head_norm(zq[:, j * LANES:(j + 1) * LANES], qg_ref[...], seg_ref[...])
                         for j in range(kw // LANES)], axis=1)
    q = (q * XA_HD ** -0.5).astype(BF)
    tm, nk = x1.shape[0], mkv_ref.shape[1]
    mask = None
    if rows_per_batch < tm:
        r = lax.broadcasted_iota(jnp.int32, (tm, nk), 0) // rows_per_batch
        c = lax.broadcasted_iota(jnp.int32, (tm, nk), 1) // n_mem
        mask = r == c
    outs = []
    for h in range(XA_HEADS):
        k = mkv_ref[0, :, h * XA_HD:(h + 1) * XA_HD]
        v = mkv_ref[0, :, kw + h * XA_HD:kw + (h + 1) * XA_HD]
        s = _dot_nt(q[:, h * XA_HD:(h + 1) * XA_HD], k)
        if mask is not None:
            s = jnp.where(mask, s, NEG)
        e = jnp.exp(s - jnp.max(s, axis=1, keepdims=True))
        outs.append(_dot(e.astype(BF), v) / jnp.sum(e, axis=1, keepdims=True))
    o = jnp.concatenate(outs, axis=1).astype(BF)
    o_ref[...] = x1 + _dot(o, wo_ref[...])


def post_mix(x, a, b, wa, wb, gx, wq, qg, seg, mkvb, wo, tm, rows_per_batch):
    M, D = x.shape
    n_mem = mkvb.shape[1]
    if rows_per_batch >= tm:
        per = rows_per_batch // tm
        mspec = pl.BlockSpec((1, n_mem, mkvb.shape[2]), lambda i: (i // per, 0, 0))
    else:
        assert tm == M
        mkvb = mkvb.reshape(1, -1, mkvb.shape[2])
        mspec = _full(mkvb.shape)
    row = lambda wd: pl.BlockSpec((tm, wd), lambda i: (i, 0))
    return pl.pallas_call(
        functools.partial(_postmix_kernel, rows_per_batch=rows_per_batch, n_mem=n_mem),
        grid=(M // tm,),
        in_specs=[row(D), row(a.shape[1]), row(b.shape[1]), _full(wa.shape), _full(wb.shape), _full((1, D)),
                  _full(wq.shape), _full(qg.shape), _full(seg.shape), mspec, _full(wo.shape)],
        out_specs=row(D),
        out_shape=jax.ShapeDtypeStruct((M, D), F32),
        compiler_params=_params("parallel"),
        name="post_mix",
    )(x, a, b, wa, wb, gx.reshape(1, D), wq, qg, seg, mkvb, wo)


def _mlp_kernel(x_ref, g_ref, w1_ref, w2_ref, o_ref, xn_ref, acc_ref):
    f = pl.program_id(1)

    @pl.when(f == 0)
    def _():
        x = x_ref[...]
        xn_ref[...] = _rms(x, g_ref[...]).astype(BF)
        acc_ref[...] = x

    h = _dot(xn_ref[...], w1_ref[...].astype(BF))
    h = jnp.square(jnp.maximum(h, 0.0)).astype(BF)
    acc_ref[...] += _dot(h, w2_ref[...].astype(BF))

    @pl.when(f == pl.num_programs(1) - 1)
    def _():
        o_ref[...] = acc_ref[...]


def mlp_block(x, g, w1b, w2b, layer, tm, tf=512):
    M, D = x.shape
    FF = w1b.shape[2]
    return pl.pallas_call(
        _mlp_kernel,
        grid=(M // tm, FF // tf),
        in_specs=[pl.BlockSpec((tm, D), lambda i, f: (i, 0)),
                  pl.BlockSpec((1, D), lambda i, f: (0, 0)),
                  pl.BlockSpec((None, D, tf), lambda i, f: (layer, 0, f)),
                  pl.BlockSpec((None, tf, D), lambda i, f: (layer, f, 0))],
        out_specs=pl.BlockSpec((tm, D), lambda i, f: (i, 0)),
        out_shape=jax.ShapeDtypeStruct((M, D), F32),
        scratch_shapes=[pltpu.VMEM((tm, D), BF), pltpu.VMEM((tm, D), F32)],
        compiler_params=_params("parallel", "arbitrary"),
        name="mlp",
    )(x, g.reshape(1, D), w1b, w2b)


ML_CHUNK = 128
CONV_PAD = 8


def _dot3(a, b):
    hi = b.astype(BF)
    r1 = b - hi.astype(F32)
    mid = r1.astype(BF)
    lo = (r1 - mid.astype(F32)).astype(BF)
    return _dot(a, hi) + _dot(a, mid) + _dot(a, lo)


def _mlstm_kernel(zqk_ref, zv_ref, zo_ref, gi_ref, cw_ref, cb_ref, gb_ref, hg_ref, tril_ref,
                  c0_ref, n0_ref, m0_ref, cv0_ref,
                  h_out, c_out, n_out, m_out, cv_out,
                  xin_ref, c_ref, n_ref, m_ref, *, n_valid):
    L = ML_CHUNK
    ci = pl.program_id(1)

    @pl.when(ci == 0)
    def _():
        c_ref[...] = c0_ref[0]
        n_ref[...] = n0_ref[0]
        m_ref[...] = m0_ref[0]
        xin_ref[pl.ds(0, CONV_PAD), :] = cv0_ref[0]

    xin_ref[pl.ds(CONV_PAD, L), :] = zqk_ref[...]
    y = cb_ref[...]
    for j in range(CONV_W):
        y = y + cw_ref[j:j + 1, :] * xin_ref[pl.ds(CONV_PAD - (CONV_W - 1) + j, L), :]
    qk = y * (1.0 / (1.0 + jnp.exp(-y)))
    hist = xin_ref[pl.ds(n_valid, CONV_PAD), :]
    xin_ref[pl.ds(0, CONV_PAD), :] = hist
    cv_out[0] = hist

    row = lax.broadcasted_iota(jnp.int32, (L, L), 0)
    col = lax.broadcasted_iota(jnp.int32, (L, L), 1)
    valid = row < n_valid
    gl = gi_ref[...] + gb_ref[...]
    for h in range(ML_HEADS):
        hs = slice(h * ML_HD, (h + 1) * ML_HD)
        q = qk[:, hs]
        k = qk[:, ML_W + h * ML_HD:ML_W + (h + 1) * ML_HD] * (ML_HD ** -0.5)
        v = zv_ref[:, hs]
        gi_col = jnp.broadcast_to(gl[:, NSA_G_W + h:NSA_G_W + h + 1], (L, L))
        gf_col = jnp.broadcast_to(gl[:, NSA_G_W + ML_HEADS + h:NSA_G_W + ML_HEADS + h + 1], (L, L))
        li = jnp.where(valid, gi_col, NEG)
        lf = jnp.where(valid, jnp.minimum(gf_col, 0.0) - jnp.log(1.0 + jnp.exp(-jnp.abs(gf_col))), 0.0)
        b = _dot3(tril_ref[...], lf)
        m_prev = m_ref[h:h + 1, :]
        dm = jnp.where(col <= row, b - (b - li).T, NEG)
        inter = b + m_prev
        m_t = jnp.maximum(inter, jnp.max(dm, axis=1, keepdims=True))
        w = jnp.exp(dm - m_t)
        a = jnp.exp(inter - m_t)
        qb, kb, vb = q.astype(BF), k.astype(BF), v.astype(BF)
        wqk = w * _dot_nt(qb, kb)
        c_old = c_ref[h]
        n_old = n_ref[h:h + 1, :]
        num = a * _dot_nt(qb, c_old.astype(BF)) + _dot(wqk.astype(BF), vb)
        den = a * jnp.sum(q * n_old, axis=1, keepdims=True) + jnp.sum(wqk, axis=1, keepdims=True)
        hh = num / jnp.maximum(jnp.abs(den), jnp.exp(-m_t))
        m_new = m_t[L - 1:L, :]
        b_last = b[L - 1:L, :]
        wk = jnp.exp(b_last - b + li - m_new) * k
        decay = jnp.exp(b_last + m_prev - m_new)
        c_ref[h] = decay * c_old + _dot(v.T.astype(BF), wk.astype(BF))
        n_ref[h:h + 1, :] = decay * n_old + jnp.sum(wk, axis=0, keepdims=True)
        m_ref[h:h + 1, :] = m_new
        hn = hh * lax.rsqrt(jnp.mean(hh * hh, axis=1, keepdims=True) + EPS) * hg_ref[...]
        h_out[:, hs] = (hn * (1.0 / (1.0 + jnp.exp(-zo_ref[:, hs])))).astype(BF)

    @pl.when(ci == pl.num_programs(1) - 1)
    def _():
        c_out[0] = c_ref[...]
        n_out[0] = n_ref[...]
        m_out[0] = m_ref[...]


def mlstm(zqk, zv, zo, gi, conv_w, conv_b, gate_b, hn_g, state, nb, n_valid):
    M = zqk.shape[0]
    nch = M // nb // ML_CHUNK
    assert n_valid == ML_CHUNK or nch == 1
    C0, n0, m0, cv0 = state
    m0 = jnp.broadcast_to(m0[:, :, None], (nb, ML_HEADS, LANES))
    cv0 = jnp.pad(cv0, ((0, 0), (CONV_PAD - (CONV_W - 1), 0), (0, 0)))
    gb = jnp.zeros((1, LANES), F32).at[0, NSA_G_W:NSA_G_W + 2 * ML_HEADS].set(gate_b)
    tril = jnp.asarray(np.tril(np.ones((ML_CHUNK, ML_CHUNK))), BF)
    row = lambda wd: pl.BlockSpec((ML_CHUNK, wd), lambda b, c: (b * nch + c, 0))
    st = lambda *shape: pl.BlockSpec((1,) + shape, lambda b, c: (b,) + (0,) * len(shape))
    h, Cn, nn, mn, cvn = pl.pallas_call(
        functools.partial(_mlstm_kernel, n_valid=n_valid),
        grid=(nb, nch),
        in_specs=[row(2 * ML_W), row(ML_W), row(ML_W), row(LANES),
                  _full(conv_w.shape), _full((1, 2 * ML_W)), _full((1, LANES)), _full((1, ML_HD)), _full(tril.shape),
                  st(ML_HEADS, ML_HD, ML_HD), st(ML_HEADS, ML_HD), st(ML_HEADS, LANES), st(CONV_PAD, 2 * ML_W)],
        out_specs=[row(ML_W), st(ML_HEADS, ML_HD, ML_HD), st(ML_HEADS, ML_HD), st(ML_HEADS, LANES),
                   st(CONV_PAD, 2 * ML_W)],
        out_shape=[jax.ShapeDtypeStruct((M, ML_W), BF),
                   jax.ShapeDtypeStruct((nb, ML_HEADS, ML_HD, ML_HD), F32),
                   jax.ShapeDtypeStruct((nb, ML_HEADS, ML_HD), F32),
                   jax.ShapeDtypeStruct((nb, ML_HEADS, LANES), F32),
                   jax.ShapeDtypeStruct((nb, CONV_PAD, 2 * ML_W), F32)],
        scratch_shapes=[pltpu.VMEM((CONV_PAD + ML_CHUNK, 2 * ML_W), F32),
                        pltpu.VMEM((ML_HEADS, ML_HD, ML_HD), F32),
                        pltpu.VMEM((ML_HEADS, ML_HD), F32),
                        pltpu.VMEM((ML_HEADS, LANES), F32)],
        compiler_params=_params("parallel", "arbitrary"),
        name="mlstm",
    )(zqk, zv, zo, gi, conv_w, conv_b.reshape(1, -1), gb, hn_g.reshape(1, -1), tril, C0, n0, m0, cv0)
    return h, (Cn, nn, mn[:, :, 0], cvn[:, CONV_PAD - (CONV_W - 1):])


def _rope_tables(pos, rows):
    half = ROPE_DIMS // 2
    freqs = ROPE_THETA ** (-jnp.arange(half, dtype=F32) / half)
    ang = pos.astype(F32)[:, None] * freqs
    cos, sin = jnp.cos(ang), jnp.sin(ang)
    n = pos.shape[0]
    one, zero = jnp.ones((n, NSA_HD - ROPE_DIMS), F32), jnp.zeros((n, NSA_HD - ROPE_DIMS), F32)
    zh = jnp.zeros((n, half), F32)
    c = jnp.concatenate([cos, cos, one], axis=1)
    sa = jnp.concatenate([-sin, zh, zero], axis=1)
    sb = jnp.concatenate([zh, sin, zero], axis=1)
    tile = lambda t: jnp.tile(t, (rows // n, LANES // NSA_HD))
    return tile(c), tile(sa), tile(sb)


def _seg_matrix():
    i = np.arange(LANES)
    return jnp.asarray((i[:, None] // NSA_HD == i[None, :] // NSA_HD) / NSA_HD, BF)


def _cover_matrix(nc_rows, lanes):
    ci = np.arange(nc_rows)[:, None] * CMP_STRIDE
    sj = np.arange(lanes)[None, :]
    return jnp.asarray((ci < (sj + 1) * SEL_LEN) & (ci + CMP_LEN > sj * SEL_LEN), BF)


def _expand_matrix(n_chunks):
    j = np.arange(LANES)[None, :, None]
    c = np.arange(n_chunks)[:, None, None]
    s = np.arange(KCH)[None, None, :]
    return jnp.asarray(j == c * (KCH // SEL_LEN) + s // SEL_LEN, BF)


def _in_weights(w):
    zq, zkv, zg, zqk, zv, zo, zif = jnp.split(w, np.cumsum(COL_SPLITS)[:-1].tolist(), axis=1)
    pad = jnp.zeros((w.shape[0], LANES - NSA_G_W - 2 * ML_HEADS), w.dtype)
    return tuple(a.astype(BF) for a in (zq, zkv, zqk, zv, zo, jnp.concatenate([zg, zif, pad], axis=1)))


def _compress_weights(cmp_pos, cmp_w1, cmp_w2):
    G, D, S = NSA_KV_HEADS, NSA_HD, CMP_STRIDE
    pos = jnp.tile(cmp_pos[:, :, None, :], (1, 1, G, 1))
    pos = pos.transpose(1, 0, 2, 3).reshape(CMP_LEN, 2 * G * D)
    w1 = cmp_w1.reshape(2, CMP_LEN, D, CMP_HIDDEN)
    z = jnp.zeros_like(w1)
    w1g = jnp.stack([jnp.concatenate([w1, z], axis=2), jnp.concatenate([z, w1], axis=2)], axis=1)
    return (pos[:S], pos[S:], w1g[:, :, :S].astype(BF), w1g[:, :, S:].astype(BF), cmp_w2.astype(BF))


def kernel(x_prompt, x_sample, mem_prompt, cache_nsa_kv, cache_win_kv, cache_mem_kv, state_mlstm_C, state_mlstm_n, state_mlstm_m, state_conv, page_table, norm_mix_g, w_in, nsa_qn_g, nsa_kn_g, cmp_pos, cmp_w1, cmp_w2, ml_conv_w, ml_conv_b, ml_gate_b, ml_hn_g, w_out, norm_xa_g, norm_mem_g, xa_wq, xa_wkv, xa_qn_g, xa_kn_g, xa_wo, norm_mlp_g, mlp_w1, mlp_w2):
    B, T, D = x_prompt.shape
    DB, DS, _ = x_sample.shape
    depth = w_in.shape[0]
    n_mem = mem_prompt.shape[1]
    P = page_table.shape[1] * cache_nsa_kv.shape[1]
    wkeep = min(WINDOW, T)
    G, HD = NSA_KV_HEADS, NSA_HD
    MP, MS = B * T, DB * DS
    tm_p, tm_s = 512, MS

    pos_p = jnp.arange(T, dtype=jnp.int32)
    pos_s = P + jnp.arange(DS, dtype=jnp.int32)
    tabs_p = _rope_tables(pos_p, T)
    tabs_s = _rope_tables(pos_s, MS)
    seg = _seg_matrix()
    cover = _cover_matrix(T // CMP_STRIDE, LANES)
    expand = _expand_matrix(T // KCH)
    tile2 = lambda v: jnp.tile(v.reshape(-1, HD), (1, LANES // HD))
    pad_chunk = lambda a: jnp.pad(a.reshape(DB, DS, -1), ((0, 0), (0, ML_CHUNK - DS), (0, 0))).reshape(DB * ML_CHUNK, -1)

    n_pool, page = cache_nsa_kv.shape[:2]
    cache_t = cache_nsa_kv.transpose(0, 2, 3, 4, 5, 1).reshape(n_pool, depth, 4 * G * HD, page)
    win_t = cache_win_kv.transpose(0, 1, 3, 4, 5, 2).reshape(depth, DB, 2 * G * HD, cache_win_kv.shape[2])

    xp = x_prompt.reshape(MP, D)
    xs = x_sample.reshape(MS, D)
    mem = mem_prompt.reshape(B * n_mem, D)
    nsa_p, nsa_s, win_p, win_s = [], [], [], []
    C_p, C_s, n_p, n_s, m_p, m_s = [], [], [], [], [], []
    cv_p, cv_s, mem_p = [], [], []
    for l in range(depth):
        w_l = _in_weights(w_in[l])
        qg, kg = tile2(nsa_qn_g[l]), tile2(nsa_kn_g[l])
        cw = _compress_weights(cmp_pos[l], cmp_w1[l], cmp_w2[l])
        wa, wb = w_out[l, :NSA_Q_W].astype(BF), w_out[l, NSA_Q_W:].astype(BF)
        xq, xo = xa_wq[l].astype(BF), xa_wo[l].astype(BF)
        xqg = tile2(xa_qn_g[l])
        w1b, w2b = mlp_w1, mlp_w2

        q, kvc, kvw, kb, va, zqk, zv, zo, gi = in_proj(xp, norm_mix_g[l], w_l, tabs_p, qg, kg, seg, tm_p)
        kcv = compress_prompt(kvc, cw, B, T)
        a_p = nsa_prompt(q, gi, kb, va, kcv, cover, expand, B, T)
        nsa_p.append(kvc.reshape(B, T, 4, G, HD))
        win_p.append(kvw.reshape(B, T, 2, G, HD)[:, T - wkeep:])
        st0 = (jnp.zeros((B, ML_HEADS, ML_HD, ML_HD), F32), jnp.zeros((B, ML_HEADS, ML_HD), F32),
               jnp.zeros((B, ML_HEADS), F32), jnp.zeros((B, CONV_W - 1, 2 * ML_W), F32))
        b_p, (Cn, nn, mn, buf_p) = mlstm(zqk, zv, zo, gi, ml_conv_w[l], ml_conv_b[l], ml_gate_b[l], ml_hn_g[l],
                                         st0, B, ML_CHUNK)
        C_p.append(Cn); n_p.append(nn); m_p.append(mn); cv_p.append(buf_p)
        mkv, mkvb = memory_kv(mem, norm_mem_g[l], xa_wkv[l].astype(BF), tile2(xa_kn_g[l]), seg)
        mem_p.append(mkv.reshape(B, n_mem, 2, XA_HEADS, XA_HD))
        xp = post_mix(xp, a_p, b_p, wa, wb, norm_xa_g[l], xq, xqg, seg, mkvb.reshape(B, n_mem, -1), xo, tm_p, T)
        xp = mlp_block(xp, norm_mlp_g[l], w1b, w2b, l, tm=1024)

        q, kvc, kvw, _, _, zqk, zv, zo, gi = in_proj(xs, norm_mix_g[l], w_l, tabs_s, qg, kg, seg, tm_s)
        kv = jnp.concatenate([kvc, kvw], axis=1)
        kvp = kv.reshape(DB, DS, 3, 2, G, HD)
        a_s = nsa_sample(page_table, cache_t, l, q, gi, kv, win_t[l],
                         _sample_compress_weights(cmp_pos[l], cmp_w1[l], cmp_w2[l]), DB, DS)
        nsa_s.append(kvp[:, :, :2].reshape(DB, DS, 4, G, HD))
        win_s.append(kvp[:, :, 2])
        st = (state_mlstm_C[l], state_mlstm_n[l], state_mlstm_m[l], state_conv[l])
        b_s, (Cn, nn, mn, buf_s) = mlstm(*(pad_chunk(a) for a in (zqk, zv, zo, gi)), ml_conv_w[l], ml_conv_b[l],
                                         ml_gate_b[l], ml_hn_g[l], st, DB, DS)
        b_s = b_s.reshape(DB, ML_CHUNK, ML_W)[:, :DS].reshape(MS, ML_W)
        C_s.append(Cn); n_s.append(nn); m_s.append(mn); cv_s.append(buf_s)
        mkvb_s = cache_mem_kv[l].reshape(DB, n_mem, -1).astype(BF)
        xs = post_mix(xs, a_s, b_s, wa, wb, norm_xa_g[l], xq, xqg, seg, mkvb_s, xo, tm_s, DS)
        xs = mlp_block(xs, norm_mlp_g[l], w1b, w2b, l, tm=MS)
    return (xp.reshape(B, T, D), xs.reshape(DB, DS, D),
            jnp.stack(nsa_p, axis=2), jnp.stack(nsa_s, axis=2),
            jnp.stack(win_p),
            jnp.concatenate([cache_win_kv, jnp.stack(win_s)], axis=2)[:, :, -cache_win_kv.shape[2]:],
            jnp.stack(C_p), jnp.stack(C_s),
            jnp.stack(n_p), jnp.stack(n_s),
            jnp.stack(m_p), jnp.stack(m_s),
            jnp.stack(cv_p), jnp.stack(cv_s),
            jnp.stack(mem_p))
```

```python
import functools

import jax
import jax.numpy as jnp
import numpy as np
from jax import lax
from jax.experimental import pallas as pl
from jax.experimental.pallas import tpu as pltpu

F32 = jnp.float32
BF = jnp.bfloat16

EPS = 1e-6
NSA_HEADS, NSA_KV_HEADS, NSA_HD = 8, 2, 64
NSA_GROUP = NSA_HEADS // NSA_KV_HEADS
CMP_LEN, CMP_STRIDE, CMP_HIDDEN = 32, 16, 128
SEL_LEN, N_SEL, WINDOW = 64, 8, 512
ROPE_DIMS, ROPE_THETA = NSA_HD // 4, 500000.0
ML_HEADS, ML_HD, CONV_W = 4, 128, 4
ML_W = ML_HEADS * ML_HD
XA_HEADS, XA_HD = 4, 64
LANES = 128
QB = 256
VMEM_LIMIT = 56 * 1024 * 1024

NSA_Q_W = NSA_HEADS * NSA_HD
NSA_KV_W = 6 * NSA_KV_HEADS * NSA_HD
NSA_G_W = 3 * NSA_HEADS
COL_SPLITS = (NSA_Q_W, NSA_KV_W, NSA_G_W, 2 * ML_W, ML_W, ML_W, 2 * ML_HEADS)
NEG = -1e30


def _dot(a, b):
    return jnp.dot(a, b, preferred_element_type=F32)


def _dot_nt(a, b):
    return lax.dot_general(a, b, (((1,), (1,)), ((), ())), preferred_element_type=F32)


def _dot_hl(a, b):
    hi = a.astype(BF)
    lo = (a - hi.astype(F32)).astype(BF)
    return _dot(hi, b) + _dot(lo, b)


def _rms(x, g):
    return x * lax.rsqrt(jnp.mean(x * x, axis=-1, keepdims=True) + EPS) * g


def _head_norm(z, g, seg):
    return z * lax.rsqrt(_dot_hl(z * z, seg) + EPS) * g


def _rope(y, c, sa, sb):
    half = ROPE_DIMS // 2
    return y * c + pltpu.roll(y, LANES - half, 1) * sa + pltpu.roll(y, half, 1) * sb


def _params(*sem):
    return pltpu.CompilerParams(dimension_semantics=sem, vmem_limit_bytes=VMEM_LIMIT)


def _full(shape):
    n = len(shape)
    return pl.BlockSpec(shape, lambda *_: (0,) * n)


def _inproj_kernel(x_ref, g_ref, wq_ref, wkv_ref, wqk_ref, wv_ref, wo_ref, wgi_ref,
                   rc_ref, rsa_ref, rsb_ref, qg_ref, kg_ref, seg_ref,
                   q_out, kvc_out, kvw_out, kb_out, va_out, zqk_out, zv_out, zo_out, gi_out):
    D = NSA_HD
    xn = _rms(x_ref[...], g_ref[...]).astype(BF)
    c, sa, sb = rc_ref[...], rsa_ref[...], rsb_ref[...]
    seg = seg_ref[...]
    lane = lax.broadcasted_iota(jnp.int32, c.shape, 1)
    low = lane < D
    swap = lambda z: pltpu.roll(z, D, 1)
    zq = _dot(xn, wq_ref[...])
    for j in range(NSA_Q_W // LANES):
        y = _rope(_head_norm(zq[:, j * LANES:(j + 1) * LANES], qg_ref[...], seg), c, sa, sb) * D ** -0.5
        ys = swap(y)
        for i in range(2):
            h = 2 * j + i
            want_low = h // NSA_GROUP == 0
            src = y if want_low == (i == 0) else ys
            q_out[:, h * LANES:(h + 1) * LANES] = jnp.where(low if want_low else ~low, src, 0.0).astype(BF)
    zkv = _dot(xn, wkv_ref[...])
    for j in range(NSA_KV_W // LANES):
        sl = slice(j * LANES, (j + 1) * LANES)
        z = zkv[:, sl]
        br = j // 2
        if j % 2 == 0:
            z = _rope(_head_norm(z, kg_ref[br:br + 1, :], seg), c, sa, sb)
            if br > 0:
                kb_out[:, (br - 1) * LANES:br * LANES] = z.astype(BF)
        elif br > 0:
            one = jnp.where(lane == D, 1.0, 0.0)
            for g, src in enumerate((z, swap(z))):
                col = ((br - 1) * NSA_KV_HEADS + g) * LANES
                va_out[:, col:col + LANES] = jnp.where(low, src, one).astype(BF)
        if br < 2:
            kvc_out[:, sl] = z
        else:
            kvw_out[:, (j - 4) * LANES:(j - 3) * LANES] = z
    zqk_out[...] = _dot(xn, wqk_ref[...])
    zv_out[...] = _dot(xn, wv_ref[...])
    zo_out[...] = _dot(xn, wo_ref[...])
    gi_out[...] = _dot(xn, wgi_ref[...])


def in_proj(x, g, w, rope_tabs, qg, kg, seg, tm):
    M, D = x.shape
    rc, rsa, rsb = rope_tabs
    nt = rc.shape[0] // tm
    widths = (NSA_HEADS * LANES, 4 * LANES, 2 * LANES, 2 * LANES, 4 * LANES, 2 * ML_W, ML_W, ML_W, LANES)
    dtypes = (BF, F32, F32, BF, BF, F32, F32, F32, F32)
    row = lambda wd: pl.BlockSpec((tm, wd), lambda i: (i, 0))
    tab = pl.BlockSpec((tm, LANES), lambda i: (i % nt, 0))
    return pl.pallas_call(
        _inproj_kernel,
        grid=(M // tm,),
        in_specs=[row(D), _full((1, D))] + [_full(a.shape) for a in w] + [tab, tab, tab,
                  _full(qg.shape), _full(kg.shape), _full(seg.shape)],
        out_specs=[row(wd) for wd in widths],
        out_shape=[jax.ShapeDtypeStruct((M, wd), dt) for wd, dt in zip(widths, dtypes)],
        compiler_params=_params("parallel"),
        name="in_proj",
    )(x, g.reshape(1, D), *w, rc, rsa, rsb, qg, kg, seg)


def _compress_body(xk_ref, xv_ref, pa_ref, pb_ref, w1a_ref, w1b_ref, w2_ref, sh_ref, nsub):
    G = NSA_KV_HEADS
    acc_a = [jnp.zeros((nsub, CMP_HIDDEN), F32) for _ in range(2 * G)]
    acc_b = [jnp.zeros((nsub, CMP_HIDDEN), F32) for _ in range(2 * G)]
    for u in range(CMP_STRIDE):
        for kind, x_ref in enumerate((xk_ref, xv_ref)):
            sl = slice(kind * LANES, (kind + 1) * LANES)
            xu = x_ref[pl.ds(u, nsub, stride=CMP_STRIDE), :]
            xa = (xu + pa_ref[u:u + 1, sl]).astype(BF)
            xb = (xu + pb_ref[u:u + 1, sl]).astype(BF)
            for g in range(G):
                acc_a[kind * G + g] += _dot(xa, w1a_ref[kind, g, u])
                acc_b[kind * G + g] += _dot(xb, w1b_ref[kind, g, u])
    sh_ref[pl.ds(nsub, 8), :] = jnp.zeros((8, CMP_HIDDEN), F32)
    outs = []
    for kind in range(2):
        for g in range(G):
            sh_ref[pl.ds(0, nsub), :] = acc_b[kind * G + g]
            h = acc_a[kind * G + g] + sh_ref[pl.ds(1, nsub), :]
            h = h * (1.0 / (1.0 + jnp.exp(-h)))
            outs.append(_dot(h.astype(BF), w2_ref[kind]))
    return outs


def _compress_kernel(xk_ref, xv_ref, pa_ref, pb_ref, w1a_ref, w1b_ref, w2_ref, o_ref, sh_ref, *, nsub):
    kc0, kc1, vc0, vc1 = _compress_body(xk_ref, xv_ref, pa_ref, pb_ref, w1a_ref, w1b_ref, w2_ref, sh_ref, nsub)
    zero = jnp.zeros_like(vc0)
    for j, o in enumerate((kc0, kc1, vc0, zero, vc1, zero)):
        o_ref[0, :, j * NSA_HD:(j + 1) * NSA_HD] = o


def compress_prompt(kv, cw, B, T):
    pa, pb, w1a, w1b, w2 = cw
    nsub = T // CMP_STRIDE
    return pl.pallas_call(
        functools.partial(_compress_kernel, nsub=nsub),
        grid=(B,),
        in_specs=[pl.BlockSpec((T, LANES), lambda b: (b, 0)), pl.BlockSpec((T, LANES), lambda b: (b, 1)),
                  _full(pa.shape), _full(pb.shape), _full(w1a.shape), _full(w1b.shape), _full(w2.shape)],
        out_specs=pl.BlockSpec((1, nsub, 3 * LANES), lambda b: (b, 0, 0)),
        out_shape=jax.ShapeDtypeStruct((B, nsub, 3 * LANES), F32),
        scratch_shapes=[pltpu.VMEM((nsub + 8, CMP_HIDDEN), F32)],
        compiler_params=_params("parallel"),
        name="compress_prompt",
    )(kv, kv, pa, pb, w1a, w1b, w2)


def _select_topk(imp, cur, n_blocks, axis=1):
    j = lax.broadcasted_iota(jnp.int32, imp.shape, axis)
    jf = j.astype(F32)
    forced = (j == cur) | (j == 0)
    dead = (j > cur) | (j >= n_blocks)
    val = jnp.where(forced, jnp.inf, jnp.where(dead, -jnp.inf, imp))
    sel = jnp.zeros(imp.shape, F32)
    for _ in range(N_SEL):
        m = jnp.max(val, axis=axis, keepdims=True)
        idx = jnp.min(jnp.where(val == m, jf, float(imp.shape[axis])), axis=axis, keepdims=True)
        hit = jf == idx
        sel = jnp.where(hit & (m > -jnp.inf), 1.0, sel)
        val = jnp.where(hit, -jnp.inf, val)
    return sel


KCH = 2 * LANES


def _nsa_prompt_kernel(q_ref, gi_ref, kcv_ref, kb_ref, va_ref, cover_ref, exp_ref, o_ref,
                       s_ref, mx_ref, acc_ref, *, n_sel_blocks):
    G, HG, D, H = NSA_KV_HEADS, NSA_GROUP, NSA_HD, NSA_HEADS
    qb = pl.program_id(1)
    q0 = qb * QB
    qpos = q0 + lax.broadcasted_iota(jnp.int32, (QB, LANES), 0)
    lane = lax.broadcasted_iota(jnp.int32, (QB, LANES), 1)
    gates = 1.0 / (1.0 + jnp.exp(-gi_ref[...]))
    nc = kcv_ref.shape[1]
    qp = jnp.concatenate([q_ref[:, h * LANES:(h + 1) * LANES] for h in range(H)], axis=0)

    s_all = _dot_nt(qp, kcv_ref[0, :, 0:LANES].astype(BF))
    cmask = (lane[:, :nc] * CMP_STRIDE + (CMP_LEN - 1)) <= qpos[:, :nc]
    o_c, psums = [], []
    for g in range(G):
        vc = kcv_ref[0, :, (1 + g) * LANES:(2 + g) * LANES].astype(BF)
        psum = jnp.zeros((QB, nc), F32)
        for h in range(HG):
            hh = g * HG + h
            s = jnp.where(cmask, s_all[hh * QB:(hh + 1) * QB], NEG)
            e = jnp.where(cmask, jnp.exp(s - jnp.max(s, axis=1, keepdims=True)), 0.0)
            den = jnp.sum(e, axis=1, keepdims=True)
            p = e / jnp.where(den > 0, den, 1.0)
            psum = psum + p
            o_c.append(_dot(p.astype(BF), vc))
        psums.append(psum)
    psum = jnp.concatenate(psums, axis=0)
    hi = psum.astype(BF)
    lo = (psum - hi.astype(F32)).astype(BF)
    imp_t = _dot_nt(cover_ref[...], hi) + _dot_nt(cover_ref[...], lo)
    nb8 = -(-n_sel_blocks // 8) * 8
    cur_t = (q0 + lax.broadcasted_iota(jnp.int32, (nb8, G * QB), 1) % QB) // SEL_LEN
    sel_t = _select_topk(imp_t[:nb8], cur_t, n_sel_blocks, axis=0)
    sel_t = jnp.concatenate([sel_t, jnp.zeros((LANES - nb8, G * QB), F32)], axis=0)
    sel_all = sel_t.T.astype(BF)
    sels = [sel_all[g * QB:(g + 1) * QB] for g in range(G)]

    qpos2 = q0 + lax.broadcasted_iota(jnp.int32, (QB, KCH), 0)
    lane2 = lax.broadcasted_iota(jnp.int32, (QB, KCH), 1)
    branch_out = []
    for br in range(2):
        mx_ref[...] = jnp.full(mx_ref.shape, NEG, F32)
        acc_ref[...] = jnp.zeros(acc_ref.shape, F32)

        def scores(c, carry, br=br):
            k0 = pl.multiple_of(c * KCH, KCH)
            s_all = _dot_nt(qp, kb_ref[pl.ds(k0, KCH), br * LANES:(br + 1) * LANES])
            d = qpos2 - (k0 + lane2)
            if br == 0:
                masks = [(_dot(sels[g], exp_ref[c]) > 0.5) & (d >= 0) for g in range(G)]
            else:
                masks = [(d >= 0) & (d <= WINDOW)] * G
            for hh in range(H):
                rows = pl.ds(hh * QB, QB)
                s = jnp.where(masks[hh // HG], s_all[hh * QB:(hh + 1) * QB], NEG)
                s_ref[c, rows, :] = s
                mx_ref[rows, :] = functools.reduce(
                    jnp.maximum, [mx_ref[rows, :]] + [s[:, i * LANES:(i + 1) * LANES] for i in range(KCH // LANES)])
            return carry

        def values(c, carry, br=br):
            k0 = pl.multiple_of(c * KCH, KCH)
            for g in range(G):
                rows = pl.ds(g * HG * QB, HG * QB)
                mx = mx_ref[rows, :]
                p = jnp.concatenate([jnp.exp(s_ref[c, rows, pl.ds(i * LANES, LANES)] - mx)
                                     for i in range(KCH // LANES)], axis=1).astype(BF)
                acc_ref[rows, :] += _dot(p, va_ref[pl.ds(k0, KCH), (br * G + g) * LANES:(br * G + g + 1) * LANES])
            return carry

        lo = 0 if br == 0 else jnp.maximum(q0 - WINDOW, 0) // KCH
        hi = (q0 + QB - 1) // KCH + 1
        lax.fori_loop(lo, hi, scores, 0)
        mx_ref[...] = jnp.broadcast_to(jnp.max(mx_ref[...], axis=1, keepdims=True), mx_ref.shape)
        lax.fori_loop(lo, hi, values, 0)
        r = acc_ref[...]
        branch_out.append(r / r[:, D:D + 1])

    for j in range(H // 2):
        tiles = []
        for hh in (2 * j, 2 * j + 1):
            rows = slice(hh * QB, (hh + 1) * QB)
            tiles.append(gates[:, 3 * hh:3 * hh + 1] * o_c[hh]
                         + gates[:, 3 * hh + 1:3 * hh + 2] * branch_out[0][rows]
                         + gates[:, 3 * hh + 2:3 * hh + 3] * branch_out[1][rows])
        o_ref[:, j * LANES:(j + 1) * LANES] = jnp.where(lane < D, tiles[0], pltpu.roll(tiles[1], D, 1)).astype(BF)


def nsa_prompt(q, gi, kb, va, kcv, cover, expand, B, T):
    nqb = T // QB
    nc = kcv.shape[1]
    rows = NSA_HEADS * QB
    return pl.pallas_call(
        functools.partial(_nsa_prompt_kernel, n_sel_blocks=T // SEL_LEN),
        grid=(B, nqb),
        in_specs=[pl.BlockSpec((QB, q.shape[1]), lambda b, i: (b * nqb + i, 0)),
                  pl.BlockSpec((QB, LANES), lambda b, i: (b * nqb + i, 0)),
                  pl.BlockSpec((1, nc, kcv.shape[2]), lambda b, i: (b, 0, 0)),
                  pl.BlockSpec((T, kb.shape[1]), lambda b, i: (b, 0)),
                  pl.BlockSpec((T, va.shape[1]), lambda b, i: (b, 0)),
                  _full(cover.shape), _full(expand.shape)],
        out_specs=pl.BlockSpec((QB, NSA_Q_W), lambda b, i: (b * nqb + i, 0)),
        out_shape=jax.ShapeDtypeStruct((B * T, NSA_Q_W), BF),
        scratch_shapes=[pltpu.VMEM((T // KCH, rows, KCH), F32), pltpu.VMEM((rows, LANES), F32),
                        pltpu.VMEM((rows, LANES), F32)],
        compiler_params=_params("parallel", "arbitrary"),
        name="nsa_prompt",
    )(q, gi, kcv, kb, va, cover, expand)


TPAD = 8


def _masked_softmax_parts(parts):
    m = functools.reduce(jnp.maximum, [jnp.max(jnp.where(k, s, NEG), axis=1, keepdims=True) for s, k in parts])
    es = [jnp.where(k, jnp.exp(jnp.where(k, s, NEG) - m), 0.0) for s, k in parts]
    den = functools.reduce(jnp.add, [jnp.sum(e, axis=1, keepdims=True) for e in es])
    return es, jnp.where(den > 0, den, 1.0)


def _nsa_sample_kernel(pt_ref, ct_ref, q_ref, gt_ref, new_ref, wt_ref, perm_ref, pos_ref, wk_ref, wv_ref, w2_ref,
                       cover_ref, exp_ref, o_ref, cbuf, sbuf, xk_ref, xv_ref, sh_ref, sem,
                       *, layer, n_pages, page, past_len, n_new):
    G, HG, D = NSA_KV_HEADS, NSA_GROUP, NSA_HD
    b = pl.program_id(0)
    nb = pl.num_programs(0)
    half_rows = 2 * G * D
    bufs = (cbuf, sbuf)
    sub_per_page = page // CMP_STRIDE

    def page_copy(bb, p, half):
        return pltpu.make_async_copy(
            ct_ref.at[pt_ref[bb, p], layer, pl.ds(half * half_rows, half_rows), :],
            bufs[half].at[:, pl.ds(pl.multiple_of(p * page, page), page)], sem.at[half])

    def start_all(bb, half):
        lax.fori_loop(0, n_pages, lambda p, c: (page_copy(bb, p, half).start(), c)[1], 0)

    def wait_all(bb, half):
        lax.fori_loop(0, n_pages, lambda p, c: (page_copy(bb, p, half).wait(), c)[1], 0)

    @pl.when(b == 0)
    def _():
        start_all(0, 0)
        start_all(0, 1)

    wait_all(b, 0)

    def to_rows(p, c):
        p0 = pl.multiple_of(p * page, page)
        z = _dot_nt(perm_ref[...], cbuf[:, pl.ds(p0, page)].astype(BF))
        r0 = pl.multiple_of(p * sub_per_page, sub_per_page)
        for u in range(CMP_STRIDE):
            zu = z[u * sub_per_page:(u + 1) * sub_per_page]
            xk_ref[pl.ds(r0, sub_per_page), u * LANES:(u + 1) * LANES] = zu[:, 0:G * D]
            xv_ref[pl.ds(r0, sub_per_page), u * LANES:(u + 1) * LANES] = zu[:, G * D:2 * G * D]
        return c

    lax.fori_loop(0, n_pages, to_rows, 0, unroll=4)

    @pl.when(b + 1 < nb)
    def _():
        start_all(b + 1, 0)

    nsub = past_len // CMP_STRIDE
    half_w = G * CMP_HIDDEN
    sh_ref[pl.ds(nsub, 8), :] = jnp.zeros((8, half_w), F32)
    kcv = []
    for kind, (x_ref, w_ref) in enumerate(((xk_ref, wk_ref), (xv_ref, wv_ref))):
        bias = _dot(pos_ref[kind], w_ref[...])
        h_all = _dot(x_ref[...].astype(BF), w_ref[...])
        sh_ref[pl.ds(0, nsub), :] = h_all[:, half_w:] + bias[1:2, half_w:]
        h = h_all[:, :half_w] + bias[0:1, :half_w] + sh_ref[pl.ds(1, nsub), :]
        h = h * (1.0 / (1.0 + jnp.exp(-h)))
        kcv.append(_dot(h.astype(BF), w2_ref[kind]).astype(BF))
    kc, vc = kcv

    rows = G * HG * TPAD
    q = q_ref[0]
    t_rows = lax.broadcasted_iota(jnp.int32, (rows, LANES), 0) % TPAD
    lane_r = lax.broadcasted_iota(jnp.int32, (rows, LANES), 1)
    new_ok = (lane_r <= t_rows) & (lane_r < n_new)
    tile_gh = lambda xs: jnp.concatenate([x for x in xs for _ in range(HG)], axis=0)

    s = _dot_nt(q, kc)
    c_end = lax.broadcasted_iota(jnp.int32, s.shape, 1) * CMP_STRIDE + (CMP_LEN - 1)
    q_pos = past_len + lax.broadcasted_iota(jnp.int32, s.shape, 0) % TPAD
    (e,), den = _masked_softmax_parts([(s, c_end <= q_pos)])
    p = e / den
    o_c = _dot(p.astype(BF), vc)
    psum = jnp.concatenate([functools.reduce(jnp.add, [p[(g * HG + h) * TPAD:(g * HG + h + 1) * TPAD]
                                                       for h in range(HG)]) for g in range(G)], axis=0)
    imp = _dot_hl(psum, cover_ref[...])
    cur = (past_len + lax.broadcasted_iota(jnp.int32, imp.shape, 0) % TPAD) // SEL_LEN
    sel = _select_topk(imp, cur, -(-(past_len + n_new) // SEL_LEN))

    wait_all(b, 1)
    n_past_blocks = past_len // SEL_LEN
    selx = _dot(sel[:, :n_past_blocks].astype(BF), exp_ref[...])
    in_new = jnp.broadcast_to(sel[:, n_past_blocks:n_past_blocks + 1], (G * TPAD, LANES))
    split_g = lambda x: [x[g * TPAD:(g + 1) * TPAD] for g in range(G)]
    s_past = _dot(q, sbuf[0:G * D, :].astype(BF))
    s_new = _dot_nt(q, new_ref[0, 0])
    (e_past, e_new), den = _masked_softmax_parts(
        [(s_past, tile_gh(split_g(selx)) > 0.5), (s_new, (tile_gh(split_g(in_new)) > 0.5) & new_ok)])
    o_t = _dot_nt(sbuf[G * D:2 * G * D, :].astype(BF), e_past.astype(BF))
    o_s = (o_t.T + _dot(e_new.astype(BF), new_ref[0, 1])) / den

    @pl.when(b + 1 < nb)
    def _():
        start_all(b + 1, 1)

    wb = wt_ref.shape[2]
    s_buf = _dot(q, wt_ref[0, 0:G * D, :].astype(BF))
    i = lax.broadcasted_iota(jnp.int32, s_buf.shape, 1)
    t = lax.broadcasted_iota(jnp.int32, s_buf.shape, 0) % TPAD
    s_new = _dot_nt(q, new_ref[0, 2])
    (e_buf, e_new), den = _masked_softmax_parts([(s_buf, wb + t - i <= WINDOW), (s_new, new_ok)])
    o_t = _dot_nt(wt_ref[0, G * D:2 * G * D, :].astype(BF), e_buf.astype(BF))
    o_w = (o_t.T + _dot(e_new.astype(BF), new_ref[0, 3])) / den
    gates = 1.0 / (1.0 + jnp.exp(-gt_ref[0]))
    o_ref[0] = gates[:, 0:1] * o_c + gates[:, 1:2] * o_s + gates[:, 2:3] * o_w


def _sample_compress_weights(cmp_pos, cmp_w1, cmp_w2):
    G, D, S, Hd = NSA_KV_HEADS, NSA_HD, CMP_STRIDE, CMP_HIDDEN
    w1 = cmp_w1.reshape(2, 2, S, D, Hd)
    eye = jnp.eye(G, dtype=w1.dtype)
    wk = jnp.einsum('khudn,gj->kugdhjn', w1, eye).reshape(2, S * G * D, 2 * G * Hd).astype(BF)
    pos = cmp_pos.reshape(2, 2, S, 1, D)
    pos = jnp.broadcast_to(pos, (2, 2, S, G, D)).reshape(2, 2, S * G * D)
    pos = jnp.pad(pos, ((0, 0), (0, 8 - 2), (0, 0))).astype(BF)
    w2 = jnp.einsum('knd,gj->kgnjd', cmp_w2, eye).reshape(2, G * Hd, G * D).astype(BF)
    return wk[0], wk[1], pos, w2


def nsa_sample(page_table, cache_t, layer, q, gi, kv, win_t, cw, DB, DS):
    G, HG, D, H = NSA_KV_HEADS, NSA_GROUP, NSA_HD, NSA_HEADS
    n_pages, page = page_table.shape[1], cache_t.shape[3]
    past_len = n_pages * page
    rows = H * TPAD
    wk, wv, pos, w2 = cw
    heads = lambda a, w: a.reshape(DB, DS, H, w).transpose(0, 2, 1, 3)
    padt = lambda a: jnp.pad(a, ((0, 0), (0, 0), (0, TPAD - DS), (0, 0)))
    qs = padt(heads(q, LANES)).reshape(DB, rows, LANES)
    gts = padt(heads(gi[:, :NSA_G_W], 3)).reshape(DB, rows, 3)
    gts = jnp.pad(gts, ((0, 0), (0, 0), (0, LANES - 3)))
    new = kv[:, 2 * G * D:].reshape(DB, DS, 4, G * D).transpose(0, 2, 1, 3)
    new = jnp.pad(new, ((0, 0), (0, 0), (0, LANES - DS), (0, 0))).astype(BF)
    nsub = past_len // CMP_STRIDE
    sub_per_page = page // CMP_STRIDE
    cover = _cover_matrix(nsub, 2 * LANES)
    tok = np.arange(past_len)[None, :] // SEL_LEN
    expand = jnp.asarray(np.arange(past_len // SEL_LEN)[:, None] == tok, BF)
    r = np.arange(page)
    perm = jnp.asarray((r % sub_per_page)[:, None] * CMP_STRIDE + (r // sub_per_page)[:, None] == r[None, :], BF)
    blk = lambda *s: pl.BlockSpec((1,) + s, lambda b, pt: (b,) + (0,) * len(s))
    full = lambda a: pl.BlockSpec(a.shape, lambda b, pt: (0,) * a.ndim)
    out = pl.pallas_call(
        functools.partial(_nsa_sample_kernel, layer=layer, n_pages=n_pages, page=page, past_len=past_len, n_new=DS),
        grid_spec=pltpu.PrefetchScalarGridSpec(
            num_scalar_prefetch=1,
            grid=(DB,),
            in_specs=[pl.BlockSpec(memory_space=pl.ANY), blk(rows, LANES), blk(rows, LANES),
                      blk(4, LANES, G * D), blk(win_t.shape[1], win_t.shape[2]),
                      full(perm), full(pos), full(wk), full(wv), full(w2), full(cover), full(expand)],
            out_specs=blk(rows, LANES),
            scratch_shapes=[pltpu.VMEM((2 * G * D, past_len), F32), pltpu.VMEM((2 * G * D, past_len), F32),
                            pltpu.VMEM((nsub, CMP_STRIDE * LANES), F32), pltpu.VMEM((nsub, CMP_STRIDE * LANES), F32),
                            pltpu.VMEM((nsub + 8, G * CMP_HIDDEN), F32), pltpu.SemaphoreType.DMA((2,))]),
        out_shape=jax.ShapeDtypeStruct((DB, rows, LANES), F32),
        compiler_params=_params("arbitrary"),
        name="nsa_sample",
    )(page_table, cache_t, qs, gts, new, win_t, perm, pos, wk, wv, w2, cover, expand)
    out = out.reshape(DB, G, HG, TPAD, G, D)[:, :, :, :DS]
    out = jnp.stack([out[:, g, :, :, g] for g in range(G)], axis=1)
    return out.transpose(0, 3, 1, 2, 4).reshape(DB * DS, H * D).astype(BF)


def _memkv_kernel(x_ref, g_ref, w_ref, kg_ref, seg_ref, o_ref, ob_ref):
    z = _dot(_rms(x_ref[...], g_ref[...]).astype(BF), w_ref[...])
    kw = XA_HEADS * XA_HD
    for j in range(2 * kw // LANES):
        sl = slice(j * LANES, (j + 1) * LANES)
        zc = z[:, sl]
        if j * LANES < kw:
            zc = _head_norm(zc, kg_ref[...], seg_ref[...])
        o_ref[:, sl] = zc
        ob_ref[:, sl] = zc.astype(BF)


def memory_kv(mem, g, w, kg, seg, tm=512):
    M, D = mem.shape
    N = w.shape[1]
    row = lambda wd: pl.BlockSpec((tm, wd), lambda i: (i, 0))
    return pl.pallas_call(
        _memkv_kernel,
        grid=(M // tm,),
        in_specs=[row(D), _full((1, D)), _full(w.shape), _full(kg.shape), _full(seg.shape)],
        out_specs=[row(N), row(N)],
        out_shape=[jax.ShapeDtypeStruct((M, N), F32), jax.ShapeDtypeStruct((M, N), BF)],
        compiler_params=_params("parallel"),
        name="memory_kv",
    )(mem, g.reshape(1, D), w, kg, seg)


def _postmix_kernel(x_ref, a_ref, b_ref, wa_ref, wb_ref, gx_ref, wq_ref, qg_ref, seg_ref, mkv_ref, wo_ref, o_ref,
                    *, rows_per_batch, n_mem):
    x1 = x_ref[...] + _dot(a_ref[...], wa_ref[...]) + _dot(b_ref[...], wb_ref[...])
    xn = _rms(x1, gx_ref[...]).astype(BF)
    zq = _dot(xn, wq_ref[...])
    kw = XA_HEADS * XA_HD
    q = jnp.concatenate([_head_norm(zq[:, j * LANES:(j + 1) * LANES], qg_ref[...], seg_ref[...])
                         for j in range(kw // LANES)], axis=1)
    q = (q * XA_HD ** -0.5).astype(BF)
    tm = x1.shape[0]
    many = rows_per_batch < tm
    mask = None
    if many:
        nk = mkv_ref.shape[1]
        r = lax.broadcasted_iota(jnp.int32, (tm, nk), 0) // rows_per_batch
        c = lax.broadcasted_iota(jnp.int32, (tm, nk), 1) // n_mem
        mask = r == c
    outs = []
    for h in range(XA_HEADS):
        qh = q[:, h * XA_HD:(h + 1) * XA_HD]
        if many:
            s = _dot(qh, mkv_ref[h * XA_HD:(h + 1) * XA_HD, :])
            s = jnp.where(mask, s, NEG)
        else:
            s = _dot_nt(qh, mkv_ref[0, :, h * XA_HD:(h + 1) * XA_HD])
        e = jnp.exp(s - jnp.max(s, axis=1, keepdims=True))
        if many:
            pv = _dot_nt(e.astype(BF), mkv_ref[kw + h * XA_HD:kw + (h + 1) * XA_HD, :])
        else:
            pv = _dot(e.astype(BF), mkv_ref[0, :, kw + h * XA_HD:kw + (h + 1) * XA_HD])
        outs.append(pv / jnp.sum(e, axis=1, keepdims=True))
    o = jnp.concatenate(outs, axis=1).astype(BF)
    o_ref[...] = x1 + _dot(o, wo_ref[...])


def post_mix(x, a, b, wa, wb, gx, wq, qg, seg, mkvb, wo, tm, rows_per_batch):
    M, D = x.shape
    if rows_per_batch >= tm:
        n_mem = mkvb.shape[1]
        per = rows_per_batch // tm
        mspec = pl.BlockSpec((1, n_mem, mkvb.shape[2]), lambda i: (i // per, 0, 0))
    else:
        assert tm == M
        n_mem = mkvb.shape[1] // (M // rows_per_batch)
        mspec = _full(mkvb.shape)
    row = lambda wd: pl.BlockSpec((tm, wd), lambda i: (i, 0))
    return pl.pallas_call(
        functools.partial(_postmix_kernel, rows_per_batch=rows_per_batch, n_mem=n_mem),
        grid=(M // tm,),
        in_specs=[row(D), row(a.shape[1]), row(b.shape[1]), _full(wa.shape), _full(wb.shape), _full((1, D)),
                  _full(wq.shape), _full(qg.shape), _full(seg.shape), mspec, _full(wo.shape)],
        out_specs=row(D),
        out_shape=jax.ShapeDtypeStruct((M, D), F32),
        compiler_params=_params("parallel"),
        name="post_mix",
    )(x, a, b, wa, wb, gx.reshape(1, D), wq, qg, seg, mkvb, wo)


def _mlp_kernel(x_ref, g_ref, w1_ref, w2_ref, o_ref, xn_ref, acc_ref):
    f = pl.program_id(1)

    @pl.when(f == 0)
    def _():
        x = x_ref[...]
        xn_ref[...] = _rms(x, g_ref[...]).astype(BF)
        acc_ref[...] = x

    h = _dot(xn_ref[...], w1_ref[...].astype(BF))
    h = jnp.square(jnp.maximum(h, 0.0)).astype(BF)
    acc_ref[...] += _dot(h, w2_ref[...].astype(BF))

    @pl.when(f == pl.num_programs(1) - 1)
    def _():
        o_ref[...] = acc_ref[...]


def mlp_block(x, g, w1b, w2b, layer, tm, tf=512):
    M, D = x.shape
    FF = w1b.shape[2]
    return pl.pallas_call(
        _mlp_kernel,
        grid=(M // tm, FF // tf),
        in_specs=[pl.BlockSpec((tm, D), lambda i, f: (i, 0)),
                  pl.BlockSpec((1, D), lambda i, f: (0, 0)),
                  pl.BlockSpec((None, D, tf), lambda i, f: (layer, 0, f)),
                  pl.BlockSpec((None, tf, D), lambda i, f: (layer, f, 0))],
        out_specs=pl.BlockSpec((tm, D), lambda i, f: (i, 0)),
        out_shape=jax.ShapeDtypeStruct((M, D), F32),
        scratch_shapes=[pltpu.VMEM((tm, D), BF), pltpu.VMEM((tm, D), F32)],
        compiler_params=_params("parallel", "arbitrary"),
        name="mlp",
    )(x, g.reshape(1, D), w1b, w2b)


ML_CHUNK = 128
CONV_PAD = 8


def _dot3(a, b):
    hi = b.astype(BF)
    r1 = b - hi.astype(F32)
    mid = r1.astype(BF)
    lo = (r1 - mid.astype(F32)).astype(BF)
    return _dot(a, hi) + _dot(a, mid) + _dot(a, lo)


def _mlstm_kernel(zqk_ref, zv_ref, zo_ref, gi_ref, cw_ref, cb_ref, gb_ref, hg_ref, tril_ref,
                  c0_ref, n0_ref, m0_ref, cv0_ref,
                  h_out, c_out, n_out, m_out, cv_out,
                  xin_ref, c_ref, n_ref, m_ref, *, n_valid):
    L = ML_CHUNK
    ci = pl.program_id(1)

    @pl.when(ci == 0)
    def _():
        c_ref[...] = c0_ref[0]
        n_ref[...] = n0_ref[0]
        m_ref[...] = m0_ref[0]
        xin_ref[pl.ds(0, CONV_PAD), :] = cv0_ref[0]

    xin_ref[pl.ds(CONV_PAD, L), :] = zqk_ref[...]
    y = cb_ref[...]
    for j in range(CONV_W):
        y = y + cw_ref[j:j + 1, :] * xin_ref[pl.ds(CONV_PAD - (CONV_W - 1) + j, L), :]
    qk = y * (1.0 / (1.0 + jnp.exp(-y)))
    hist = xin_ref[pl.ds(n_valid, CONV_PAD), :]
    xin_ref[pl.ds(0, CONV_PAD), :] = hist
    cv_out[0] = hist

    row = lax.broadcasted_iota(jnp.int32, (L, L), 0)
    col = lax.broadcasted_iota(jnp.int32, (L, L), 1)
    valid = row < n_valid
    gl = gi_ref[...] + gb_ref[...]
    for h in range(ML_HEADS):
        hs = slice(h * ML_HD, (h + 1) * ML_HD)
        q = qk[:, hs]
        k = qk[:, ML_W + h * ML_HD:ML_W + (h + 1) * ML_HD] * (ML_HD ** -0.5)
        v = zv_ref[:, hs]
        gi_col = jnp.broadcast_to(gl[:, NSA_G_W + h:NSA_G_W + h + 1], (L, L))
        gf_col = jnp.broadcast_to(gl[:, NSA_G_W + ML_HEADS + h:NSA_G_W + ML_HEADS + h + 1], (L, L))
        li = jnp.where(valid, gi_col, NEG)
        lf = jnp.where(valid, jnp.minimum(gf_col, 0.0) - jnp.log(1.0 + jnp.exp(-jnp.abs(gf_col))), 0.0)
        b = _dot3(tril_ref[...], lf)
        m_prev = m_ref[h:h + 1, :]
        dm = jnp.where(col <= row, b - (b - li).T, NEG)
        inter = b + m_prev
        m_t = jnp.maximum(inter, jnp.max(dm, axis=1, keepdims=True))
        w = jnp.exp(dm - m_t)
        a = jnp.exp(inter - m_t)
        qb, kb, vb = q.astype(BF), k.astype(BF), v.astype(BF)
        wqk = w * _dot_nt(qb, kb)
        c_old = c_ref[h]
        n_old = n_ref[h:h + 1, :]
        num = a * _dot_nt(qb, c_old.astype(BF)) + _dot(wqk.astype(BF), vb)
        den = a * jnp.sum(q * n_old, axis=1, keepdims=True) + jnp.sum(wqk, axis=1, keepdims=True)
        hh = num / jnp.maximum(jnp.abs(den), jnp.exp(-m_t))
        m_new = m_t[L - 1:L, :]
        b_last = b[L - 1:L, :]
        wk = jnp.exp(b_last - b + li - m_new) * k
        decay = jnp.exp(b_last + m_prev - m_new)
        c_ref[h] = decay * c_old + _dot(v.T.astype(BF), wk.astype(BF))
        n_ref[h:h + 1, :] = decay * n_old + jnp.sum(wk, axis=0, keepdims=True)
        m_ref[h:h + 1, :] = m_new
        hn = hh * lax.rsqrt(jnp.mean(hh * hh, axis=1, keepdims=True) + EPS) * hg_ref[...]
        h_out[:, hs] = (hn * (1.0 / (1.0 + jnp.exp(-zo_ref[:, hs])))).astype(BF)

    @pl.when(ci == pl.num_programs(1) - 1)
    def _():
        c_out[0] = c_ref[...]
        n_out[0] = n_ref[...]
        m_out[0] = m_ref[...]


def mlstm(zqk, zv, zo, gi, conv_w, conv_b, gate_b, hn_g, state, nb, n_valid):
    M = zqk.shape[0]
    nch = M // nb // ML_CHUNK
    assert n_valid == ML_CHUNK or nch == 1
    C0, n0, m0, cv0 = state
    m0 = jnp.broadcast_to(m0[:, :, None], (nb, ML_HEADS, LANES))
    cv0 = jnp.pad(cv0, ((0, 0), (CONV_PAD - (CONV_W - 1), 0), (0, 0)))
    gb = jnp.zeros((1, LANES), F32).at[0, NSA_G_W:NSA_G_W + 2 * ML_HEADS].set(gate_b)
    tril = jnp.asarray(np.tril(np.ones((ML_CHUNK, ML_CHUNK))), BF)
    row = lambda wd: pl.BlockSpec((ML_CHUNK, wd), lambda b, c: (b * nch + c, 0))
    st = lambda *shape: pl.BlockSpec((1,) + shape, lambda b, c: (b,) + (0,) * len(shape))
    h, Cn, nn, mn, cvn = pl.pallas_call(
        functools.partial(_mlstm_kernel, n_valid=n_valid),
        grid=(nb, nch),
        in_specs=[row(2 * ML_W), row(ML_W), row(ML_W), row(LANES),
                  _full(conv_w.shape), _full((1, 2 * ML_W)), _full((1, LANES)), _full((1, ML_HD)), _full(tril.shape),
                  st(ML_HEADS, ML_HD, ML_HD), st(ML_HEADS, ML_HD), st(ML_HEADS, LANES), st(CONV_PAD, 2 * ML_W)],
        out_specs=[row(ML_W), st(ML_HEADS, ML_HD, ML_HD), st(ML_HEADS, ML_HD), st(ML_HEADS, LANES),
                   st(CONV_PAD, 2 * ML_W)],
        out_shape=[jax.ShapeDtypeStruct((M, ML_W), BF),
                   jax.ShapeDtypeStruct((nb, ML_HEADS, ML_HD, ML_HD), F32),
                   jax.ShapeDtypeStruct((nb, ML_HEADS, ML_HD), F32),
                   jax.ShapeDtypeStruct((nb, ML_HEADS, LANES), F32),
                   jax.ShapeDtypeStruct((nb, CONV_PAD, 2 * ML_W), F32)],
        scratch_shapes=[pltpu.VMEM((CONV_PAD + ML_CHUNK, 2 * ML_W), F32),
                        pltpu.VMEM((ML_HEADS, ML_HD, ML_HD), F32),
                        pltpu.VMEM((ML_HEADS, ML_HD), F32),
                        pltpu.VMEM((ML_HEADS, LANES), F32)],
        compiler_params=_params("parallel", "arbitrary"),
        name="mlstm",
    )(zqk, zv, zo, gi, conv_w, conv_b.reshape(1, -1), gb, hn_g.reshape(1, -1), tril, C0, n0, m0, cv0)
    return h, (Cn, nn, mn[:, :, 0], cvn[:, CONV_PAD - (CONV_W - 1):])


def _rope_tables(pos, rows):
    half = ROPE_DIMS // 2
    freqs = ROPE_THETA ** (-jnp.arange(half, dtype=F32) / half)
    ang = pos.astype(F32)[:, None] * freqs
    cos, sin = jnp.cos(ang), jnp.sin(ang)
    n = pos.shape[0]
    one, zero = jnp.ones((n, NSA_HD - ROPE_DIMS), F32), jnp.zeros((n, NSA_HD - ROPE_DIMS), F32)
    zh = jnp.zeros((n, half), F32)
    c = jnp.concatenate([cos, cos, one], axis=1)
    sa = jnp.concatenate([-sin, zh, zero], axis=1)
    sb = jnp.concatenate([zh, sin, zero], axis=1)
    tile = lambda t: jnp.tile(t, (rows // n, LANES // NSA_HD))
    return tile(c), tile(sa), tile(sb)


def _seg_matrix():
    i = np.arange(LANES)
    return jnp.asarray((i[:, None] // NSA_HD == i[None, :] // NSA_HD) / NSA_HD, BF)


def _cover_matrix(nc_rows, lanes):
    ci = np.arange(nc_rows)[:, None] * CMP_STRIDE
    sj = np.arange(lanes)[None, :]
    return jnp.asarray((ci < (sj + 1) * SEL_LEN) & (ci + CMP_LEN > sj * SEL_LEN), BF)


def _expand_matrix(n_chunks):
    j = np.arange(LANES)[None, :, None]
    c = np.arange(n_chunks)[:, None, None]
    s = np.arange(KCH)[None, None, :]
    return jnp.asarray(j == c * (KCH // SEL_LEN) + s // SEL_LEN, BF)


def _in_weights(w):
    zq, zkv, zg, zqk, zv, zo, zif = jnp.split(w, np.cumsum(COL_SPLITS)[:-1].tolist(), axis=1)
    pad = jnp.zeros((w.shape[0], LANES - NSA_G_W - 2 * ML_HEADS), w.dtype)
    return tuple(a.astype(BF) for a in (zq, zkv, zqk, zv, zo, jnp.concatenate([zg, zif, pad], axis=1)))


def _compress_weights(cmp_pos, cmp_w1, cmp_w2):
    G, D, S = NSA_KV_HEADS, NSA_HD, CMP_STRIDE
    pos = jnp.tile(cmp_pos[:, :, None, :], (1, 1, G, 1))
    pos = pos.transpose(1, 0, 2, 3).reshape(CMP_LEN, 2 * G * D)
    w1 = cmp_w1.reshape(2, CMP_LEN, D, CMP_HIDDEN)
    z = jnp.zeros_like(w1)
    w1g = jnp.stack([jnp.concatenate([w1, z], axis=2), jnp.concatenate([z, w1], axis=2)], axis=1)
    return (pos[:S], pos[S:], w1g[:, :, :S].astype(BF), w1g[:, :, S:].astype(BF), cmp_w2.astype(BF))


def kernel(x_prompt, x_sample, mem_prompt, cache_nsa_kv, cache_win_kv, cache_mem_kv, state_mlstm_C, state_mlstm_n, state_mlstm_m, state_conv, page_table, norm_mix_g, w_in, nsa_qn_g, nsa_kn_g, cmp_pos, cmp_w1, cmp_w2, ml_conv_w, ml_conv_b, ml_gate_b, ml_hn_g, w_out, norm_xa_g, norm_mem_g, xa_wq, xa_wkv, xa_qn_g, xa_kn_g, xa_wo, norm_mlp_g, mlp_w1, mlp_w2):
    B, T, D = x_prompt.shape
    DB, DS, _ = x_sample.shape
    depth = w_in.shape[0]
    n_mem = mem_prompt.shape[1]
    P = page_table.shape[1] * cache_nsa_kv.shape[1]
    wkeep = min(WINDOW, T)
    G, HD = NSA_KV_HEADS, NSA_HD
    MP, MS = B * T, DB * DS
    tm_p, tm_s = 512, MS

    pos_p = jnp.arange(T, dtype=jnp.int32)
    pos_s = P + jnp.arange(DS, dtype=jnp.int32)
    tabs_p = _rope_tables(pos_p, T)
    tabs_s = _rope_tables(pos_s, MS)
    seg = _seg_matrix()
    cover = _cover_matrix(T // CMP_STRIDE, LANES).T
    expand = _expand_matrix(T // KCH)
    tile2 = lambda v: jnp.tile(v.reshape(-1, HD), (1, LANES // HD))
    pad_chunk = lambda a: jnp.pad(a.reshape(DB, DS, -1), ((0, 0), (0, ML_CHUNK - DS), (0, 0))).reshape(DB * ML_CHUNK, -1)

    n_pool, page = cache_nsa_kv.shape[:2]
    cache_t = cache_nsa_kv.transpose(0, 2, 3, 4, 5, 1).reshape(n_pool, depth, 4 * G * HD, page)
    win_t = cache_win_kv.transpose(0, 1, 3, 4, 5, 2).reshape(depth, DB, 2 * G * HD, cache_win_kv.shape[2])

    xp = x_prompt.reshape(MP, D)
    xs = x_sample.reshape(MS, D)
    mem = mem_prompt.reshape(B * n_mem, D)
    nsa_p, nsa_s, win_p, win_s = [], [], [], []
    C_p, C_s, n_p, n_s, m_p, m_s = [], [], [], [], [], []
    cv_p, cv_s, mem_p = [], [], []
    for l in range(depth):
        w_l = _in_weights(w_in[l])
        qg, kg = tile2(nsa_qn_g[l]), tile2(nsa_kn_g[l])
        cw = _compress_weights(cmp_pos[l], cmp_w1[l], cmp_w2[l])
        wa, wb = w_out[l, :NSA_Q_W].astype(BF), w_out[l, NSA_Q_W:].astype(BF)
        xq, xo = xa_wq[l].astype(BF), xa_wo[l].astype(BF)
        xqg = tile2(xa_qn_g[l])
        w1b, w2b = mlp_w1, mlp_w2

        q, kvc, kvw, kb, va, zqk, zv, zo, gi = in_proj(xp, norm_mix_g[l], w_l, tabs_p, qg, kg, seg, tm_p)
        kcv = compress_prompt(kvc, cw, B, T)
        a_p = nsa_prompt(q, gi, kb, va, kcv, cover, expand, B, T)
        nsa_p.append(kvc.reshape(B, T, 4, G, HD))
        win_p.append(kvw.reshape(B, T, 2, G, HD)[:, T - wkeep:])
        st0 = (jnp.zeros((B, ML_HEADS, ML_HD, ML_HD), F32), jnp.zeros((B, ML_HEADS, ML_HD), F32),
               jnp.zeros((B, ML_HEADS), F32), jnp.zeros((B, CONV_W - 1, 2 * ML_W), F32))
        b_p, (Cn, nn, mn, buf_p) = mlstm(zqk, zv, zo, gi, ml_conv_w[l], ml_conv_b[l], ml_gate_b[l], ml_hn_g[l],
                                         st0, B, ML_CHUNK)
        C_p.append(Cn); n_p.append(nn); m_p.append(mn); cv_p.append(buf_p)
        mkv, mkvb = memory_kv(mem, norm_mem_g[l], xa_wkv[l].astype(BF), tile2(xa_kn_g[l]), seg)
        mem_p.append(mkv.reshape(B, n_mem, 2, XA_HEADS, XA_HD))
        xp = post_mix(xp, a_p, b_p, wa, wb, norm_xa_g[l], xq, xqg, seg, mkvb.reshape(B, n_mem, -1), xo, tm_p, T)
        xp = mlp_block(xp, norm_mlp_g[l], w1b, w2b, l, tm=1024)

        q, kvc, kvw, _, _, zqk, zv, zo, gi = in_proj(xs, norm_mix_g[l], w_l, tabs_s, qg, kg, seg, tm_s)
        kv = jnp.concatenate([kvc, kvw], axis=1)
        kvp = kv.reshape(DB, DS, 3, 2, G, HD)
        a_s = nsa_sample(page_table, cache_t, l, q, gi, kv, win_t[l],
                         _sample_compress_weights(cmp_pos[l], cmp_w1[l], cmp_w2[l]), DB, DS)
        nsa_s.append(kvp[:, :, :2].reshape(DB, DS, 4, G, HD))
        win_s.append(kvp[:, :, 2])
        st = (state_mlstm_C[l], state_mlstm_n[l], state_mlstm_m[l], state_conv[l])
        b_s, (Cn, nn, mn, buf_s) = mlstm(*(pad_chunk(a) for a in (zqk, zv, zo, gi)), ml_conv_w[l], ml_conv_b[l],
                                         ml_gate_b[l], ml_hn_g[l], st, DB, DS)
        b_s = b_s.reshape(DB, ML_CHUNK, ML_W)[:, :DS].reshape(MS, ML_W)
        C_s.append(Cn); n_s.append(nn); m_s.append(mn); cv_s.append(buf_s)
        mkvb_s = cache_mem_kv[l].transpose(2, 3, 4, 0, 1).reshape(2 * XA_HEADS * XA_HD, DB * n_mem).astype(BF)
        xs = post_mix(xs, a_s, b_s, wa, wb, norm_xa_g[l], xq, xqg, seg, mkvb_s, xo, tm_s, DS)
        xs = mlp_block(xs, norm_mlp_g[l], w1b, w2b, l, tm=MS)
    return (xp.reshape(B, T, D), xs.reshape(DB, DS, D),
            jnp.stack(nsa_p, axis=2), jnp.stack(nsa_s, axis=2),
            jnp.stack(win_p),
            jnp.concatenate([cache_win_kv, jnp.stack(win_s)], axis=2)[:, :, -cache_win_kv.shape[2]:],
            jnp.stack(C_p), jnp.stack(C_s),
            jnp.stack(n_p), jnp.stack(n_s),
            jnp.stack(m_p), jnp.stack(m_s),
            jnp.stack(cv_p), jnp.stack(cv_s),
            jnp.stack(mem_p))
```

```python
import functools

import jax
import jax.numpy as jnp
import numpy as np
from jax import lax
from jax.experimental import pallas as pl
from jax.experimental.pallas import tpu as pltpu

F32 = jnp.float32
BF = jnp.bfloat16

EPS = 1e-6
NSA_HEADS, NSA_KV_HEADS, NSA_HD = 8, 2, 64
NSA_GROUP = NSA_HEADS // NSA_KV_HEADS
CMP_LEN, CMP_STRIDE, CMP_HIDDEN = 32, 16, 128
SEL_LEN, N_SEL, WINDOW = 64, 8, 512
ROPE_DIMS, ROPE_THETA = NSA_HD // 4, 500000.0
ML_HEADS, ML_HD, CONV_W = 4, 128, 4
ML_W = ML_HEADS * ML_HD
XA_HEADS, XA_HD = 4, 64
LANES = 128
QB = 256
VMEM_LIMIT = 56 * 1024 * 1024

NSA_Q_W = NSA_HEADS * NSA_HD
NSA_KV_W = 6 * NSA_KV_HEADS * NSA_HD
NSA_G_W = 3 * NSA_HEADS
COL_SPLITS = (NSA_Q_W, NSA_KV_W, NSA_G_W, 2 * ML_W, ML_W, ML_W, 2 * ML_HEADS)
NEG = -1e30


def _dot(a, b):
    return jnp.dot(a, b, preferred_element_type=F32)


def _dot_nt(a, b):
    return lax.dot_general(a, b, (((1,), (1,)), ((), ())), preferred_element_type=F32)


def _dot_hl(a, b):
    hi = a.astype(BF)
    lo = (a - hi.astype(F32)).astype(BF)
    return _dot(hi, b) + _dot(lo, b)


def _rms(x, g):
    return x * lax.rsqrt(jnp.mean(x * x, axis=-1, keepdims=True) + EPS) * g


def _head_norm(z, g, seg):
    return z * lax.rsqrt(_dot_hl(z * z, seg) + EPS) * g


def _rope(y, c, sa, sb):
    half = ROPE_DIMS // 2
    return y * c + pltpu.roll(y, LANES - half, 1) * sa + pltpu.roll(y, half, 1) * sb


def _params(*sem):
    return pltpu.CompilerParams(dimension_semantics=sem, vmem_limit_bytes=VMEM_LIMIT)


def _full(shape):
    n = len(shape)
    return pl.BlockSpec(shape, lambda *_: (0,) * n)


def _inproj_kernel(x_ref, g_ref, wq_ref, wkv_ref, wqk_ref, wv_ref, wo_ref, wgi_ref,
                   rc_ref, rsa_ref, rsb_ref, qg_ref, kg_ref, seg_ref,
                   q_out, kvc_out, kvw_out, kb_out, va_out, zqk_out, zv_out, zo_out, gi_out):
    D = NSA_HD
    xn = _rms(x_ref[...], g_ref[...]).astype(BF)
    c, sa, sb = rc_ref[...], rsa_ref[...], rsb_ref[...]
    seg = seg_ref[...]
    lane = lax.broadcasted_iota(jnp.int32, c.shape, 1)
    low = lane < D
    swap = lambda z: pltpu.roll(z, D, 1)
    zq = _dot(xn, wq_ref[...])
    for j in range(NSA_Q_W // LANES):
        y = _rope(_head_norm(zq[:, j * LANES:(j + 1) * LANES], qg_ref[...], seg), c, sa, sb) * D ** -0.5
        ys = swap(y)
        for i in range(2):
            h = 2 * j + i
            want_low = h // NSA_GROUP == 0
            src = y if want_low == (i == 0) else ys
            q_out[:, h * LANES:(h + 1) * LANES] = jnp.where(low if want_low else ~low, src, 0.0).astype(BF)
    zkv = _dot(xn, wkv_ref[...])
    for j in range(NSA_KV_W // LANES):
        sl = slice(j * LANES, (j + 1) * LANES)
        z = zkv[:, sl]
        br = j // 2
        if j % 2 == 0:
            z = _rope(_head_norm(z, kg_ref[br:br + 1, :], seg), c, sa, sb)
            if br > 0:
                kb_out[:, (br - 1) * LANES:br * LANES] = z.astype(BF)
        elif br > 0:
            one = jnp.where(lane == D, 1.0, 0.0)
            for g, src in enumerate((z, swap(z))):
                col = ((br - 1) * NSA_KV_HEADS + g) * LANES
                va_out[:, col:col + LANES] = jnp.where(low, src, one).astype(BF)
        if br < 2:
            kvc_out[:, sl] = z
        else:
            kvw_out[:, (j - 4) * LANES:(j - 3) * LANES] = z
    zqk_out[...] = _dot(xn, wqk_ref[...])
    zv_out[...] = _dot(xn, wv_ref[...])
    zo_out[...] = _dot(xn, wo_ref[...])
    gi_out[...] = _dot(xn, wgi_ref[...])


def in_proj(x, g, w, rope_tabs, qg, kg, seg, tm):
    M, D = x.shape
    rc, rsa, rsb = rope_tabs
    nt = rc.shape[0] // tm
    widths = (NSA_HEADS * LANES, 4 * LANES, 2 * LANES, 2 * LANES, 4 * LANES, 2 * ML_W, ML_W, ML_W, LANES)
    dtypes = (BF, F32, F32, BF, BF, F32, F32, F32, F32)
    row = lambda wd: pl.BlockSpec((tm, wd), lambda i: (i, 0))
    tab = pl.BlockSpec((tm, LANES), lambda i: (i % nt, 0))
    return pl.pallas_call(
        _inproj_kernel,
        grid=(M // tm,),
        in_specs=[row(D), _full((1, D))] + [_full(a.shape) for a in w] + [tab, tab, tab,
                  _full(qg.shape), _full(kg.shape), _full(seg.shape)],
        out_specs=[row(wd) for wd in widths],
        out_shape=[jax.ShapeDtypeStruct((M, wd), dt) for wd, dt in zip(widths, dtypes)],
        compiler_params=_params("parallel"),
        name="in_proj",
    )(x, g.reshape(1, D), *w, rc, rsa, rsb, qg, kg, seg)


def _compress_body(xk_ref, xv_ref, pa_ref, pb_ref, w1a_ref, w1b_ref, w2_ref, sh_ref, nsub):
    G = NSA_KV_HEADS
    acc_a = [jnp.zeros((nsub, CMP_HIDDEN), F32) for _ in range(2 * G)]
    acc_b = [jnp.zeros((nsub, CMP_HIDDEN), F32) for _ in range(2 * G)]
    for u in range(CMP_STRIDE):
        for kind, x_ref in enumerate((xk_ref, xv_ref)):
            sl = slice(kind * LANES, (kind + 1) * LANES)
            xu = x_ref[pl.ds(u, nsub, stride=CMP_STRIDE), :]
            xa = (xu + pa_ref[u:u + 1, sl]).astype(BF)
            xb = (xu + pb_ref[u:u + 1, sl]).astype(BF)
            for g in range(G):
                acc_a[kind * G + g] += _dot(xa, w1a_ref[kind, g, u])
                acc_b[kind * G + g] += _dot(xb, w1b_ref[kind, g, u])
    sh_ref[pl.ds(nsub, 8), :] = jnp.zeros((8, CMP_HIDDEN), F32)
    outs = []
    for kind in range(2):
        for g in range(G):
            sh_ref[pl.ds(0, nsub), :] = acc_b[kind * G + g]
            h = acc_a[kind * G + g] + sh_ref[pl.ds(1, nsub), :]
            h = h * (1.0 / (1.0 + jnp.exp(-h)))
            outs.append(_dot(h.astype(BF), w2_ref[kind]))
    return outs


def _compress_kernel(xk_ref, xv_ref, pa_ref, pb_ref, w1a_ref, w1b_ref, w2_ref, o_ref, sh_ref, *, nsub):
    kc0, kc1, vc0, vc1 = _compress_body(xk_ref, xv_ref, pa_ref, pb_ref, w1a_ref, w1b_ref, w2_ref, sh_ref, nsub)
    zero = jnp.zeros_like(vc0)
    for j, o in enumerate((kc0, kc1, vc0, zero, vc1, zero)):
        o_ref[0, :, j * NSA_HD:(j + 1) * NSA_HD] = o


def compress_prompt(kv, cw, B, T):
    pa, pb, w1a, w1b, w2 = cw
    nsub = T // CMP_STRIDE
    return pl.pallas_call(
        functools.partial(_compress_kernel, nsub=nsub),
        grid=(B,),
        in_specs=[pl.BlockSpec((T, LANES), lambda b: (b, 0)), pl.BlockSpec((T, LANES), lambda b: (b, 1)),
                  _full(pa.shape), _full(pb.shape), _full(w1a.shape), _full(w1b.shape), _full(w2.shape)],
        out_specs=pl.BlockSpec((1, nsub, 3 * LANES), lambda b: (b, 0, 0)),
        out_shape=jax.ShapeDtypeStruct((B, nsub, 3 * LANES), F32),
        scratch_shapes=[pltpu.VMEM((nsub + 8, CMP_HIDDEN), F32)],
        compiler_params=_params("parallel"),
        name="compress_prompt",
    )(kv, kv, pa, pb, w1a, w1b, w2)


def _select_topk(imp, cur, n_blocks, axis=1):
    j = lax.broadcasted_iota(jnp.int32, imp.shape, axis)
    jf = j.astype(F32)
    forced = (j == cur) | (j == 0)
    dead = (j > cur) | (j >= n_blocks)
    val = jnp.where(forced, jnp.inf, jnp.where(dead, -jnp.inf, imp))
    sel = jnp.zeros(imp.shape, F32)
    for _ in range(N_SEL):
        m = jnp.max(val, axis=axis, keepdims=True)
        idx = jnp.min(jnp.where(val == m, jf, float(imp.shape[axis])), axis=axis, keepdims=True)
        hit = jf == idx
        sel = jnp.where(hit & (m > -jnp.inf), 1.0, sel)
        val = jnp.where(hit, -jnp.inf, val)
    return sel


KCH = 2 * LANES


def _nsa_prompt_kernel(q_ref, gi_ref, kcv_ref, kb_ref, va_ref, cover_ref, exp_ref, o_ref,
                       s_ref, mx_ref, acc_ref, *, n_sel_blocks):
    G, HG, D, H = NSA_KV_HEADS, NSA_GROUP, NSA_HD, NSA_HEADS
    qb = pl.program_id(1)
    q0 = qb * QB
    qpos = q0 + lax.broadcasted_iota(jnp.int32, (QB, LANES), 0)
    lane = lax.broadcasted_iota(jnp.int32, (QB, LANES), 1)
    gates = 1.0 / (1.0 + jnp.exp(-gi_ref[...]))
    nc = kcv_ref.shape[1]
    qp = jnp.concatenate([q_ref[:, h * LANES:(h + 1) * LANES] for h in range(H)], axis=0)

    s_all = _dot_nt(qp, kcv_ref[0, :, 0:LANES].astype(BF))
    cmask = (lane[:, :nc] * CMP_STRIDE + (CMP_LEN - 1)) <= qpos[:, :nc]
    o_c, psums = [], []
    for g in range(G):
        vc = kcv_ref[0, :, (1 + g) * LANES:(2 + g) * LANES].astype(BF)
        psum = jnp.zeros((QB, nc), F32)
        for h in range(HG):
            hh = g * HG + h
            s = jnp.where(cmask, s_all[hh * QB:(hh + 1) * QB], NEG)
            e = jnp.where(cmask, jnp.exp(s - jnp.max(s, axis=1, keepdims=True)), 0.0)
            den = jnp.sum(e, axis=1, keepdims=True)
            p = e / jnp.where(den > 0, den, 1.0)
            psum = psum + p
            o_c.append(_dot(p.astype(BF), vc))
        psums.append(psum)
    psum = jnp.concatenate(psums, axis=0)
    hi = psum.astype(BF)
    lo = (psum - hi.astype(F32)).astype(BF)
    imp_t = _dot_nt(cover_ref[...], hi) + _dot_nt(cover_ref[...], lo)
    nb8 = -(-n_sel_blocks // 8) * 8
    cur_t = (q0 + lax.broadcasted_iota(jnp.int32, (nb8, G * QB), 1) % QB) // SEL_LEN
    sel_t = _select_topk(imp_t[:nb8], cur_t, n_sel_blocks, axis=0)
    sel_t = jnp.concatenate([sel_t, jnp.zeros((LANES - nb8, G * QB), F32)], axis=0)
    sel_all = sel_t.T.astype(BF)
    sels = [sel_all[g * QB:(g + 1) * QB] for g in range(G)]

    qpos2 = q0 + lax.broadcasted_iota(jnp.int32, (QB, KCH), 0)
    lane2 = lax.broadcasted_iota(jnp.int32, (QB, KCH), 1)
    branch_out = []
    for br in range(2):
        mx_ref[...] = jnp.full(mx_ref.shape, NEG, F32)
        acc_ref[...] = jnp.zeros(acc_ref.shape, F32)

        def scores(c, carry, br=br):
            k0 = pl.multiple_of(c * KCH, KCH)
            s_all = _dot_nt(qp, kb_ref[pl.ds(k0, KCH), br * LANES:(br + 1) * LANES])
            d = qpos2 - (k0 + lane2)
            if br == 0:
                masks = [(_dot(sels[g], exp_ref[c]) > 0.5) & (d >= 0) for g in range(G)]
            else:
                masks = [(d >= 0) & (d <= WINDOW)] * G
            for hh in range(H):
                rows = pl.ds(hh * QB, QB)
                s = jnp.where(masks[hh // HG], s_all[hh * QB:(hh + 1) * QB], NEG)
                s_ref[c, rows, :] = s
                mx_ref[rows, :] = functools.reduce(
                    jnp.maximum, [mx_ref[rows, :]] + [s[:, i * LANES:(i + 1) * LANES] for i in range(KCH // LANES)])
            return carry

        def values(c, carry, br=br):
            k0 = pl.multiple_of(c * KCH, KCH)
            for g in range(G):
                rows = pl.ds(g * HG * QB, HG * QB)
                mx = mx_ref[rows, :]
                p = jnp.concatenate([jnp.exp(s_ref[c, rows, pl.ds(i * LANES, LANES)] - mx)
                                     for i in range(KCH // LANES)], axis=1).astype(BF)
                acc_ref[rows, :] += _dot(p, va_ref[pl.ds(k0, KCH), (br * G + g) * LANES:(br * G + g + 1) * LANES])
            return carry

        lo = 0 if br == 0 else jnp.maximum(q0 - WINDOW, 0) // KCH
        hi = (q0 + QB - 1) // KCH + 1
        lax.fori_loop(lo, hi, scores, 0)
        mx_ref[...] = jnp.broadcast_to(jnp.max(mx_ref[...], axis=1, keepdims=True), mx_ref.shape)
        lax.fori_loop(lo, hi, values, 0)
        r = acc_ref[...]
        branch_out.append(r / r[:, D:D + 1])

    for j in range(H // 2):
        tiles = []
        for hh in (2 * j, 2 * j + 1):
            rows = slice(hh * QB, (hh + 1) * QB)
            tiles.append(gates[:, 3 * hh:3 * hh + 1] * o_c[hh]
                         + gates[:, 3 * hh + 1:3 * hh + 2] * branch_out[0][rows]
                         + gates[:, 3 * hh + 2:3 * hh + 3] * branch_out[1][rows])
        o_ref[:, j * LANES:(j + 1) * LANES] = jnp.where(lane < D, tiles[0], pltpu.roll(tiles[1], D, 1)).astype(BF)


def nsa_prompt(q, gi, kb, va, kcv, cover, expand, B, T):
    nqb = T // QB
    nc = kcv.shape[1]
    rows = NSA_HEADS * QB
    return pl.pallas_call(
        functools.partial(_nsa_prompt_kernel, n_sel_blocks=T // SEL_LEN),
        grid=(B, nqb),
        in_specs=[pl.BlockSpec((QB, q.shape[1]), lambda b, i: (b * nqb + i, 0)),
                  pl.BlockSpec((QB, LANES), lambda b, i: (b * nqb + i, 0)),
                  pl.BlockSpec((1, nc, kcv.shape[2]), lambda b, i: (b, 0, 0)),
                  pl.BlockSpec((T, kb.shape[1]), lambda b, i: (b, 0)),
                  pl.BlockSpec((T, va.shape[1]), lambda b, i: (b, 0)),
                  _full(cover.shape), _full(expand.shape)],
        out_specs=pl.BlockSpec((QB, NSA_Q_W), lambda b, i: (b * nqb + i, 0)),
        out_shape=jax.ShapeDtypeStruct((B * T, NSA_Q_W), BF),
        scratch_shapes=[pltpu.VMEM((T // KCH, rows, KCH), F32), pltpu.VMEM((rows, LANES), F32),
                        pltpu.VMEM((rows, LANES), F32)],
        compiler_params=_params("parallel", "arbitrary"),
        name="nsa_prompt",
    )(q, gi, kcv, kb, va, cover, expand)


TPAD = 8


def _masked_softmax_parts(parts):
    m = functools.reduce(jnp.maximum, [jnp.max(jnp.where(k, s, NEG), axis=1, keepdims=True) for s, k in parts])
    es = [jnp.where(k, jnp.exp(jnp.where(k, s, NEG) - m), 0.0) for s, k in parts]
    den = functools.reduce(jnp.add, [jnp.sum(e, axis=1, keepdims=True) for e in es])
    return es, jnp.where(den > 0, den, 1.0)


def _nsa_sample_kernel(pt_ref, ct_ref, q_ref, gt_ref, new_ref, wt_ref, perm_ref, pos_ref, wk_ref, wv_ref, w2_ref,
                       cover_ref, exp_ref, o_ref, cbuf, sbuf, xk_ref, xv_ref, sh_ref, sem,
                       *, layer, n_pages, page, past_len, n_new):
    G, HG, D = NSA_KV_HEADS, NSA_GROUP, NSA_HD
    b = pl.program_id(0)
    nb = pl.num_programs(0)
    half_rows = 2 * G * D
    bufs = (cbuf, sbuf)
    sub_per_page = page // CMP_STRIDE

    def page_copy(bb, p, half):
        return pltpu.make_async_copy(
            ct_ref.at[pt_ref[bb, p], layer, pl.ds(half * half_rows, half_rows), :],
            bufs[half].at[:, pl.ds(pl.multiple_of(p * page, page), page)], sem.at[half])

    def start_all(bb, half):
        lax.fori_loop(0, n_pages, lambda p, c: (page_copy(bb, p, half).start(), c)[1], 0)

    def wait_all(bb, half):
        lax.fori_loop(0, n_pages, lambda p, c: (page_copy(bb, p, half).wait(), c)[1], 0)

    @pl.when(b == 0)
    def _():
        start_all(0, 0)
        start_all(0, 1)

    wait_all(b, 0)

    def to_rows(p, c):
        p0 = pl.multiple_of(p * page, page)
        z = _dot_nt(perm_ref[...], cbuf[:, pl.ds(p0, page)].astype(BF))
        r0 = pl.multiple_of(p * sub_per_page, sub_per_page)
        for u in range(CMP_STRIDE):
            zu = z[u * sub_per_page:(u + 1) * sub_per_page]
            xk_ref[pl.ds(r0, sub_per_page), u * LANES:(u + 1) * LANES] = zu[:, 0:G * D]
            xv_ref[pl.ds(r0, sub_per_page), u * LANES:(u + 1) * LANES] = zu[:, G * D:2 * G * D]
        return c

    lax.fori_loop(0, n_pages, to_rows, 0, unroll=4)

    @pl.when(b + 1 < nb)
    def _():
        start_all(b + 1, 0)

    nsub = past_len // CMP_STRIDE
    half_w = G * CMP_HIDDEN
    sh_ref[pl.ds(nsub, 8), :] = jnp.zeros((8, half_w), F32)
    kcv = []
    for kind, (x_ref, w_ref) in enumerate(((xk_ref, wk_ref), (xv_ref, wv_ref))):
        bias = _dot(pos_ref[kind], w_ref[...])
        h_all = _dot(x_ref[...].astype(BF), w_ref[...])
        sh_ref[pl.ds(0, nsub), :] = h_all[:, half_w:] + bias[1:2, half_w:]
        h = h_all[:, :half_w] + bias[0:1, :half_w] + sh_ref[pl.ds(1, nsub), :]
        h = h * (1.0 / (1.0 + jnp.exp(-h)))
        kcv.append(_dot(h.astype(BF), w2_ref[kind]).astype(BF))
    kc, vc = kcv

    rows = G * HG * TPAD
    q = q_ref[0]
    t_rows = lax.broadcasted_iota(jnp.int32, (rows, LANES), 0) % TPAD
    lane_r = lax.broadcasted_iota(jnp.int32, (rows, LANES), 1)
    new_ok = (lane_r <= t_rows) & (lane_r < n_new)
    tile_gh = lambda xs: jnp.concatenate([x for x in xs for _ in range(HG)], axis=0)

    s = _dot_nt(q, kc)
    c_end = lax.broadcasted_iota(jnp.int32, s.shape, 1) * CMP_STRIDE + (CMP_LEN - 1)
    q_pos = past_len + lax.broadcasted_iota(jnp.int32, s.shape, 0) % TPAD
    (e,), den = _masked_softmax_parts([(s, c_end <= q_pos)])
    p = e / den
    o_c = _dot(p.astype(BF), vc)
    psum = jnp.concatenate([functools.reduce(jnp.add, [p[(g * HG + h) * TPAD:(g * HG + h + 1) * TPAD]
                                                       for h in range(HG)]) for g in range(G)], axis=0)
    imp = _dot_hl(psum, cover_ref[...])
    cur = (past_len + lax.broadcasted_iota(jnp.int32, imp.shape, 0) % TPAD) // SEL_LEN
    sel = _select_topk(imp, cur, -(-(past_len + n_new) // SEL_LEN))

    wait_all(b, 1)
    n_past_blocks = past_len // SEL_LEN
    selx = _dot(sel[:, :n_past_blocks].astype(BF), exp_ref[...])
    in_new = jnp.broadcast_to(sel[:, n_past_blocks:n_past_blocks + 1], (G * TPAD, LANES))
    split_g = lambda x: [x[g * TPAD:(g + 1) * TPAD] for g in range(G)]
    s_past = _dot(q, sbuf[0:G * D, :].astype(BF))
    s_new = _dot_nt(q, new_ref[0, 0])
    (e_past, e_new), den = _masked_softmax_parts(
        [(s_past, tile_gh(split_g(selx)) > 0.5), (s_new, (tile_gh(split_g(in_new)) > 0.5) & new_ok)])
    o_t = _dot_nt(sbuf[G * D:2 * G * D, :].astype(BF), e_past.astype(BF))
    o_s = (o_t.T + _dot(e_new.astype(BF), new_ref[0, 1])) / den

    @pl.when(b + 1 < nb)
    def _():
        start_all(b + 1, 1)

    wb = wt_ref.shape[2]
    s_buf = _dot(q, wt_ref[0, 0:G * D, :].astype(BF))
    i = lax.broadcasted_iota(jnp.int32, s_buf.shape, 1)
    t = lax.broadcasted_iota(jnp.int32, s_buf.shape, 0) % TPAD
    s_new = _dot_nt(q, new_ref[0, 2])
    (e_buf, e_new), den = _masked_softmax_parts([(s_buf, wb + t - i <= WINDOW), (s_new, new_ok)])
    o_t = _dot_nt(wt_ref[0, G * D:2 * G * D, :].astype(BF), e_buf.astype(BF))
    o_w = (o_t.T + _dot(e_new.astype(BF), new_ref[0, 3])) / den
    gates = 1.0 / (1.0 + jnp.exp(-gt_ref[0]))
    o_ref[0] = gates[:, 0:1] * o_c + gates[:, 1:2] * o_s + gates[:, 2:3] * o_w


def _sample_compress_weights(cmp_pos, cmp_w1, cmp_w2):
    G, D, S, Hd = NSA_KV_HEADS, NSA_HD, CMP_STRIDE, CMP_HIDDEN
    w1 = cmp_w1.reshape(2, 2, S, D, Hd)
    eye = jnp.eye(G, dtype=w1.dtype)
    wk = jnp.einsum('khudn,gj->kugdhjn', w1, eye).reshape(2, S * G * D, 2 * G * Hd).astype(BF)
    pos = cmp_pos.reshape(2, 2, S, 1, D)
    pos = jnp.broadcast_to(pos, (2, 2, S, G, D)).reshape(2, 2, S * G * D)
    pos = jnp.pad(pos, ((0, 0), (0, 8 - 2), (0, 0))).astype(BF)
    w2 = jnp.einsum('knd,gj->kgnjd', cmp_w2, eye).reshape(2, G * Hd, G * D).astype(BF)
    return wk[0], wk[1], pos, w2


def nsa_sample(page_table, cache_t, layer, q, gi, kv, win_t, cw, DB, DS):
    G, HG, D, H = NSA_KV_HEADS, NSA_GROUP, NSA_HD, NSA_HEADS
    n_pages, page = page_table.shape[1], cache_t.shape[3]
    past_len = n_pages * page
    rows = H * TPAD
    wk, wv, pos, w2 = cw
    heads = lambda a, w: a.reshape(DB, DS, H, w).transpose(0, 2, 1, 3)
    padt = lambda a: jnp.pad(a, ((0, 0), (0, 0), (0, TPAD - DS), (0, 0)))
    qs = padt(heads(q, LANES)).reshape(DB, rows, LANES)
    gts = padt(heads(gi[:, :NSA_G_W], 3)).reshape(DB, rows, 3)
    gts = jnp.pad(gts, ((0, 0), (0, 0), (0, LANES - 3)))
    new = kv[:, 2 * G * D:].reshape(DB, DS, 4, G * D).transpose(0, 2, 1, 3)
    new = jnp.pad(new, ((0, 0), (0, 0), (0, LANES - DS), (0, 0))).astype(BF)
    nsub = past_len // CMP_STRIDE
    sub_per_page = page // CMP_STRIDE
    cover = _cover_matrix(nsub, 2 * LANES)
    tok = np.arange(past_len)[None, :] // SEL_LEN
    expand = jnp.asarray(np.arange(past_len // SEL_LEN)[:, None] == tok, BF)
    r = np.arange(page)
    perm = jnp.asarray((r % sub_per_page)[:, None] * CMP_STRIDE + (r // sub_per_page)[:, None] == r[None, :], BF)
    blk = lambda *s: pl.BlockSpec((1,) + s, lambda b, pt: (b,) + (0,) * len(s))
    full = lambda a: pl.BlockSpec(a.shape, lambda b, pt: (0,) * a.ndim)
    out = pl.pallas_call(
        functools.partial(_nsa_sample_kernel, layer=layer, n_pages=n_pages, page=page, past_len=past_len, n_new=DS),
        grid_spec=pltpu.PrefetchScalarGridSpec(
            num_scalar_prefetch=1,
            grid=(DB,),
            in_specs=[pl.BlockSpec(memory_space=pl.ANY), blk(rows, LANES), blk(rows, LANES),
                      blk(4, LANES, G * D),
                      pl.BlockSpec((None, 1) + win_t.shape[2:], lambda b, pt: (layer, b, 0, 0)),
                      full(perm), full(pos), full(wk), full(wv), full(w2), full(cover), full(expand)],
            out_specs=blk(rows, LANES),
            scratch_shapes=[pltpu.VMEM((2 * G * D, past_len), F32), pltpu.VMEM((2 * G * D, past_len), F32),
                            pltpu.VMEM((nsub, CMP_STRIDE * LANES), F32), pltpu.VMEM((nsub, CMP_STRIDE * LANES), F32),
                            pltpu.VMEM((nsub + 8, G * CMP_HIDDEN), F32), pltpu.SemaphoreType.DMA((2,))]),
        out_shape=jax.ShapeDtypeStruct((DB, rows, LANES), F32),
        compiler_params=_params("arbitrary"),
        name="nsa_sample",
    )(page_table, cache_t, qs, gts, new, win_t, perm, pos, wk, wv, w2, cover, expand)
    out = out.reshape(DB, G, HG, TPAD, G, D)[:, :, :, :DS]
    out = jnp.stack([out[:, g, :, :, g] for g in range(G)], axis=1)
    return out.transpose(0, 3, 1, 2, 4).reshape(DB * DS, H * D).astype(BF)


def _memkv_kernel(x_ref, g_ref, w_ref, kg_ref, seg_ref, o_ref, ob_ref):
    z = _dot(_rms(x_ref[...], g_ref[...]).astype(BF), w_ref[...])
    kw = XA_HEADS * XA_HD
    for j in range(2 * kw // LANES):
        sl = slice(j * LANES, (j + 1) * LANES)
        zc = z[:, sl]
        if j * LANES < kw:
            zc = _head_norm(zc, kg_ref[...], seg_ref[...])
        o_ref[:, sl] = zc
        ob_ref[:, sl] = zc.astype(BF)


def memory_kv(mem, g, w, kg, seg, tm=512):
    M, D = mem.shape
    N = w.shape[1]
    row = lambda wd: pl.BlockSpec((tm, wd), lambda i: (i, 0))
    return pl.pallas_call(
        _memkv_kernel,
        grid=(M // tm,),
        in_specs=[row(D), _full((1, D)), _full(w.shape), _full(kg.shape), _full(seg.shape)],
        out_specs=[row(N), row(N)],
        out_shape=[jax.ShapeDtypeStruct((M, N), F32), jax.ShapeDtypeStruct((M, N), BF)],
        compiler_params=_params("parallel"),
        name="memory_kv",
    )(mem, g.reshape(1, D), w, kg, seg)


def _postmix_kernel(x_ref, a_ref, b_ref, wa_ref, wb_ref, gx_ref, wq_ref, qg_ref, seg_ref, mkv_ref, wo_ref, o_ref,
                    *, rows_per_batch, n_mem):
    x1 = x_ref[...] + _dot(a_ref[...], wa_ref[...]) + _dot(b_ref[...], wb_ref[...])
    xn = _rms(x1, gx_ref[...]).astype(BF)
    zq = _dot(xn, wq_ref[...])
    kw = XA_HEADS * XA_HD
    q = jnp.concatenate([_head_norm(zq[:, j * LANES:(j + 1) * LANES], qg_ref[...], seg_ref[...])
                         for j in range(kw // LANES)], axis=1)
    q = (q * XA_HD ** -0.5).astype(BF)
    tm = x1.shape[0]
    many = rows_per_batch < tm
    mask = None
    if many:
        nk = mkv_ref.shape[1]
        r = lax.broadcasted_iota(jnp.int32, (tm, nk), 0) // rows_per_batch
        c = lax.broadcasted_iota(jnp.int32, (tm, nk), 1) // n_mem
        mask = r == c
    outs = []
    for h in range(XA_HEADS):
        qh = q[:, h * XA_HD:(h + 1) * XA_HD]
        if many:
            s = _dot(qh, mkv_ref[h * XA_HD:(h + 1) * XA_HD, :])
            s = jnp.where(mask, s, NEG)
        else:
            s = _dot_nt(qh, mkv_ref[0, :, h * XA_HD:(h + 1) * XA_HD])
        e = jnp.exp(s - jnp.max(s, axis=1, keepdims=True))
        if many:
            pv = _dot_nt(e.astype(BF), mkv_ref[kw + h * XA_HD:kw + (h + 1) * XA_HD, :])
        else:
            pv = _dot(e.astype(BF), mkv_ref[0, :, kw + h * XA_HD:kw + (h + 1) * XA_HD])
        outs.append(pv / jnp.sum(e, axis=1, keepdims=True))
    o = jnp.concatenate(outs, axis=1).astype(BF)
    o_ref[...] = x1 + _dot(o, wo_ref[...])


def post_mix(x, a, b, wa, wb, gx, wq, qg, seg, mkvb, wo, tm, rows_per_batch):
    M, D = x.shape
    if rows_per_batch >= tm:
        n_mem = mkvb.shape[1]
        per = rows_per_batch // tm
        mspec = pl.BlockSpec((1, n_mem, mkvb.shape[2]), lambda i: (i // per, 0, 0))
    else:
        assert tm == M
        n_mem = mkvb.shape[1] // (M // rows_per_batch)
        mspec = _full(mkvb.shape)
    row = lambda wd: pl.BlockSpec((tm, wd), lambda i: (i, 0))
    return pl.pallas_call(
        functools.partial(_postmix_kernel, rows_per_batch=rows_per_batch, n_mem=n_mem),
        grid=(M // tm,),
        in_specs=[row(D), row(a.shape[1]), row(b.shape[1]), _full(wa.shape), _full(wb.shape), _full((1, D)),
                  _full(wq.shape), _full(qg.shape), _full(seg.shape), mspec, _full(wo.shape)],
        out_specs=row(D),
        out_shape=jax.ShapeDtypeStruct((M, D), F32),
        compiler_params=_params("parallel"),
        name="post_mix",
    )(x, a, b, wa, wb, gx.reshape(1, D), wq, qg, seg, mkvb, wo)


def _mlp_kernel(x_ref, g_ref, w1_ref, w2_ref, o_ref, xn_ref, acc_ref):
    f = pl.program_id(1)

    @pl.when(f == 0)
    def _():
        x = x_ref[...]
        xn_ref[...] = _rms(x, g_ref[...]).astype(BF)
        acc_ref[...] = x

    h = _dot(xn_ref[...], w1_ref[...].astype(BF))
    h = jnp.square(jnp.maximum(h, 0.0)).astype(BF)
    acc_ref[...] += _dot(h, w2_ref[...].astype(BF))

    @pl.when(f == pl.num_programs(1) - 1)
    def _():
        o_ref[...] = acc_ref[...]


def mlp_block(x, g, w1b, w2b, layer, tm, tf=1024):
    M, D = x.shape
    FF = w1b.shape[2]
    return pl.pallas_call(
        _mlp_kernel,
        grid=(M // tm, FF // tf),
        in_specs=[pl.BlockSpec((tm, D), lambda i, f: (i, 0)),
                  pl.BlockSpec((1, D), lambda i, f: (0, 0)),
                  pl.BlockSpec((None, D, tf), lambda i, f: (layer, 0, f)),
                  pl.BlockSpec((None, tf, D), lambda i, f: (layer, f, 0))],
        out_specs=pl.BlockSpec((tm, D), lambda i, f: (i, 0)),
        out_shape=jax.ShapeDtypeStruct((M, D), F32),
        scratch_shapes=[pltpu.VMEM((tm, D), BF), pltpu.VMEM((tm, D), F32)],
        compiler_params=_params("parallel", "arbitrary"),
        name="mlp",
    )(x, g.reshape(1, D), w1b, w2b)


ML_CHUNK = 128
CONV_PAD = 8
ML_PAR = 2


def _dot3(a, b):
    hi = b.astype(BF)
    r1 = b - hi.astype(F32)
    mid = r1.astype(BF)
    lo = (r1 - mid.astype(F32)).astype(BF)
    return _dot(a, hi) + _dot(a, mid) + _dot(a, lo)


def _mlstm_kernel(zqk_ref, zv_ref, zo_ref, gi_ref, cw_ref, cb_ref, gb_ref, hg_ref, tril_ref,
                  c0_ref, n0_ref, m0_ref, cv0_ref,
                  h_out, c_out, n_out, m_out, cv_out,
                  xin_ref, c_ref, n_ref, m_ref, *, n_valid):
    ci = pl.program_id(1)

    @pl.when(ci == 0)
    def _():
        c_ref[...] = c0_ref[...]
        n_ref[...] = n0_ref[...]
        m_ref[...] = m0_ref[...]
        xin_ref[:, pl.ds(0, CONV_PAD), :] = cv0_ref[...]

    for e in range(ML_PAR):
        _mlstm_chunk(zqk_ref.at[e], zv_ref.at[e], zo_ref.at[e], gi_ref.at[e], cw_ref, cb_ref, gb_ref, hg_ref, tril_ref,
                     h_out.at[e], cv_out.at[e], xin_ref.at[e], c_ref.at[e], n_ref.at[e], m_ref.at[e], n_valid)

    @pl.when(ci == pl.num_programs(1) - 1)
    def _():
        c_out[...] = c_ref[...]
        n_out[...] = n_ref[...]
        m_out[...] = m_ref[...]


def _mlstm_chunk(zqk_ref, zv_ref, zo_ref, gi_ref, cw_ref, cb_ref, gb_ref, hg_ref, tril_ref,
                 h_out, cv_out, xin_ref, c_ref, n_ref, m_ref, n_valid):
    L = ML_CHUNK
    xin_ref[pl.ds(CONV_PAD, L), :] = zqk_ref[...]
    y = cb_ref[...]
    for j in range(CONV_W):
        y = y + cw_ref[j:j + 1, :] * xin_ref[pl.ds(CONV_PAD - (CONV_W - 1) + j, L), :]
    qk = y * (1.0 / (1.0 + jnp.exp(-y)))
    hist = xin_ref[pl.ds(n_valid, CONV_PAD), :]
    xin_ref[pl.ds(0, CONV_PAD), :] = hist
    cv_out[...] = hist

    row = lax.broadcasted_iota(jnp.int32, (L, L), 0)
    col = lax.broadcasted_iota(jnp.int32, (L, L), 1)
    valid = row < n_valid
    gl = gi_ref[...] + gb_ref[...]
    for h in range(ML_HEADS):
        hs = slice(h * ML_HD, (h + 1) * ML_HD)
        q = qk[:, hs]
        k = qk[:, ML_W + h * ML_HD:ML_W + (h + 1) * ML_HD] * (ML_HD ** -0.5)
        v = zv_ref[:, hs]
        gi_col = jnp.broadcast_to(gl[:, NSA_G_W + h:NSA_G_W + h + 1], (L, L))
        gf_col = jnp.broadcast_to(gl[:, NSA_G_W + ML_HEADS + h:NSA_G_W + ML_HEADS + h + 1], (L, L))
        li = jnp.where(valid, gi_col, NEG)
        lf = jnp.where(valid, jnp.minimum(gf_col, 0.0) - jnp.log(1.0 + jnp.exp(-jnp.abs(gf_col))), 0.0)
        b = _dot3(tril_ref[...], lf)
        m_prev = m_ref[h:h + 1, :]
        dm = jnp.where(col <= row, b - (b - li).T, NEG)
        inter = b + m_prev
        m_t = jnp.maximum(inter, jnp.max(dm, axis=1, keepdims=True))
        w = jnp.exp(dm - m_t)
        a = jnp.exp(inter - m_t)
        qb, kb, vb = q.astype(BF), k.astype(BF), v.astype(BF)
        wqk = w * _dot_nt(qb, kb)
        c_old = c_ref[h]
        n_old = n_ref[h:h + 1, :]
        num = a * _dot_nt(qb, c_old.astype(BF)) + _dot(wqk.astype(BF), vb)
        den = a * jnp.sum(q * n_old, axis=1, keepdims=True) + jnp.sum(wqk, axis=1, keepdims=True)
        hh = num / jnp.maximum(jnp.abs(den), jnp.exp(-m_t))
        m_new = m_t[L - 1:L, :]
        b_last = b[L - 1:L, :]
        wk = jnp.exp(b_last - b + li - m_new) * k
        decay = jnp.exp(b_last + m_prev - m_new)
        c_ref[h] = decay * c_old + _dot(v.T.astype(BF), wk.astype(BF))
        n_ref[h:h + 1, :] = decay * n_old + jnp.sum(wk, axis=0, keepdims=True)
        m_ref[h:h + 1, :] = m_new
        hn = hh * lax.rsqrt(jnp.mean(hh * hh, axis=1, keepdims=True) + EPS) * hg_ref[...]
        h_out[:, hs] = (hn * (1.0 / (1.0 + jnp.exp(-zo_ref[:, hs])))).astype(BF)


def mlstm(zqk, zv, zo, gi, conv_w, conv_b, gate_b, hn_g, state, nb, n_valid):
    M = zqk.shape[0]
    T = M // nb
    nch = T // ML_CHUNK
    assert n_valid == ML_CHUNK or nch == 1
    assert nb % ML_PAR == 0
    C0, n0, m0, cv0 = state
    m0 = jnp.broadcast_to(m0[:, :, None], (nb, ML_HEADS, LANES))
    cv0 = jnp.pad(cv0, ((0, 0), (CONV_PAD - (CONV_W - 1), 0), (0, 0)))
    gb = jnp.zeros((1, LANES), F32).at[0, NSA_G_W:NSA_G_W + 2 * ML_HEADS].set(gate_b)
    tril = jnp.asarray(np.tril(np.ones((ML_CHUNK, ML_CHUNK))), BF)
    zqk, zv, zo, gi = (a.reshape(nb, T, a.shape[1]) for a in (zqk, zv, zo, gi))
    row = lambda wd: pl.BlockSpec((ML_PAR, ML_CHUNK, wd), lambda b, c: (b, c, 0))
    st = lambda *shape: pl.BlockSpec((ML_PAR,) + shape, lambda b, c: (b,) + (0,) * len(shape))
    h, Cn, nn, mn, cvn = pl.pallas_call(
        functools.partial(_mlstm_kernel, n_valid=n_valid),
        grid=(nb // ML_PAR, nch),
        in_specs=[row(2 * ML_W), row(ML_W), row(ML_W), row(LANES),
                  _full(conv_w.shape), _full((1, 2 * ML_W)), _full((1, LANES)), _full((1, ML_HD)), _full(tril.shape),
                  st(ML_HEADS, ML_HD, ML_HD), st(ML_HEADS, ML_HD), st(ML_HEADS, LANES), st(CONV_PAD, 2 * ML_W)],
        out_specs=[row(ML_W), st(ML_HEADS, ML_HD, ML_HD), st(ML_HEADS, ML_HD), st(ML_HEADS, LANES),
                   st(CONV_PAD, 2 * ML_W)],
        out_shape=[jax.ShapeDtypeStruct((nb, T, ML_W), BF),
                   jax.ShapeDtypeStruct((nb, ML_HEADS, ML_HD, ML_HD), F32),
                   jax.ShapeDtypeStruct((nb, ML_HEADS, ML_HD), F32),
                   jax.ShapeDtypeStruct((nb, ML_HEADS, LANES), F32),
                   jax.ShapeDtypeStruct((nb, CONV_PAD, 2 * ML_W), F32)],
        scratch_shapes=[pltpu.VMEM((ML_PAR, CONV_PAD + ML_CHUNK, 2 * ML_W), F32),
                        pltpu.VMEM((ML_PAR, ML_HEADS, ML_HD, ML_HD), F32),
                        pltpu.VMEM((ML_PAR, ML_HEADS, ML_HD), F32),
                        pltpu.VMEM((ML_PAR, ML_HEADS, LANES), F32)],
        compiler_params=_params("parallel", "arbitrary"),
        name="mlstm",
    )(zqk, zv, zo, gi, conv_w, conv_b.reshape(1, -1), gb, hn_g.reshape(1, -1), tril, C0, n0, m0, cv0)
    return h.reshape(M, ML_W), (Cn, nn, mn[:, :, 0], cvn[:, CONV_PAD - (CONV_W - 1):])


def _rope_tables(pos, rows):
    half = ROPE_DIMS // 2
    freqs = ROPE_THETA ** (-jnp.arange(half, dtype=F32) / half)
    ang = pos.astype(F32)[:, None] * freqs
    cos, sin = jnp.cos(ang), jnp.sin(ang)
    n = pos.shape[0]
    one, zero = jnp.ones((n, NSA_HD - ROPE_DIMS), F32), jnp.zeros((n, NSA_HD - ROPE_DIMS), F32)
    zh = jnp.zeros((n, half), F32)
    c = jnp.concatenate([cos, cos, one], axis=1)
    sa = jnp.concatenate([-sin, zh, zero], axis=1)
    sb = jnp.concatenate([zh, sin, zero], axis=1)
    tile = lambda t: jnp.tile(t, (rows // n, LANES // NSA_HD))
    return tile(c), tile(sa), tile(sb)


def _seg_matrix():
    i = np.arange(LANES)
    return jnp.asarray((i[:, None] // NSA_HD == i[None, :] // NSA_HD) / NSA_HD, BF)


def _cover_matrix(nc_rows, lanes):
    ci = np.arange(nc_rows)[:, None] * CMP_STRIDE
    sj = np.arange(lanes)[None, :]
    return jnp.asarray((ci < (sj + 1) * SEL_LEN) & (ci + CMP_LEN > sj * SEL_LEN), BF)


def _expand_matrix(n_chunks):
    j = np.arange(LANES)[None, :, None]
    c = np.arange(n_chunks)[:, None, None]
    s = np.arange(KCH)[None, None, :]
    return jnp.asarray(j == c * (KCH // SEL_LEN) + s // SEL_LEN, BF)


def _in_weights(w):
    zq, zkv, zg, zqk, zv, zo, zif = jnp.split(w, np.cumsum(COL_SPLITS)[:-1].tolist(), axis=1)
    pad = jnp.zeros((w.shape[0], LANES - NSA_G_W - 2 * ML_HEADS), w.dtype)
    return tuple(a.astype(BF) for a in (zq, zkv, zqk, zv, zo, jnp.concatenate([zg, zif, pad], axis=1)))


def _compress_weights(cmp_pos, cmp_w1, cmp_w2):
    G, D, S = NSA_KV_HEADS, NSA_HD, CMP_STRIDE
    pos = jnp.tile(cmp_pos[:, :, None, :], (1, 1, G, 1))
    pos = pos.transpose(1, 0, 2, 3).reshape(CMP_LEN, 2 * G * D)
    w1 = cmp_w1.reshape(2, CMP_LEN, D, CMP_HIDDEN)
    z = jnp.zeros_like(w1)
    w1g = jnp.stack([jnp.concatenate([w1, z], axis=2), jnp.concatenate([z, w1], axis=2)], axis=1)
    return (pos[:S], pos[S:], w1g[:, :, :S].astype(BF), w1g[:, :, S:].astype(BF), cmp_w2.astype(BF))


def kernel(x_prompt, x_sample, mem_prompt, cache_nsa_kv, cache_win_kv, cache_mem_kv, state_mlstm_C, state_mlstm_n, state_mlstm_m, state_conv, page_table, norm_mix_g, w_in, nsa_qn_g, nsa_kn_g, cmp_pos, cmp_w1, cmp_w2, ml_conv_w, ml_conv_b, ml_gate_b, ml_hn_g, w_out, norm_xa_g, norm_mem_g, xa_wq, xa_wkv, xa_qn_g, xa_kn_g, xa_wo, norm_mlp_g, mlp_w1, mlp_w2):
    B, T, D = x_prompt.shape
    DB, DS, _ = x_sample.shape
    depth = w_in.shape[0]
    n_mem = mem_prompt.shape[1]
    P = page_table.shape[1] * cache_nsa_kv.shape[1]
    wkeep = min(WINDOW, T)
    G, HD = NSA_KV_HEADS, NSA_HD
    MP, MS = B * T, DB * DS
    tm_p, tm_s = 512, MS

    pos_p = jnp.arange(T, dtype=jnp.int32)
    pos_s = P + jnp.arange(DS, dtype=jnp.int32)
    tabs_p = _rope_tables(pos_p, T)
    tabs_s = _rope_tables(pos_s, MS)
    seg = _seg_matrix()
    cover = _cover_matrix(T // CMP_STRIDE, LANES).T
    expand = _expand_matrix(T // KCH)
    tile2 = lambda v: jnp.tile(v.reshape(-1, HD), (1, LANES // HD))
    pad_chunk = lambda a: jnp.pad(a.reshape(DB, DS, -1), ((0, 0), (0, ML_CHUNK - DS), (0, 0))).reshape(DB * ML_CHUNK, -1)

    n_pool, page = cache_nsa_kv.shape[:2]
    cache_t = cache_nsa_kv.transpose(0, 2, 3, 4, 5, 1).reshape(n_pool, depth, 4 * G * HD, page)
    win_t = cache_win_kv.transpose(0, 1, 3, 4, 5, 2).reshape(depth, DB, 2 * G * HD, cache_win_kv.shape[2])

    xp = x_prompt.reshape(MP, D)
    xs = x_sample.reshape(MS, D)
    mem = mem_prompt.reshape(B * n_mem, D)
    nsa_p, nsa_s, win_p, win_s = [], [], [], []
    C_p, C_s, n_p, n_s, m_p, m_s = [], [], [], [], [], []
    cv_p, cv_s, mem_p = [], [], []
    for l in range(depth):
        w_l = _in_weights(w_in[l])
        qg, kg = tile2(nsa_qn_g[l]), tile2(nsa_kn_g[l])
        cw = _compress_weights(cmp_pos[l], cmp_w1[l], cmp_w2[l])
        wa, wb = w_out[l, :NSA_Q_W].astype(BF), w_out[l, NSA_Q_W:].astype(BF)
        xq, xo = xa_wq[l].astype(BF), xa_wo[l].astype(BF)
        xqg = tile2(xa_qn_g[l])
        w1b, w2b = mlp_w1, mlp_w2

        q, kvc, kvw, kb, va, zqk, zv, zo, gi = in_proj(xp, norm_mix_g[l], w_l, tabs_p, qg, kg, seg, tm_p)
        kcv = compress_prompt(kvc, cw, B, T)
        a_p = nsa_prompt(q, gi, kb, va, kcv, cover, expand, B, T)
        nsa_p.append(kvc.reshape(B, T, 4, G, HD))
        win_p.append(kvw.reshape(B, T, 2, G, HD)[:, T - wkeep:])
        st0 = (jnp.zeros((B, ML_HEADS, ML_HD, ML_HD), F32), jnp.zeros((B, ML_HEADS, ML_HD), F32),
               jnp.zeros((B, ML_HEADS), F32), jnp.zeros((B, CONV_W - 1, 2 * ML_W), F32))
        b_p, (Cn, nn, mn, buf_p) = mlstm(zqk, zv, zo, gi, ml_conv_w[l], ml_conv_b[l], ml_gate_b[l], ml_hn_g[l],
                                         st0, B, ML_CHUNK)
        C_p.append(Cn); n_p.append(nn); m_p.append(mn); cv_p.append(buf_p)
        mkv, mkvb = memory_kv(mem, norm_mem_g[l], xa_wkv[l].astype(BF), tile2(xa_kn_g[l]), seg)
        mem_p.append(mkv.reshape(B, n_mem, 2, XA_HEADS, XA_HD))
        xp = post_mix(xp, a_p, b_p, wa, wb, norm_xa_g[l], xq, xqg, seg, mkvb.reshape(B, n_mem, -1), xo, tm_p, T)
        xp = mlp_block(xp, norm_mlp_g[l], w1b, w2b, l, tm=1024)

        q, kvc, kvw, _, _, zqk, zv, zo, gi = in_proj(xs, norm_mix_g[l], w_l, tabs_s, qg, kg, seg, tm_s)
        kv = jnp.concatenate([kvc, kvw], axis=1)
        kvp = kv.reshape(DB, DS, 3, 2, G, HD)
        a_s = nsa_sample(page_table, cache_t, l, q, gi, kv, win_t,
                         _sample_compress_weights(cmp_pos[l], cmp_w1[l], cmp_w2[l]), DB, DS)
        nsa_s.append(kvp[:, :, :2].reshape(DB, DS, 4, G, HD))
        win_s.append(kvp[:, :, 2])
        st = (state_mlstm_C[l], state_mlstm_n[l], state_mlstm_m[l], state_conv[l])
        b_s, (Cn, nn, mn, buf_s) = mlstm(*(pad_chunk(a) for a in (zqk, zv, zo, gi)), ml_conv_w[l], ml_conv_b[l],
                                         ml_gate_b[l], ml_hn_g[l], st, DB, DS)
        b_s = b_s.reshape(DB, ML_CHUNK, ML_W)[:, :DS].reshape(MS, ML_W)
        C_s.append(Cn); n_s.append(nn); m_s.append(mn); cv_s.append(buf_s)
        mkvb_s = cache_mem_kv[l].transpose(2, 3, 4, 0, 1).reshape(2 * XA_HEADS * XA_HD, DB * n_mem).astype(BF)
        xs = post_mix(xs, a_s, b_s, wa, wb, norm_xa_g[l], xq, xqg, seg, mkvb_s, xo, tm_s, DS)
        xs = mlp_block(xs, norm_mlp_g[l], w1b, w2b, l, tm=MS)
    return (xp.reshape(B, T, D), xs.reshape(DB, DS, D),
            jnp.stack(nsa_p, axis=2), jnp.stack(nsa_s, axis=2),
            jnp.stack(win_p),
            jnp.concatenate([cache_win_kv, jnp.stack(win_s)], axis=2)[:, :, -cache_win_kv.shape[2]:],
            jnp.stack(C_p), jnp.stack(C_s),
            jnp.stack(n_p), jnp.stack(n_s),
            jnp.stack(m_p), jnp.stack(m_s),
            jnp.stack(cv_p), jnp.stack(cv_s),
            jnp.stack(mem_p))
```

```python
import functools

import jax
import jax.numpy as jnp
import numpy as np
from jax import lax
from jax.experimental import pallas as pl
from jax.experimental.pallas import tpu as pltpu

F32 = jnp.float32
BF = jnp.bfloat16

EPS = 1e-6
NSA_HEADS, NSA_KV_HEADS, NSA_HD = 8, 2, 64
NSA_GROUP = NSA_HEADS // NSA_KV_HEADS
CMP_LEN, CMP_STRIDE, CMP_HIDDEN = 32, 16, 128
SEL_LEN, N_SEL, WINDOW = 64, 8, 512
ROPE_DIMS, ROPE_THETA = NSA_HD // 4, 500000.0
ML_HEADS, ML_HD, CONV_W = 4, 128, 4
ML_W = ML_HEADS * ML_HD
XA_HEADS, XA_HD = 4, 64
LANES = 128
QB = 256
VMEM_LIMIT = 56 * 1024 * 1024

NSA_Q_W = NSA_HEADS * NSA_HD
NSA_KV_W = 6 * NSA_KV_HEADS * NSA_HD
NSA_G_W = 3 * NSA_HEADS
COL_SPLITS = (NSA_Q_W, NSA_KV_W, NSA_G_W, 2 * ML_W, ML_W, ML_W, 2 * ML_HEADS)
NEG = -1e30


def _dot(a, b):
    return jnp.dot(a, b, preferred_element_type=F32)


def _dot_nt(a, b):
    return lax.dot_general(a, b, (((1,), (1,)), ((), ())), preferred_element_type=F32)


def _dot_hl(a, b):
    hi = a.astype(BF)
    lo = (a - hi.astype(F32)).astype(BF)
    return _dot(hi, b) + _dot(lo, b)


def _rms(x, g):
    return x * lax.rsqrt(jnp.mean(x * x, axis=-1, keepdims=True) + EPS) * g


def _head_norm(z, g, seg):
    return z * lax.rsqrt(_dot_hl(z * z, seg) + EPS) * g


def _rope(y, c, sa, sb):
    half = ROPE_DIMS // 2
    return y * c + pltpu.roll(y, LANES - half, 1) * sa + pltpu.roll(y, half, 1) * sb


def _params(*sem):
    return pltpu.CompilerParams(dimension_semantics=sem, vmem_limit_bytes=VMEM_LIMIT)


def _full(shape):
    n = len(shape)
    return pl.BlockSpec(shape, lambda *_: (0,) * n)


def _inproj_kernel(x_ref, g_ref, wq_ref, wkv_ref, wqk_ref, wv_ref, wo_ref, wgi_ref,
                   rc_ref, rsa_ref, rsb_ref, qg_ref, kg_ref, seg_ref,
                   q_out, kvc_out, kvw_out, kb_out, va_out, zqk_out, zv_out, zo_out, gi_out):
    D = NSA_HD
    xn = _rms(x_ref[...], g_ref[...]).astype(BF)
    c, sa, sb = rc_ref[...], rsa_ref[...], rsb_ref[...]
    seg = seg_ref[...]
    lane = lax.broadcasted_iota(jnp.int32, c.shape, 1)
    low = lane < D
    swap = lambda z: pltpu.roll(z, D, 1)
    zq = _dot(xn, wq_ref[...])
    for j in range(NSA_Q_W // LANES):
        y = _rope(_head_norm(zq[:, j * LANES:(j + 1) * LANES], qg_ref[...], seg), c, sa, sb) * D ** -0.5
        ys = swap(y)
        for i in range(2):
            h = 2 * j + i
            want_low = h // NSA_GROUP == 0
            src = y if want_low == (i == 0) else ys
            q_out[:, h * LANES:(h + 1) * LANES] = jnp.where(low if want_low else ~low, src, 0.0).astype(BF)
    zkv = _dot(xn, wkv_ref[...])
    for j in range(NSA_KV_W // LANES):
        sl = slice(j * LANES, (j + 1) * LANES)
        z = zkv[:, sl]
        br = j // 2
        if j % 2 == 0:
            z = _rope(_head_norm(z, kg_ref[br:br + 1, :], seg), c, sa, sb)
            if br > 0:
                kb_out[:, (br - 1) * LANES:br * LANES] = z.astype(BF)
        elif br > 0:
            one = jnp.where(lane == D, 1.0, 0.0)
            for g, src in enumerate((z, swap(z))):
                col = ((br - 1) * NSA_KV_HEADS + g) * LANES
                va_out[:, col:col + LANES] = jnp.where(low, src, one).astype(BF)
        if br < 2:
            kvc_out[:, sl] = z
        else:
            kvw_out[:, (j - 4) * LANES:(j - 3) * LANES] = z
    zqk_out[...] = _dot(xn, wqk_ref[...])
    zv_out[...] = _dot(xn, wv_ref[...])
    zo_out[...] = _dot(xn, wo_ref[...])
    gi_out[...] = _dot(xn, wgi_ref[...])


def in_proj(x, g, w, rope_tabs, qg, kg, seg, tm):
    M, D = x.shape
    rc, rsa, rsb = rope_tabs
    nt = rc.shape[0] // tm
    widths = (NSA_HEADS * LANES, 4 * LANES, 2 * LANES, 2 * LANES, 4 * LANES, 2 * ML_W, ML_W, ML_W, LANES)
    dtypes = (BF, F32, F32, BF, BF, F32, F32, F32, F32)
    row = lambda wd: pl.BlockSpec((tm, wd), lambda i: (i, 0))
    tab = pl.BlockSpec((tm, LANES), lambda i: (i % nt, 0))
    return pl.pallas_call(
        _inproj_kernel,
        grid=(M // tm,),
        in_specs=[row(D), _full((1, D))] + [_full(a.shape) for a in w] + [tab, tab, tab,
                  _full(qg.shape), _full(kg.shape), _full(seg.shape)],
        out_specs=[row(wd) for wd in widths],
        out_shape=[jax.ShapeDtypeStruct((M, wd), dt) for wd, dt in zip(widths, dtypes)],
        compiler_params=_params("parallel"),
        name="in_proj",
    )(x, g.reshape(1, D), *w, rc, rsa, rsb, qg, kg, seg)


def _compress_body(xk_ref, xv_ref, pa_ref, pb_ref, w1a_ref, w1b_ref, w2_ref, sh_ref, nsub):
    G = NSA_KV_HEADS
    acc_a = [jnp.zeros((nsub, CMP_HIDDEN), F32) for _ in range(2 * G)]
    acc_b = [jnp.zeros((nsub, CMP_HIDDEN), F32) for _ in range(2 * G)]
    for u in range(CMP_STRIDE):
        for kind, x_ref in enumerate((xk_ref, xv_ref)):
            sl = slice(kind * LANES, (kind + 1) * LANES)
            xu = x_ref[pl.ds(u, nsub, stride=CMP_STRIDE), :]
            xa = (xu + pa_ref[u:u + 1, sl]).astype(BF)
            xb = (xu + pb_ref[u:u + 1, sl]).astype(BF)
            for g in range(G):
                acc_a[kind * G + g] += _dot(xa, w1a_ref[kind, g, u])
                acc_b[kind * G + g] += _dot(xb, w1b_ref[kind, g, u])
    sh_ref[pl.ds(nsub, 8), :] = jnp.zeros((8, CMP_HIDDEN), F32)
    outs = []
    for kind in range(2):
        for g in range(G):
            sh_ref[pl.ds(0, nsub), :] = acc_b[kind * G + g]
            h = acc_a[kind * G + g] + sh_ref[pl.ds(1, nsub), :]
            h = h * (1.0 / (1.0 + jnp.exp(-h)))
            outs.append(_dot(h.astype(BF), w2_ref[kind]))
    return outs


def _compress_kernel(xk_ref, xv_ref, pa_ref, pb_ref, w1a_ref, w1b_ref, w2_ref, o_ref, sh_ref, *, nsub):
    kc0, kc1, vc0, vc1 = _compress_body(xk_ref, xv_ref, pa_ref, pb_ref, w1a_ref, w1b_ref, w2_ref, sh_ref, nsub)
    zero = jnp.zeros_like(vc0)
    for j, o in enumerate((kc0, kc1, vc0, zero, vc1, zero)):
        o_ref[0, :, j * NSA_HD:(j + 1) * NSA_HD] = o


def compress_prompt(kv, cw, B, T):
    pa, pb, w1a, w1b, w2 = cw
    nsub = T // CMP_STRIDE
    return pl.pallas_call(
        functools.partial(_compress_kernel, nsub=nsub),
        grid=(B,),
        in_specs=[pl.BlockSpec((T, LANES), lambda b: (b, 0)), pl.BlockSpec((T, LANES), lambda b: (b, 1)),
                  _full(pa.shape), _full(pb.shape), _full(w1a.shape), _full(w1b.shape), _full(w2.shape)],
        out_specs=pl.BlockSpec((1, nsub, 3 * LANES), lambda b: (b, 0, 0)),
        out_shape=jax.ShapeDtypeStruct((B, nsub, 3 * LANES), F32),
        scratch_shapes=[pltpu.VMEM((nsub + 8, CMP_HIDDEN), F32)],
        compiler_params=_params("parallel"),
        name="compress_prompt",
    )(kv, kv, pa, pb, w1a, w1b, w2)


def _select_topk(imp, cur, n_blocks, axis=1):
    j = lax.broadcasted_iota(jnp.int32, imp.shape, axis)
    jf = j.astype(F32)
    forced = (j == cur) | (j == 0)
    dead = (j > cur) | (j >= n_blocks)
    val = jnp.where(forced, jnp.inf, jnp.where(dead, -jnp.inf, imp))
    sel = jnp.zeros(imp.shape, F32)
    for _ in range(N_SEL):
        m = jnp.max(val, axis=axis, keepdims=True)
        idx = jnp.min(jnp.where(val == m, jf, float(imp.shape[axis])), axis=axis, keepdims=True)
        hit = jf == idx
        sel = jnp.where(hit & (m > -jnp.inf), 1.0, sel)
        val = jnp.where(hit, -jnp.inf, val)
    return sel


KCH = 2 * LANES


def _nsa_prompt_kernel(q_ref, gi_ref, kcv_ref, kb_ref, va_ref, cover_ref, exp_ref, o_ref,
                       s_ref, mx_ref, acc_ref, *, n_sel_blocks):
    G, HG, D, H = NSA_KV_HEADS, NSA_GROUP, NSA_HD, NSA_HEADS
    qb = pl.program_id(1)
    q0 = qb * QB
    qpos = q0 + lax.broadcasted_iota(jnp.int32, (QB, LANES), 0)
    lane = lax.broadcasted_iota(jnp.int32, (QB, LANES), 1)
    gates = 1.0 / (1.0 + jnp.exp(-gi_ref[...]))
    nc = kcv_ref.shape[1]
    qp = jnp.concatenate([q_ref[:, h * LANES:(h + 1) * LANES] for h in range(H)], axis=0)

    s_all = _dot_nt(qp, kcv_ref[0, :, 0:LANES].astype(BF))
    cmask = (lane[:, :nc] * CMP_STRIDE + (CMP_LEN - 1)) <= qpos[:, :nc]
    o_c, psums = [], []
    for g in range(G):
        vc = kcv_ref[0, :, (1 + g) * LANES:(2 + g) * LANES].astype(BF)
        psum = jnp.zeros((QB, nc), F32)
        for h in range(HG):
            hh = g * HG + h
            s = jnp.where(cmask, s_all[hh * QB:(hh + 1) * QB], NEG)
            e = jnp.where(cmask, jnp.exp(s - jnp.max(s, axis=1, keepdims=True)), 0.0)
            den = jnp.sum(e, axis=1, keepdims=True)
            p = e / jnp.where(den > 0, den, 1.0)
            psum = psum + p
            o_c.append(_dot(p.astype(BF), vc))
        psums.append(psum)
    psum = jnp.concatenate(psums, axis=0)
    hi = psum.astype(BF)
    lo = (psum - hi.astype(F32)).astype(BF)
    imp_t = _dot_nt(cover_ref[...], hi) + _dot_nt(cover_ref[...], lo)
    nb8 = -(-n_sel_blocks // 8) * 8
    cur_t = (q0 + lax.broadcasted_iota(jnp.int32, (nb8, G * QB), 1) % QB) // SEL_LEN
    sel_t = _select_topk(imp_t[:nb8], cur_t, n_sel_blocks, axis=0)
    sel_t = jnp.concatenate([sel_t, jnp.zeros((LANES - nb8, G * QB), F32)], axis=0)
    sel_all = sel_t.T.astype(BF)
    sels = [sel_all[g * QB:(g + 1) * QB] for g in range(G)]

    qpos2 = q0 + lax.broadcasted_iota(jnp.int32, (QB, KCH), 0)
    lane2 = lax.broadcasted_iota(jnp.int32, (QB, KCH), 1)
    branch_out = []
    for br in range(2):
        mx_ref[...] = jnp.full(mx_ref.shape, NEG, F32)
        acc_ref[...] = jnp.zeros(acc_ref.shape, F32)

        def scores(c, carry, br=br):
            k0 = pl.multiple_of(c * KCH, KCH)
            s_all = _dot_nt(qp, kb_ref[pl.ds(k0, KCH), br * LANES:(br + 1) * LANES])
            d = qpos2 - (k0 + lane2)
            if br == 0:
                masks = [(_dot(sels[g], exp_ref[c]) > 0.5) & (d >= 0) for g in range(G)]
            else:
                masks = [(d >= 0) & (d <= WINDOW)] * G
            for hh in range(H):
                rows = pl.ds(hh * QB, QB)
                s = jnp.where(masks[hh // HG], s_all[hh * QB:(hh + 1) * QB], NEG)
                s_ref[c, rows, :] = s
                mx_ref[rows, :] = functools.reduce(
                    jnp.maximum, [mx_ref[rows, :]] + [s[:, i * LANES:(i + 1) * LANES] for i in range(KCH // LANES)])
            return carry

        def values(c, carry, br=br):
            k0 = pl.multiple_of(c * KCH, KCH)
            for g in range(G):
                rows = pl.ds(g * HG * QB, HG * QB)
                mx = mx_ref[rows, :]
                p = jnp.concatenate([jnp.exp(s_ref[c, rows, pl.ds(i * LANES, LANES)] - mx)
                                     for i in range(KCH // LANES)], axis=1).astype(BF)
                acc_ref[rows, :] += _dot(p, va_ref[pl.ds(k0, KCH), (br * G + g) * LANES:(br * G + g + 1) * LANES])
            return carry

        lo = 0 if br == 0 else jnp.maximum(q0 - WINDOW, 0) // KCH
        hi = (q0 + QB - 1) // KCH + 1
        lax.fori_loop(lo, hi, scores, 0)
        mx_ref[...] = jnp.broadcast_to(jnp.max(mx_ref[...], axis=1, keepdims=True), mx_ref.shape)
        lax.fori_loop(lo, hi, values, 0)
        r = acc_ref[...]
        branch_out.append(r / r[:, D:D + 1])

    for j in range(H // 2):
        tiles = []
        for hh in (2 * j, 2 * j + 1):
            rows = slice(hh * QB, (hh + 1) * QB)
            tiles.append(gates[:, 3 * hh:3 * hh + 1] * o_c[hh]
                         + gates[:, 3 * hh + 1:3 * hh + 2] * branch_out[0][rows]
                         + gates[:, 3 * hh + 2:3 * hh + 3] * branch_out[1][rows])
        o_ref[:, j * LANES:(j + 1) * LANES] = jnp.where(lane < D, tiles[0], pltpu.roll(tiles[1], D, 1)).astype(BF)


def nsa_prompt(q, gi, kb, va, kcv, cover, expand, B, T):
    nqb = T // QB
    nc = kcv.shape[1]
    rows = NSA_HEADS * QB
    return pl.pallas_call(
        functools.partial(_nsa_prompt_kernel, n_sel_blocks=T // SEL_LEN),
        grid=(B, nqb),
        in_specs=[pl.BlockSpec((QB, q.shape[1]), lambda b, i: (b * nqb + i, 0)),
                  pl.BlockSpec((QB, LANES), lambda b, i: (b * nqb + i, 0)),
                  pl.BlockSpec((1, nc, kcv.shape[2]), lambda b, i: (b, 0, 0)),
                  pl.BlockSpec((T, kb.shape[1]), lambda b, i: (b, 0)),
                  pl.BlockSpec((T, va.shape[1]), lambda b, i: (b, 0)),
                  _full(cover.shape), _full(expand.shape)],
        out_specs=pl.BlockSpec((QB, NSA_Q_W), lambda b, i: (b * nqb + i, 0)),
        out_shape=jax.ShapeDtypeStruct((B * T, NSA_Q_W), BF),
        scratch_shapes=[pltpu.VMEM((T // KCH, rows, KCH), F32), pltpu.VMEM((rows, LANES), F32),
                        pltpu.VMEM((rows, LANES), F32)],
        compiler_params=_params("parallel", "arbitrary"),
        name="nsa_prompt",
    )(q, gi, kcv, kb, va, cover, expand)


TPAD = 8


def _masked_softmax_parts(parts):
    m = functools.reduce(jnp.maximum, [jnp.max(jnp.where(k, s, NEG), axis=1, keepdims=True) for s, k in parts])
    es = [jnp.where(k, jnp.exp(jnp.where(k, s, NEG) - m), 0.0) for s, k in parts]
    den = functools.reduce(jnp.add, [jnp.sum(e, axis=1, keepdims=True) for e in es])
    return es, jnp.where(den > 0, den, 1.0)


def _nsa_sample_kernel(pt_ref, ct_ref, q_ref, gt_ref, new_ref, wt_ref, perm_ref, pos_ref, wk_ref, wv_ref, w2_ref,
                       cover_ref, exp_ref, o_ref, cbuf, sbuf, xk_ref, xv_ref, sh_ref, sem,
                       *, layer, n_pages, page, past_len, n_new):
    G, HG, D = NSA_KV_HEADS, NSA_GROUP, NSA_HD
    b = pl.program_id(0)
    nb = pl.num_programs(0)
    half_rows = 2 * G * D
    bufs = (cbuf, sbuf)
    sub_per_page = page // CMP_STRIDE

    def page_copy(bb, p, half):
        return pltpu.make_async_copy(
            ct_ref.at[pt_ref[bb, p], layer, pl.ds(half * half_rows, half_rows), :],
            bufs[half].at[:, pl.ds(pl.multiple_of(p * page, page), page)], sem.at[half])

    def start_all(bb, half):
        lax.fori_loop(0, n_pages, lambda p, c: (page_copy(bb, p, half).start(), c)[1], 0)

    def wait_all(bb, half):
        lax.fori_loop(0, n_pages, lambda p, c: (page_copy(bb, p, half).wait(), c)[1], 0)

    @pl.when(b == 0)
    def _():
        start_all(0, 0)
        start_all(0, 1)

    wait_all(b, 0)

    def to_rows(p, c):
        p0 = pl.multiple_of(p * page, page)
        z = _dot_nt(perm_ref[...], cbuf[:, pl.ds(p0, page)].astype(BF))
        r0 = pl.multiple_of(p * sub_per_page, sub_per_page)
        for u in range(CMP_STRIDE):
            zu = z[u * sub_per_page:(u + 1) * sub_per_page]
            xk_ref[pl.ds(r0, sub_per_page), u * LANES:(u + 1) * LANES] = zu[:, 0:G * D]
            xv_ref[pl.ds(r0, sub_per_page), u * LANES:(u + 1) * LANES] = zu[:, G * D:2 * G * D]
        return c

    lax.fori_loop(0, n_pages, to_rows, 0, unroll=4)

    @pl.when(b + 1 < nb)
    def _():
        start_all(b + 1, 0)

    nsub = past_len // CMP_STRIDE
    half_w = G * CMP_HIDDEN
    sh_ref[pl.ds(nsub, 8), :] = jnp.zeros((8, half_w), F32)
    kcv = []
    for kind, (x_ref, w_ref) in enumerate(((xk_ref, wk_ref), (xv_ref, wv_ref))):
        bias = _dot(pos_ref[kind], w_ref[...])
        h_all = _dot(x_ref[...].astype(BF), w_ref[...])
        sh_ref[pl.ds(0, nsub), :] = h_all[:, half_w:] + bias[1:2, half_w:]
        h = h_all[:, :half_w] + bias[0:1, :half_w] + sh_ref[pl.ds(1, nsub), :]
        h = h * (1.0 / (1.0 + jnp.exp(-h)))
        kcv.append(_dot(h.astype(BF), w2_ref[kind]).astype(BF))
    kc, vc = kcv

    rows = G * HG * TPAD
    q = q_ref[0]
    t_rows = lax.broadcasted_iota(jnp.int32, (rows, LANES), 0) % TPAD
    lane_r = lax.broadcasted_iota(jnp.int32, (rows, LANES), 1)
    new_ok = (lane_r <= t_rows) & (lane_r < n_new)
    tile_gh = lambda xs: jnp.concatenate([x for x in xs for _ in range(HG)], axis=0)

    s = _dot_nt(q, kc)
    c_end = lax.broadcasted_iota(jnp.int32, s.shape, 1) * CMP_STRIDE + (CMP_LEN - 1)
    q_pos = past_len + lax.broadcasted_iota(jnp.int32, s.shape, 0) % TPAD
    (e,), den = _masked_softmax_parts([(s, c_end <= q_pos)])
    p = e / den
    o_c = _dot(p.astype(BF), vc)
    psum = jnp.concatenate([functools.reduce(jnp.add, [p[(g * HG + h) * TPAD:(g * HG + h + 1) * TPAD]
                                                       for h in range(HG)]) for g in range(G)], axis=0)
    imp = _dot_hl(psum, cover_ref[...])
    cur = (past_len + lax.broadcasted_iota(jnp.int32, imp.shape, 0) % TPAD) // SEL_LEN
    sel = _select_topk(imp, cur, -(-(past_len + n_new) // SEL_LEN))

    wait_all(b, 1)
    n_past_blocks = past_len // SEL_LEN
    selx = _dot(sel[:, :n_past_blocks].astype(BF), exp_ref[...])
    in_new = jnp.broadcast_to(sel[:, n_past_blocks:n_past_blocks + 1], (G * TPAD, LANES))
    split_g = lambda x: [x[g * TPAD:(g + 1) * TPAD] for g in range(G)]
    s_past = _dot(q, sbuf[0:G * D, :].astype(BF))
    s_new = _dot_nt(q, new_ref[0, 0])
    (e_past, e_new), den = _masked_softmax_parts(
        [(s_past, tile_gh(split_g(selx)) > 0.5), (s_new, (tile_gh(split_g(in_new)) > 0.5) & new_ok)])
    o_t = _dot_nt(sbuf[G * D:2 * G * D, :].astype(BF), e_past.astype(BF))
    o_s = (o_t.T + _dot(e_new.astype(BF), new_ref[0, 1])) / den

    @pl.when(b + 1 < nb)
    def _():
        start_all(b + 1, 1)

    wb = wt_ref.shape[2]
    s_buf = _dot(q, wt_ref[0, 0:G * D, :].astype(BF))
    i = lax.broadcasted_iota(jnp.int32, s_buf.shape, 1)
    t = lax.broadcasted_iota(jnp.int32, s_buf.shape, 0) % TPAD
    s_new = _dot_nt(q, new_ref[0, 2])
    (e_buf, e_new), den = _masked_softmax_parts([(s_buf, wb + t - i <= WINDOW), (s_new, new_ok)])
    o_t = _dot_nt(wt_ref[0, G * D:2 * G * D, :].astype(BF), e_buf.astype(BF))
    o_w = (o_t.T + _dot(e_new.astype(BF), new_ref[0, 3])) / den
    gates = 1.0 / (1.0 + jnp.exp(-gt_ref[0]))
    o_ref[0] = gates[:, 0:1] * o_c + gates[:, 1:2] * o_s + gates[:, 2:3] * o_w


def _sample_compress_weights(cmp_pos, cmp_w1, cmp_w2):
    G, D, S, Hd = NSA_KV_HEADS, NSA_HD, CMP_STRIDE, CMP_HIDDEN
    w1 = cmp_w1.reshape(2, 2, S, D, Hd)
    eye = jnp.eye(G, dtype=w1.dtype)
    wk = jnp.einsum('khudn,gj->kugdhjn', w1, eye).reshape(2, S * G * D, 2 * G * Hd).astype(BF)
    pos = cmp_pos.reshape(2, 2, S, 1, D)
    pos = jnp.broadcast_to(pos, (2, 2, S, G, D)).reshape(2, 2, S * G * D)
    pos = jnp.pad(pos, ((0, 0), (0, 8 - 2), (0, 0))).astype(BF)
    w2 = jnp.einsum('knd,gj->kgnjd', cmp_w2, eye).reshape(2, G * Hd, G * D).astype(BF)
    return wk[0], wk[1], pos, w2


def nsa_sample(page_table, cache_t, layer, q, gi, kv, win_t, cw, DB, DS):
    G, HG, D, H = NSA_KV_HEADS, NSA_GROUP, NSA_HD, NSA_HEADS
    n_pages, page = page_table.shape[1], cache_t.shape[3]
    past_len = n_pages * page
    rows = H * TPAD
    wk, wv, pos, w2 = cw
    heads = lambda a, w: a.reshape(DB, DS, H, w).transpose(0, 2, 1, 3)
    padt = lambda a: jnp.pad(a, ((0, 0), (0, 0), (0, TPAD - DS), (0, 0)))
    qs = padt(heads(q, LANES)).reshape(DB, rows, LANES)
    gts = padt(heads(gi[:, :NSA_G_W], 3)).reshape(DB, rows, 3)
    gts = jnp.pad(gts, ((0, 0), (0, 0), (0, LANES - 3)))
    new = kv[:, 2 * G * D:].reshape(DB, DS, 4, G * D).transpose(0, 2, 1, 3)
    new = jnp.pad(new, ((0, 0), (0, 0), (0, LANES - DS), (0, 0))).astype(BF)
    nsub = past_len // CMP_STRIDE
    sub_per_page = page // CMP_STRIDE
    cover = _cover_matrix(nsub, 2 * LANES)
    tok = np.arange(past_len)[None, :] // SEL_LEN
    expand = jnp.asarray(np.arange(past_len // SEL_LEN)[:, None] == tok, BF)
    r = np.arange(page)
    perm = jnp.asarray((r % sub_per_page)[:, None] * CMP_STRIDE + (r // sub_per_page)[:, None] == r[None, :], BF)
    blk = lambda *s: pl.BlockSpec((1,) + s, lambda b, pt: (b,) + (0,) * len(s))
    full = lambda a: pl.BlockSpec(a.shape, lambda b, pt: (0,) * a.ndim)
    out = pl.pallas_call(
        functools.partial(_nsa_sample_kernel, layer=layer, n_pages=n_pages, page=page, past_len=past_len, n_new=DS),
        grid_spec=pltpu.PrefetchScalarGridSpec(
            num_scalar_prefetch=1,
            grid=(DB,),
            in_specs=[pl.BlockSpec(memory_space=pl.ANY), blk(rows, LANES), blk(rows, LANES),
                      blk(4, LANES, G * D),
                      pl.BlockSpec((None, 1) + win_t.shape[2:], lambda b, pt: (layer, b, 0, 0)),
                      full(perm), full(pos), full(wk), full(wv), full(w2), full(cover), full(expand)],
            out_specs=blk(rows, LANES),
            scratch_shapes=[pltpu.VMEM((2 * G * D, past_len), F32), pltpu.VMEM((2 * G * D, past_len), F32),
                            pltpu.VMEM((nsub, CMP_STRIDE * LANES), F32), pltpu.VMEM((nsub, CMP_STRIDE * LANES), F32),
                            pltpu.VMEM((nsub + 8, G * CMP_HIDDEN), F32), pltpu.SemaphoreType.DMA((2,))]),
        out_shape=jax.ShapeDtypeStruct((DB, rows, LANES), F32),
        compiler_params=_params("arbitrary"),
        name="nsa_sample",
    )(page_table, cache_t, qs, gts, new, win_t, perm, pos, wk, wv, w2, cover, expand)
    out = out.reshape(DB, G, HG, TPAD, G, D)[:, :, :, :DS]
    out = jnp.stack([out[:, g, :, :, g] for g in range(G)], axis=1)
    return out.transpose(0, 3, 1, 2, 4).reshape(DB * DS, H * D).astype(BF)


def _memkv_kernel(x_ref, g_ref, w_ref, kg_ref, seg_ref, o_ref, ob_ref):
    z = _dot(_rms(x_ref[...], g_ref[...]).astype(BF), w_ref[...])
    kw = XA_HEADS * XA_HD
    for j in range(2 * kw // LANES):
        sl = slice(j * LANES, (j + 1) * LANES)
        zc = z[:, sl]
        if j * LANES < kw:
            zc = _head_norm(zc, kg_ref[...], seg_ref[...])
        o_ref[:, sl] = zc
        ob_ref[:, sl] = zc.astype(BF)


def memory_kv(mem, g, w, kg, seg, tm=512):
    M, D = mem.shape
    N = w.shape[1]
    row = lambda wd: pl.BlockSpec((tm, wd), lambda i: (i, 0))
    return pl.pallas_call(
        _memkv_kernel,
        grid=(M // tm,),
        in_specs=[row(D), _full((1, D)), _full(w.shape), _full(kg.shape), _full(seg.shape)],
        out_specs=[row(N), row(N)],
        out_shape=[jax.ShapeDtypeStruct((M, N), F32), jax.ShapeDtypeStruct((M, N), BF)],
        compiler_params=_params("parallel"),
        name="memory_kv",
    )(mem, g.reshape(1, D), w, kg, seg)


def _postmix_kernel(x_ref, a_ref, b_ref, wa_ref, wb_ref, gx_ref, wq_ref, qg_ref, seg_ref, mkv_ref, wo_ref, o_ref,
                    *, rows_per_batch, n_mem):
    x1 = x_ref[...] + _dot(a_ref[...], wa_ref[...]) + _dot(b_ref[...], wb_ref[...])
    xn = _rms(x1, gx_ref[...]).astype(BF)
    zq = _dot(xn, wq_ref[...])
    kw = XA_HEADS * XA_HD
    q = jnp.concatenate([_head_norm(zq[:, j * LANES:(j + 1) * LANES], qg_ref[...], seg_ref[...])
                         for j in range(kw // LANES)], axis=1)
    q = (q * XA_HD ** -0.5).astype(BF)
    tm = x1.shape[0]
    many = rows_per_batch < tm
    mask = None
    if many:
        nk = mkv_ref.shape[1]
        r = lax.broadcasted_iota(jnp.int32, (tm, nk), 0) // rows_per_batch
        c = lax.broadcasted_iota(jnp.int32, (tm, nk), 1) // n_mem
        mask = r == c
    outs = []
    for h in range(XA_HEADS):
        qh = q[:, h * XA_HD:(h + 1) * XA_HD]
        if many:
            s = _dot(qh, mkv_ref[h * XA_HD:(h + 1) * XA_HD, :])
            s = jnp.where(mask, s, NEG)
        else:
            s = _dot_nt(qh, mkv_ref[0, :, h * XA_HD:(h + 1) * XA_HD])
        e = jnp.exp(s - jnp.max(s, axis=1, keepdims=True))
        if many:
            pv = _dot_nt(e.astype(BF), mkv_ref[kw + h * XA_HD:kw + (h + 1) * XA_HD, :])
        else:
            pv = _dot(e.astype(BF), mkv_ref[0, :, kw + h * XA_HD:kw + (h + 1) * XA_HD])
        outs.append(pv / jnp.sum(e, axis=1, keepdims=True))
    o = jnp.concatenate(outs, axis=1).astype(BF)
    o_ref[...] = x1 + _dot(o, wo_ref[...])


def post_mix(x, a, b, wa, wb, gx, wq, qg, seg, mkvb, wo, tm, rows_per_batch):
    M, D = x.shape
    if rows_per_batch >= tm:
        n_mem = mkvb.shape[1]
        per = rows_per_batch // tm
        mspec = pl.BlockSpec((1, n_mem, mkvb.shape[2]), lambda i: (i // per, 0, 0))
    else:
        assert tm == M
        n_mem = mkvb.shape[1] // (M // rows_per_batch)
        mspec = _full(mkvb.shape)
    row = lambda wd: pl.BlockSpec((tm, wd), lambda i: (i, 0))
    return pl.pallas_call(
        functools.partial(_postmix_kernel, rows_per_batch=rows_per_batch, n_mem=n_mem),
        grid=(M // tm,),
        in_specs=[row(D), row(a.shape[1]), row(b.shape[1]), _full(wa.shape), _full(wb.shape), _full((1, D)),
                  _full(wq.shape), _full(qg.shape), _full(seg.shape), mspec, _full(wo.shape)],
        out_specs=row(D),
        out_shape=jax.ShapeDtypeStruct((M, D), F32),
        compiler_params=_params("parallel"),
        name="post_mix",
    )(x, a, b, wa, wb, gx.reshape(1, D), wq, qg, seg, mkvb, wo)


def _mlp_kernel(x_ref, g_ref, w1_ref, w2_ref, o_ref, xn_ref, acc_ref):
    f = pl.program_id(1)

    @pl.when(f == 0)
    def _():
        x = x_ref[...]
        xn_ref[...] = _rms(x, g_ref[...]).astype(BF)
        acc_ref[...] = x

    h = _dot(xn_ref[...], w1_ref[...].astype(BF))
    h = jnp.square(jnp.maximum(h, 0.0)).astype(BF)
    acc_ref[...] += _dot(h, w2_ref[...].astype(BF))

    @pl.when(f == pl.num_programs(1) - 1)
    def _():
        o_ref[...] = acc_ref[...]


def mlp_block(x, g, w1b, w2b, layer, tm, tf=1024):
    M, D = x.shape
    FF = w1b.shape[2]
    return pl.pallas_call(
        _mlp_kernel,
        grid=(M // tm, FF // tf),
        in_specs=[pl.BlockSpec((tm, D), lambda i, f: (i, 0)),
                  pl.BlockSpec((1, D), lambda i, f: (0, 0)),
                  pl.BlockSpec((None, D, tf), lambda i, f: (layer, 0, f)),
                  pl.BlockSpec((None, tf, D), lambda i, f: (layer, f, 0))],
        out_specs=pl.BlockSpec((tm, D), lambda i, f: (i, 0)),
        out_shape=jax.ShapeDtypeStruct((M, D), F32),
        scratch_shapes=[pltpu.VMEM((tm, D), BF), pltpu.VMEM((tm, D), F32)],
        compiler_params=_params("parallel", "arbitrary"),
        name="mlp",
    )(x, g.reshape(1, D), w1b, w2b)


ML_CHUNK = 128
CONV_PAD = 8
ML_PAR = 4


def _dot3(a, b):
    hi = b.astype(BF)
    r1 = b - hi.astype(F32)
    mid = r1.astype(BF)
    lo = (r1 - mid.astype(F32)).astype(BF)
    return _dot(a, hi) + _dot(a, mid) + _dot(a, lo)


def _mlstm_kernel(zqk_ref, zv_ref, zo_ref, gi_ref, cw_ref, cb_ref, gb_ref, hg_ref, tril_ref,
                  c0_ref, n0_ref, m0_ref, cv0_ref,
                  h_out, c_out, n_out, m_out, cv_out,
                  xin_ref, c_ref, n_ref, m_ref, *, n_valid):
    ci = pl.program_id(1)

    @pl.when(ci == 0)
    def _():
        c_ref[...] = c0_ref[...]
        n_ref[...] = n0_ref[...]
        m_ref[...] = m0_ref[...]
        xin_ref[:, pl.ds(0, CONV_PAD), :] = cv0_ref[...]

    for e in range(ML_PAR):
        _mlstm_chunk(zqk_ref.at[e], zv_ref.at[e], zo_ref.at[e], gi_ref.at[e], cw_ref, cb_ref, gb_ref, hg_ref, tril_ref,
                     h_out.at[e], cv_out.at[e], xin_ref.at[e], c_ref.at[e], n_ref.at[e], m_ref.at[e], n_valid)

    @pl.when(ci == pl.num_programs(1) - 1)
    def _():
        c_out[...] = c_ref[...]
        n_out[...] = n_ref[...]
        m_out[...] = m_ref[...]


def _mlstm_chunk(zqk_ref, zv_ref, zo_ref, gi_ref, cw_ref, cb_ref, gb_ref, hg_ref, tril_ref,
                 h_out, cv_out, xin_ref, c_ref, n_ref, m_ref, n_valid):
    L = ML_CHUNK
    xin_ref[pl.ds(CONV_PAD, L), :] = zqk_ref[...]
    y = cb_ref[...]
    for j in range(CONV_W):
        y = y + cw_ref[j:j + 1, :] * xin_ref[pl.ds(CONV_PAD - (CONV_W - 1) + j, L), :]
    qk = y * (1.0 / (1.0 + jnp.exp(-y)))
    hist = xin_ref[pl.ds(n_valid, CONV_PAD), :]
    xin_ref[pl.ds(0, CONV_PAD), :] = hist
    cv_out[...] = hist

    row = lax.broadcasted_iota(jnp.int32, (L, L), 0)
    col = lax.broadcasted_iota(jnp.int32, (L, L), 1)
    valid = row < n_valid
    gl = gi_ref[...] + gb_ref[...]
    for h in range(ML_HEADS):
        hs = slice(h * ML_HD, (h + 1) * ML_HD)
        q = qk[:, hs]
        k = qk[:, ML_W + h * ML_HD:ML_W + (h + 1) * ML_HD] * (ML_HD ** -0.5)
        v = zv_ref[:, hs]
        gi_col = jnp.broadcast_to(gl[:, NSA_G_W + h:NSA_G_W + h + 1], (L, L))
        gf_col = jnp.broadcast_to(gl[:, NSA_G_W + ML_HEADS + h:NSA_G_W + ML_HEADS + h + 1], (L, L))
        li = jnp.where(valid, gi_col, NEG)
        lf = jnp.where(valid, jnp.minimum(gf_col, 0.0) - jnp.log(1.0 + jnp.exp(-jnp.abs(gf_col))), 0.0)
        b = _dot3(tril_ref[...], lf)
        m_prev = m_ref[h:h + 1, :]
        dm = jnp.where(col <= row, b - (b - li).T, NEG)
        inter = b + m_prev
        m_t = jnp.maximum(inter, jnp.max(dm, axis=1, keepdims=True))
        w = jnp.exp(dm - m_t)
        a = jnp.exp(inter - m_t)
        qb, kb, vb = q.astype(BF), k.astype(BF), v.astype(BF)
        wqk = w * _dot_nt(qb, kb)
        c_old = c_ref[h]
        n_old = n_ref[h:h + 1, :]
        num = a * _dot_nt(qb, c_old.astype(BF)) + _dot(wqk.astype(BF), vb)
        den = a * jnp.sum(q * n_old, axis=1, keepdims=True) + jnp.sum(wqk, axis=1, keepdims=True)
        hh = num / jnp.maximum(jnp.abs(den), jnp.exp(-m_t))
        m_new = m_t[L - 1:L, :]
        b_last = b[L - 1:L, :]
        wk = jnp.exp(b_last - b + li - m_new) * k
        decay = jnp.exp(b_last + m_prev - m_new)
        c_ref[h] = decay * c_old + _dot(v.T.astype(BF), wk.astype(BF))
        n_ref[h:h + 1, :] = decay * n_old + jnp.sum(wk, axis=0, keepdims=True)
        m_ref[h:h + 1, :] = m_new
        hn = hh * lax.rsqrt(jnp.mean(hh * hh, axis=1, keepdims=True) + EPS) * hg_ref[...]
        h_out[:, hs] = (hn * (1.0 / (1.0 + jnp.exp(-zo_ref[:, hs])))).astype(BF)


def mlstm(zqk, zv, zo, gi, conv_w, conv_b, gate_b, hn_g, state, nb, n_valid):
    M = zqk.shape[0]
    T = M // nb
    nch = T // ML_CHUNK
    assert n_valid == ML_CHUNK or nch == 1
    assert nb % ML_PAR == 0
    C0, n0, m0, cv0 = state
    m0 = jnp.broadcast_to(m0[:, :, None], (nb, ML_HEADS, LANES))
    cv0 = jnp.pad(cv0, ((0, 0), (CONV_PAD - (CONV_W - 1), 0), (0, 0)))
    gb = jnp.zeros((1, LANES), F32).at[0, NSA_G_W:NSA_G_W + 2 * ML_HEADS].set(gate_b)
    tril = jnp.asarray(np.tril(np.ones((ML_CHUNK, ML_CHUNK))), BF)
    zqk, zv, zo, gi = (a.reshape(nb, T, a.shape[1]) for a in (zqk, zv, zo, gi))
    row = lambda wd: pl.BlockSpec((ML_PAR, ML_CHUNK, wd), lambda b, c: (b, c, 0))
    st = lambda *shape: pl.BlockSpec((ML_PAR,) + shape, lambda b, c: (b,) + (0,) * len(shape))
    h, Cn, nn, mn, cvn = pl.pallas_call(
        functools.partial(_mlstm_kernel, n_valid=n_valid),
        grid=(nb // ML_PAR, nch),
        in_specs=[row(2 * ML_W), row(ML_W), row(ML_W), row(LANES),
                  _full(conv_w.shape), _full((1, 2 * ML_W)), _full((1, LANES)), _full((1, ML_HD)), _full(tril.shape),
                  st(ML_HEADS, ML_HD, ML_HD), st(ML_HEADS, ML_HD), st(ML_HEADS, LANES), st(CONV_PAD, 2 * ML_W)],
        out_specs=[row(ML_W), st(ML_HEADS, ML_HD, ML_HD), st(ML_HEADS, ML_HD), st(ML_HEADS, LANES),
                   st(CONV_PAD, 2 * ML_W)],
        out_shape=[jax.ShapeDtypeStruct((nb, T, ML_W), BF),
                   jax.ShapeDtypeStruct((nb, ML_HEADS, ML_HD, ML_HD), F32),
                   jax.ShapeDtypeStruct((nb, ML_HEADS, ML_HD), F32),
                   jax.ShapeDtypeStruct((nb, ML_HEADS, LANES), F32),
                   jax.ShapeDtypeStruct((nb, CONV_PAD, 2 * ML_W), F32)],
        scratch_shapes=[pltpu.VMEM((ML_PAR, CONV_PAD + ML_CHUNK, 2 * ML_W), F32),
                        pltpu.VMEM((ML_PAR, ML_HEADS, ML_HD, ML_HD), F32),
                        pltpu.VMEM((ML_PAR, ML_HEADS, ML_HD), F32),
                        pltpu.VMEM((ML_PAR, ML_HEADS, LANES), F32)],
        compiler_params=_params("parallel", "arbitrary"),
        name="mlstm",
    )(zqk, zv, zo, gi, conv_w, conv_b.reshape(1, -1), gb, hn_g.reshape(1, -1), tril, C0, n0, m0, cv0)
    return h.reshape(M, ML_W), (Cn, nn, mn[:, :, 0], cvn[:, CONV_PAD - (CONV_W - 1):])


def _rope_tables(pos, rows):
    half = ROPE_DIMS // 2
    freqs = ROPE_THETA ** (-jnp.arange(half, dtype=F32) / half)
    ang = pos.astype(F32)[:, None] * freqs
    cos, sin = jnp.cos(ang), jnp.sin(ang)
    n = pos.shape[0]
    one, zero = jnp.ones((n, NSA_HD - ROPE_DIMS), F32), jnp.zeros((n, NSA_HD - ROPE_DIMS), F32)
    zh = jnp.zeros((n, half), F32)
    c = jnp.concatenate([cos, cos, one], axis=1)
    sa = jnp.concatenate([-sin, zh, zero], axis=1)
    sb = jnp.concatenate([zh, sin, zero], axis=1)
    tile = lambda t: jnp.tile(t, (rows // n, LANES // NSA_HD))
    return tile(c), tile(sa), tile(sb)


def _seg_matrix():
    i = np.arange(LANES)
    return jnp.asarray((i[:, None] // NSA_HD == i[None, :] // NSA_HD) / NSA_HD, BF)


def _cover_matrix(nc_rows, lanes):
    ci = np.arange(nc_rows)[:, None] * CMP_STRIDE
    sj = np.arange(lanes)[None, :]
    return jnp.asarray((ci < (sj + 1) * SEL_LEN) & (ci + CMP_LEN > sj * SEL_LEN), BF)


def _expand_matrix(n_chunks):
    j = np.arange(LANES)[None, :, None]
    c = np.arange(n_chunks)[:, None, None]
    s = np.arange(KCH)[None, None, :]
    return jnp.asarray(j == c * (KCH // SEL_LEN) + s // SEL_LEN, BF)


def _in_weights(w):
    zq, zkv, zg, zqk, zv, zo, zif = jnp.split(w, np.cumsum(COL_SPLITS)[:-1].tolist(), axis=1)
    pad = jnp.zeros((w.shape[0], LANES - NSA_G_W - 2 * ML_HEADS), w.dtype)
    return tuple(a.astype(BF) for a in (zq, zkv, zqk, zv, zo, jnp.concatenate([zg, zif, pad], axis=1)))


def _compress_weights(cmp_pos, cmp_w1, cmp_w2):
    G, D, S = NSA_KV_HEADS, NSA_HD, CMP_STRIDE
    pos = jnp.tile(cmp_pos[:, :, None, :], (1, 1, G, 1))
    pos = pos.transpose(1, 0, 2, 3).reshape(CMP_LEN, 2 * G * D)
    w1 = cmp_w1.reshape(2, CMP_LEN, D, CMP_HIDDEN)
    z = jnp.zeros_like(w1)
    w1g = jnp.stack([jnp.concatenate([w1, z], axis=2), jnp.concatenate([z, w1], axis=2)], axis=1)
    return (pos[:S], pos[S:], w1g[:, :, :S].astype(BF), w1g[:, :, S:].astype(BF), cmp_w2.astype(BF))


def kernel(x_prompt, x_sample, mem_prompt, cache_nsa_kv, cache_win_kv, cache_mem_kv, state_mlstm_C, state_mlstm_n, state_mlstm_m, state_conv, page_table, norm_mix_g, w_in, nsa_qn_g, nsa_kn_g, cmp_pos, cmp_w1, cmp_w2, ml_conv_w, ml_conv_b, ml_gate_b, ml_hn_g, w_out, norm_xa_g, norm_mem_g, xa_wq, xa_wkv, xa_qn_g, xa_kn_g, xa_wo, norm_mlp_g, mlp_w1, mlp_w2):
    B, T, D = x_prompt.shape
    DB, DS, _ = x_sample.shape
    depth = w_in.shape[0]
    n_mem = mem_prompt.shape[1]
    P = page_table.shape[1] * cache_nsa_kv.shape[1]
    wkeep = min(WINDOW, T)
    G, HD = NSA_KV_HEADS, NSA_HD
    MP, MS = B * T, DB * DS
    tm_p, tm_s = 512, MS

    pos_p = jnp.arange(T, dtype=jnp.int32)
    pos_s = P + jnp.arange(DS, dtype=jnp.int32)
    tabs_p = _rope_tables(pos_p, T)
    tabs_s = _rope_tables(pos_s, MS)
    seg = _seg_matrix()
    cover = _cover_matrix(T // CMP_STRIDE, LANES).T
    expand = _expand_matrix(T // KCH)
    tile2 = lambda v: jnp.tile(v.reshape(-1, HD), (1, LANES // HD))
    pad_chunk = lambda a: jnp.pad(a.reshape(DB, DS, -1), ((0, 0), (0, ML_CHUNK - DS), (0, 0))).reshape(DB * ML_CHUNK, -1)

    n_pool, page = cache_nsa_kv.shape[:2]
    cache_t = cache_nsa_kv.transpose(0, 2, 3, 4, 5, 1).reshape(n_pool, depth, 4 * G * HD, page)
    win_t = cache_win_kv.transpose(0, 1, 3, 4, 5, 2).reshape(depth, DB, 2 * G * HD, cache_win_kv.shape[2])

    xp = x_prompt.reshape(MP, D)
    xs = x_sample.reshape(MS, D)
    mem = mem_prompt.reshape(B * n_mem, D)
    nsa_p, nsa_s, win_p, win_s = [], [], [], []
    C_p, C_s, n_p, n_s, m_p, m_s = [], [], [], [], [], []
    cv_p, cv_s, mem_p = [], [], []
    for l in range(depth):
        w_l = _in_weights(w_in[l])
        qg, kg = tile2(nsa_qn_g[l]), tile2(nsa_kn_g[l])
        cw = _compress_weights(cmp_pos[l], cmp_w1[l], cmp_w2[l])
        wa, wb = w_out[l, :NSA_Q_W].astype(BF), w_out[l, NSA_Q_W:].astype(BF)
        xq, xo = xa_wq[l].astype(BF), xa_wo[l].astype(BF)
        xqg = tile2(xa_qn_g[l])
        w1b, w2b = mlp_w1, mlp_w2

        q, kvc, kvw, kb, va, zqk, zv, zo, gi = in_proj(xp, norm_mix_g[l], w_l, tabs_p, qg, kg, seg, tm_p)
        kcv = compress_prompt(kvc, cw, B, T)
        a_p = nsa_prompt(q, gi, kb, va, kcv, cover, expand, B, T)
        nsa_p.append(kvc.reshape(B, T, 4, G, HD))
        win_p.append(kvw.reshape(B, T, 2, G, HD)[:, T - wkeep:])
        st0 = (jnp.zeros((B, ML_HEADS, ML_HD, ML_HD), F32), jnp.zeros((B, ML_HEADS, ML_HD), F32),
               jnp.zeros((B, ML_HEADS), F32), jnp.zeros((B, CONV_W - 1, 2 * ML_W), F32))
        b_p, (Cn, nn, mn, buf_p) = mlstm(zqk, zv, zo, gi, ml_conv_w[l], ml_conv_b[l], ml_gate_b[l], ml_hn_g[l],
                                         st0, B, ML_CHUNK)
        C_p.append(Cn); n_p.append(nn); m_p.append(mn); cv_p.append(buf_p)
        mkv, mkvb = memory_kv(mem, norm_mem_g[l], xa_wkv[l].astype(BF), tile2(xa_kn_g[l]), seg)
        mem_p.append(mkv.reshape(B, n_mem, 2, XA_HEADS, XA_HD))
        xp = post_mix(xp, a_p, b_p, wa, wb, norm_xa_g[l], xq, xqg, seg, mkvb.reshape(B, n_mem, -1), xo, tm_p, T)
        xp = mlp_block(xp, norm_mlp_g[l], w1b, w2b, l, tm=1024)

        q, kvc, kvw, _, _, zqk, zv, zo, gi = in_proj(xs, norm_mix_g[l], w_l, tabs_s, qg, kg, seg, tm_s)
        kv = jnp.concatenate([kvc, kvw], axis=1)
        kvp = kv.reshape(DB, DS, 3, 2, G, HD)
        a_s = nsa_sample(page_table, cache_t, l, q, gi, kv, win_t,
                         _sample_compress_weights(cmp_pos[l], cmp_w1[l], cmp_w2[l]), DB, DS)
        nsa_s.append(kvp[:, :, :2].reshape(DB, DS, 4, G, HD))
        win_s.append(kvp[:, :, 2])
        st = (state_mlstm_C[l], state_mlstm_n[l], state_mlstm_m[l], state_conv[l])
        b_s, (Cn, nn, mn, buf_s) = mlstm(*(pad_chunk(a) for a in (zqk, zv, zo, gi)), ml_conv_w[l], ml_conv_b[l],
                                         ml_gate_b[l], ml_hn_g[l], st, DB, DS)
        b_s = b_s.reshape(DB, ML_CHUNK, ML_W)[:, :DS].reshape(MS, ML_W)
        C_s.append(Cn); n_s.append(nn); m_s.append(mn); cv_s.append(buf_s)
        mkvb_s = cache_mem_kv[l].transpose(2, 3, 4, 0, 1).reshape(2 * XA_HEADS * XA_HD, DB * n_mem).astype(BF)
        xs = post_mix(xs, a_s, b_s, wa, wb, norm_xa_g[l], xq, xqg, seg, mkvb_s, xo, tm_s, DS)
        xs = mlp_block(xs, norm_mlp_g[l], w1b, w2b, l, tm=MS)
    return (xp.reshape(B, T, D), xs.reshape(DB, DS, D),
            jnp.stack(nsa_p, axis=2), jnp.stack(nsa_s, axis=2),
            jnp.stack(win_p),
            jnp.concatenate([cache_win_kv, jnp.stack(win_s)], axis=2)[:, :, -cache_win_kv.shape[2]:],
            jnp.stack(C_p), jnp.stack(C_s),
            jnp.stack(n_p), jnp.stack(n_s),
            jnp.stack(m_p), jnp.stack(m_s),
            jnp.stack(cv_p), jnp.stack(cv_s),
            jnp.stack(mem_p))
```

```python
import functools

import jax
import jax.numpy as jnp
import numpy as np
from jax import lax
from jax.experimental import pallas as pl
from jax.experimental.pallas import tpu as pltpu

F32 = jnp.float32
BF = jnp.bfloat16

EPS = 1e-6
NSA_HEADS, NSA_KV_HEADS, NSA_HD = 8, 2, 64
NSA_GROUP = NSA_HEADS // NSA_KV_HEADS
CMP_LEN, CMP_STRIDE, CMP_HIDDEN = 32, 16, 128
SEL_LEN, N_SEL, WINDOW = 64, 8, 512
ROPE_DIMS, ROPE_THETA = NSA_HD // 4, 500000.0
ML_HEADS, ML_HD, CONV_W = 4, 128, 4
ML_W = ML_HEADS * ML_HD
XA_HEADS, XA_HD = 4, 64
LANES = 128
QB = 256
VMEM_LIMIT = 56 * 1024 * 1024

NSA_Q_W = NSA_HEADS * NSA_HD
NSA_KV_W = 6 * NSA_KV_HEADS * NSA_HD
NSA_G_W = 3 * NSA_HEADS
COL_SPLITS = (NSA_Q_W, NSA_KV_W, NSA_G_W, 2 * ML_W, ML_W, ML_W, 2 * ML_HEADS)
NEG = -1e30


def _dot(a, b):
    return jnp.dot(a, b, preferred_element_type=F32)


def _dot_nt(a, b):
    return lax.dot_general(a, b, (((1,), (1,)), ((), ())), preferred_element_type=F32)


def _dot_hl(a, b):
    hi = a.astype(BF)
    lo = (a - hi.astype(F32)).astype(BF)
    return _dot(hi, b) + _dot(lo, b)


def _rms(x, g):
    return x * lax.rsqrt(jnp.mean(x * x, axis=-1, keepdims=True) + EPS) * g


def _head_norm(z, g, seg):
    return z * lax.rsqrt(_dot_hl(z * z, seg) + EPS) * g


def _rope(y, c, sa, sb):
    half = ROPE_DIMS // 2
    return y * c + pltpu.roll(y, LANES - half, 1) * sa + pltpu.roll(y, half, 1) * sb


def _params(*sem):
    return pltpu.CompilerParams(dimension_semantics=sem, vmem_limit_bytes=VMEM_LIMIT)


def _full(shape):
    n = len(shape)
    return pl.BlockSpec(shape, lambda *_: (0,) * n)


def _inproj_kernel(x_ref, g_ref, wq_ref, wkv_ref, wqk_ref, wv_ref, wo_ref, wgi_ref,
                   rc_ref, rsa_ref, rsb_ref, qg_ref, kg_ref, seg_ref,
                   q_out, kvc_out, kvw_out, kb_out, va_out, zqk_out, zv_out, zo_out, gi_out):
    D = NSA_HD
    xn = _rms(x_ref[...], g_ref[...]).astype(BF)
    c, sa, sb = rc_ref[...], rsa_ref[...], rsb_ref[...]
    seg = seg_ref[...]
    lane = lax.broadcasted_iota(jnp.int32, c.shape, 1)
    low = lane < D
    swap = lambda z: pltpu.roll(z, D, 1)
    zq = _dot(xn, wq_ref[...])
    for j in range(NSA_Q_W // LANES):
        y = _rope(_head_norm(zq[:, j * LANES:(j + 1) * LANES], qg_ref[...], seg), c, sa, sb) * D ** -0.5
        ys = swap(y)
        for i in range(2):
            h = 2 * j + i
            want_low = h // NSA_GROUP == 0
            src = y if want_low == (i == 0) else ys
            q_out[:, h * LANES:(h + 1) * LANES] = jnp.where(low if want_low else ~low, src, 0.0).astype(BF)
    zkv = _dot(xn, wkv_ref[...])
    for j in range(NSA_KV_W // LANES):
        sl = slice(j * LANES, (j + 1) * LANES)
        z = zkv[:, sl]
        br = j // 2
        if j % 2 == 0:
            z = _rope(_head_norm(z, kg_ref[br:br + 1, :], seg), c, sa, sb)
            if br > 0:
                kb_out[:, (br - 1) * LANES:br * LANES] = z.astype(BF)
        elif br > 0:
            one = jnp.where(lane == D, 1.0, 0.0)
            for g, src in enumerate((z, swap(z))):
                col = ((br - 1) * NSA_KV_HEADS + g) * LANES
                va_out[:, col:col + LANES] = jnp.where(low, src, one).astype(BF)
        if br < 2:
            kvc_out[:, sl] = z
        else:
            kvw_out[:, (j - 4) * LANES:(j - 3) * LANES] = z
    zqk_out[...] = _dot(xn, wqk_ref[...])
    zv_out[...] = _dot(xn, wv_ref[...])
    zo_out[...] = _dot(xn, wo_ref[...])
    gi_out[...] = _dot(xn, wgi_ref[...])


def in_proj(x, g, w, rope_tabs, qg, kg, seg, tm):
    M, D = x.shape
    rc, rsa, rsb = rope_tabs
    nt = rc.shape[0] // tm
    widths = (NSA_HEADS * LANES, 4 * LANES, 2 * LANES, 2 * LANES, 4 * LANES, 2 * ML_W, ML_W, ML_W, LANES)
    dtypes = (BF, F32, F32, BF, BF, F32, F32, F32, F32)
    row = lambda wd: pl.BlockSpec((tm, wd), lambda i: (i, 0))
    tab = pl.BlockSpec((tm, LANES), lambda i: (i % nt, 0))
    return pl.pallas_call(
        _inproj_kernel,
        grid=(M // tm,),
        in_specs=[row(D), _full((1, D))] + [_full(a.shape) for a in w] + [tab, tab, tab,
                  _full(qg.shape), _full(kg.shape), _full(seg.shape)],
        out_specs=[row(wd) for wd in widths],
        out_shape=[jax.ShapeDtypeStruct((M, wd), dt) for wd, dt in zip(widths, dtypes)],
        compiler_params=_params("parallel"),
        name="in_proj",
    )(x, g.reshape(1, D), *w, rc, rsa, rsb, qg, kg, seg)


def _compress_body(xk_ref, xv_ref, pa_ref, pb_ref, w1a_ref, w1b_ref, w2_ref, sh_ref, nsub):
    G = NSA_KV_HEADS
    acc_a = [jnp.zeros((nsub, CMP_HIDDEN), F32) for _ in range(2 * G)]
    acc_b = [jnp.zeros((nsub, CMP_HIDDEN), F32) for _ in range(2 * G)]
    for u in range(CMP_STRIDE):
        for kind, x_ref in enumerate((xk_ref, xv_ref)):
            sl = slice(kind * LANES, (kind + 1) * LANES)
            xu = x_ref[pl.ds(u, nsub, stride=CMP_STRIDE), :]
            xa = (xu + pa_ref[u:u + 1, sl]).astype(BF)
            xb = (xu + pb_ref[u:u + 1, sl]).astype(BF)
            for g in range(G):
                acc_a[kind * G + g] += _dot(xa, w1a_ref[kind, g, u])
                acc_b[kind * G + g] += _dot(xb, w1b_ref[kind, g, u])
    sh_ref[pl.ds(nsub, 8), :] = jnp.zeros((8, CMP_HIDDEN), F32)
    outs = []
    for kind in range(2):
        for g in range(G):
            sh_ref[pl.ds(0, nsub), :] = acc_b[kind * G + g]
            h = acc_a[kind * G + g] + sh_ref[pl.ds(1, nsub), :]
            h = h * (1.0 / (1.0 + jnp.exp(-h)))
            outs.append(_dot(h.astype(BF), w2_ref[kind]))
    return outs


def _compress_kernel(xk_ref, xv_ref, pa_ref, pb_ref, w1a_ref, w1b_ref, w2_ref, o_ref, sh_ref, *, nsub):
    kc0, kc1, vc0, vc1 = _compress_body(xk_ref, xv_ref, pa_ref, pb_ref, w1a_ref, w1b_ref, w2_ref, sh_ref, nsub)
    zero = jnp.zeros_like(vc0)
    for j, o in enumerate((kc0, kc1, vc0, zero, vc1, zero)):
        o_ref[0, :, j * NSA_HD:(j + 1) * NSA_HD] = o


def compress_prompt(kv, cw, B, T):
    pa, pb, w1a, w1b, w2 = cw
    nsub = T // CMP_STRIDE
    return pl.pallas_call(
        functools.partial(_compress_kernel, nsub=nsub),
        grid=(B,),
        in_specs=[pl.BlockSpec((T, LANES), lambda b: (b, 0)), pl.BlockSpec((T, LANES), lambda b: (b, 1)),
                  _full(pa.shape), _full(pb.shape), _full(w1a.shape), _full(w1b.shape), _full(w2.shape)],
        out_specs=pl.BlockSpec((1, nsub, 3 * LANES), lambda b: (b, 0, 0)),
        out_shape=jax.ShapeDtypeStruct((B, nsub, 3 * LANES), F32),
        scratch_shapes=[pltpu.VMEM((nsub + 8, CMP_HIDDEN), F32)],
        compiler_params=_params("parallel"),
        name="compress_prompt",
    )(kv, kv, pa, pb, w1a, w1b, w2)


def _select_topk(imp, cur, n_blocks, axis=1):
    j = lax.broadcasted_iota(jnp.int32, imp.shape, axis)
    jf = j.astype(F32)
    forced = (j == cur) | (j == 0)
    dead = (j > cur) | (j >= n_blocks)
    val = jnp.where(forced, jnp.inf, jnp.where(dead, -jnp.inf, imp))
    sel = jnp.zeros(imp.shape, F32)
    for _ in range(N_SEL):
        m = jnp.max(val, axis=axis, keepdims=True)
        idx = jnp.min(jnp.where(val == m, jf, float(imp.shape[axis])), axis=axis, keepdims=True)
        hit = jf == idx
        sel = jnp.where(hit & (m > -jnp.inf), 1.0, sel)
        val = jnp.where(hit, -jnp.inf, val)
    return sel


KCH = 2 * LANES


def _nsa_prompt_kernel(q_ref, gi_ref, kcv_ref, kb_ref, va_ref, cover_ref, exp_ref, o_ref,
                       s_ref, mx_ref, acc_ref, *, n_sel_blocks):
    G, HG, D, H = NSA_KV_HEADS, NSA_GROUP, NSA_HD, NSA_HEADS
    qb = pl.program_id(1)
    q0 = qb * QB
    qpos = q0 + lax.broadcasted_iota(jnp.int32, (QB, LANES), 0)
    lane = lax.broadcasted_iota(jnp.int32, (QB, LANES), 1)
    gates = 1.0 / (1.0 + jnp.exp(-gi_ref[...]))
    nc = kcv_ref.shape[1]
    qp = jnp.concatenate([q_ref[:, h * LANES:(h + 1) * LANES] for h in range(H)], axis=0)

    s_all = _dot_nt(qp, kcv_ref[0, :, 0:LANES].astype(BF))
    cmask = (lane[:, :nc] * CMP_STRIDE + (CMP_LEN - 1)) <= qpos[:, :nc]
    o_c, psums = [], []
    for g in range(G):
        vc = kcv_ref[0, :, (1 + g) * LANES:(2 + g) * LANES].astype(BF)
        psum = jnp.zeros((QB, nc), F32)
        for h in range(HG):
            hh = g * HG + h
            s = jnp.where(cmask, s_all[hh * QB:(hh + 1) * QB], NEG)
            e = jnp.where(cmask, jnp.exp(s - jnp.max(s, axis=1, keepdims=True)), 0.0)
            den = jnp.sum(e, axis=1, keepdims=True)
            p = e / jnp.where(den > 0, den, 1.0)
            psum = psum + p
            o_c.append(_dot(p.astype(BF), vc))
        psums.append(psum)
    psum = jnp.concatenate(psums, axis=0)
    hi = psum.astype(BF)
    lo = (psum - hi.astype(F32)).astype(BF)
    imp_t = _dot_nt(cover_ref[...], hi) + _dot_nt(cover_ref[...], lo)
    nb8 = -(-n_sel_blocks // 8) * 8
    cur_t = (q0 + lax.broadcasted_iota(jnp.int32, (nb8, G * QB), 1) % QB) // SEL_LEN
    sel_t = _select_topk(imp_t[:nb8], cur_t, n_sel_blocks, axis=0)
    sel_t = jnp.concatenate([sel_t, jnp.zeros((LANES - nb8, G * QB), F32)], axis=0)
    sel_all = sel_t.T.astype(BF)
    sels = [sel_all[g * QB:(g + 1) * QB] for g in range(G)]

    qpos2 = q0 + lax.broadcasted_iota(jnp.int32, (QB, KCH), 0)
    lane2 = lax.broadcasted_iota(jnp.int32, (QB, KCH), 1)
    branch_out = []
    for br in range(2):
        mx_ref[...] = jnp.full(mx_ref.shape, NEG, F32)
        acc_ref[...] = jnp.zeros(acc_ref.shape, F32)

        def scores(c, carry, br=br):
            k0 = pl.multiple_of(c * KCH, KCH)
            s_all = _dot_nt(qp, kb_ref[pl.ds(k0, KCH), br * LANES:(br + 1) * LANES])
            d = qpos2 - (k0 + lane2)
            if br == 0:
                masks = [(_dot(sels[g], exp_ref[c]) > 0.5) & (d >= 0) for g in range(G)]
            else:
                masks = [(d >= 0) & (d <= WINDOW)] * G
            for hh in range(H):
                rows = pl.ds(hh * QB, QB)
                s = jnp.where(masks[hh // HG], s_all[hh * QB:(hh + 1) * QB], NEG)
                s_ref[c, rows, :] = s
                mx_ref[rows, :] = functools.reduce(
                    jnp.maximum, [mx_ref[rows, :]] + [s[:, i * LANES:(i + 1) * LANES] for i in range(KCH // LANES)])
            return carry

        def values(c, carry, br=br):
            k0 = pl.multiple_of(c * KCH, KCH)
            for g in range(G):
                rows = pl.ds(g * HG * QB, HG * QB)
                mx = mx_ref[rows, :]
                p = jnp.concatenate([jnp.exp(s_ref[c, rows, pl.ds(i * LANES, LANES)] - mx)
                                     for i in range(KCH // LANES)], axis=1).astype(BF)
                acc_ref[rows, :] += _dot(p, va_ref[pl.ds(k0, KCH), (br * G + g) * LANES:(br * G + g + 1) * LANES])
            return carry

        lo = 0 if br == 0 else jnp.maximum(q0 - WINDOW, 0) // KCH
        hi = (q0 + QB - 1) // KCH + 1
        lax.fori_loop(lo, hi, scores, 0)
        mx_ref[...] = jnp.broadcast_to(jnp.max(mx_ref[...], axis=1, keepdims=True), mx_ref.shape)
        lax.fori_loop(lo, hi, values, 0)
        r = acc_ref[...]
        branch_out.append(r / r[:, D:D + 1])

    for j in range(H // 2):
        tiles = []
        for hh in (2 * j, 2 * j + 1):
            rows = slice(hh * QB, (hh + 1) * QB)
            tiles.append(gates[:, 3 * hh:3 * hh + 1] * o_c[hh]
                         + gates[:, 3 * hh + 1:3 * hh + 2] * branch_out[0][rows]
                         + gates[:, 3 * hh + 2:3 * hh + 3] * branch_out[1][rows])
        o_ref[:, j * LANES:(j + 1) * LANES] = jnp.where(lane < D, tiles[0], pltpu.roll(tiles[1], D, 1)).astype(BF)


def nsa_prompt(q, gi, kb, va, kcv, cover, expand, B, T):
    nqb = T // QB
    nc = kcv.shape[1]
    rows = NSA_HEADS * QB
    return pl.pallas_call(
        functools.partial(_nsa_prompt_kernel, n_sel_blocks=T // SEL_LEN),
        grid=(B, nqb),
        in_specs=[pl.BlockSpec((QB, q.shape[1]), lambda b, i: (b * nqb + i, 0)),
                  pl.BlockSpec((QB, LANES), lambda b, i: (b * nqb + i, 0)),
                  pl.BlockSpec((1, nc, kcv.shape[2]), lambda b, i: (b, 0, 0)),
                  pl.BlockSpec((T, kb.shape[1]), lambda b, i: (b, 0)),
                  pl.BlockSpec((T, va.shape[1]), lambda b, i: (b, 0)),
                  _full(cover.shape), _full(expand.shape)],
        out_specs=pl.BlockSpec((QB, NSA_Q_W), lambda b, i: (b * nqb + i, 0)),
        out_shape=jax.ShapeDtypeStruct((B * T, NSA_Q_W), BF),
        scratch_shapes=[pltpu.VMEM((T // KCH, rows, KCH), F32), pltpu.VMEM((rows, LANES), F32),
                        pltpu.VMEM((rows, LANES), F32)],
        compiler_params=_params("parallel", "arbitrary"),
        name="nsa_prompt",
    )(q, gi, kcv, kb, va, cover, expand)


TPAD = 8


def _masked_softmax_parts(parts):
    m = functools.reduce(jnp.maximum, [jnp.max(jnp.where(k, s, NEG), axis=1, keepdims=True) for s, k in parts])
    es = [jnp.where(k, jnp.exp(jnp.where(k, s, NEG) - m), 0.0) for s, k in parts]
    den = functools.reduce(jnp.add, [jnp.sum(e, axis=1, keepdims=True) for e in es])
    return es, jnp.where(den > 0, den, 1.0)


def _nsa_sample_kernel(pt_ref, ct_ref, q_ref, gt_ref, new_ref, wt_ref, perm_ref, pos_ref, wk_ref, wv_ref, w2_ref,
                       cover_ref, exp_ref, o_ref, cbuf, sbuf, xk_ref, xv_ref, sh_ref, sem,
                       *, layer, n_pages, page, past_len, n_new):
    G, HG, D = NSA_KV_HEADS, NSA_GROUP, NSA_HD
    b = pl.program_id(0)
    nb = pl.num_programs(0)
    half_rows = 2 * G * D
    bufs = (cbuf, sbuf)
    sub_per_page = page // CMP_STRIDE

    def page_copy(bb, p, half):
        return pltpu.make_async_copy(
            ct_ref.at[pt_ref[bb, p], layer, pl.ds(half * half_rows, half_rows), :],
            bufs[half].at[:, pl.ds(pl.multiple_of(p * page, page), page)], sem.at[half])

    def start_all(bb, half):
        lax.fori_loop(0, n_pages, lambda p, c: (page_copy(bb, p, half).start(), c)[1], 0)

    def wait_all(bb, half):
        lax.fori_loop(0, n_pages, lambda p, c: (page_copy(bb, p, half).wait(), c)[1], 0)

    @pl.when(b == 0)
    def _():
        start_all(0, 0)
        start_all(0, 1)

    wait_all(b, 0)

    def to_rows(p, c):
        p0 = pl.multiple_of(p * page, page)
        z = _dot_nt(perm_ref[...], cbuf[:, pl.ds(p0, page)].astype(BF))
        r0 = pl.multiple_of(p * sub_per_page, sub_per_page)
        for u in range(CMP_STRIDE):
            zu = z[u * sub_per_page:(u + 1) * sub_per_page]
            xk_ref[pl.ds(r0, sub_per_page), u * LANES:(u + 1) * LANES] = zu[:, 0:G * D]
            xv_ref[pl.ds(r0, sub_per_page), u * LANES:(u + 1) * LANES] = zu[:, G * D:2 * G * D]
        return c

    lax.fori_loop(0, n_pages, to_rows, 0, unroll=8)

    @pl.when(b + 1 < nb)
    def _():
        start_all(b + 1, 0)

    nsub = past_len // CMP_STRIDE
    half_w = G * CMP_HIDDEN
    sh_ref[pl.ds(nsub, 8), :] = jnp.zeros((8, half_w), F32)
    kcv = []
    for kind, (x_ref, w_ref) in enumerate(((xk_ref, wk_ref), (xv_ref, wv_ref))):
        bias = _dot(pos_ref[kind], w_ref[...])
        h_all = _dot(x_ref[...].astype(BF), w_ref[...])
        sh_ref[pl.ds(0, nsub), :] = h_all[:, half_w:] + bias[1:2, half_w:]
        h = h_all[:, :half_w] + bias[0:1, :half_w] + sh_ref[pl.ds(1, nsub), :]
        h = h * (1.0 / (1.0 + jnp.exp(-h)))
        kcv.append(_dot(h.astype(BF), w2_ref[kind]).astype(BF))
    kc, vc = kcv

    rows = G * HG * TPAD
    q = q_ref[0]
    t_rows = lax.broadcasted_iota(jnp.int32, (rows, LANES), 0) % TPAD
    lane_r = lax.broadcasted_iota(jnp.int32, (rows, LANES), 1)
    new_ok = (lane_r <= t_rows) & (lane_r < n_new)
    tile_gh = lambda xs: jnp.concatenate([x for x in xs for _ in range(HG)], axis=0)

    s = _dot_nt(q, kc)
    c_end = lax.broadcasted_iota(jnp.int32, s.shape, 1) * CMP_STRIDE + (CMP_LEN - 1)
    q_pos = past_len + lax.broadcasted_iota(jnp.int32, s.shape, 0) % TPAD
    (e,), den = _masked_softmax_parts([(s, c_end <= q_pos)])
    p = e / den
    o_c = _dot(p.astype(BF), vc)
    psum = jnp.concatenate([functools.reduce(jnp.add, [p[(g * HG + h) * TPAD:(g * HG + h + 1) * TPAD]
                                                       for h in range(HG)]) for g in range(G)], axis=0)
    imp = _dot_hl(psum, cover_ref[...])
    cur = (past_len + lax.broadcasted_iota(jnp.int32, imp.shape, 0) % TPAD) // SEL_LEN
    sel = _select_topk(imp, cur, -(-(past_len + n_new) // SEL_LEN))

    wait_all(b, 1)
    n_past_blocks = past_len // SEL_LEN
    selx = _dot(sel[:, :n_past_blocks].astype(BF), exp_ref[...])
    in_new = jnp.broadcast_to(sel[:, n_past_blocks:n_past_blocks + 1], (G * TPAD, LANES))
    split_g = lambda x: [x[g * TPAD:(g + 1) * TPAD] for g in range(G)]
    s_past = _dot(q, sbuf[0:G * D, :].astype(BF))
    s_new = _dot_nt(q, new_ref[0, 0])
    (e_past, e_new), den = _masked_softmax_parts(
        [(s_past, tile_gh(split_g(selx)) > 0.5), (s_new, (tile_gh(split_g(in_new)) > 0.5) & new_ok)])
    o_t = _dot_nt(sbuf[G * D:2 * G * D, :].astype(BF), e_past.astype(BF))
    o_s = (o_t.T + _dot(e_new.astype(BF), new_ref[0, 1])) / den

    @pl.when(b + 1 < nb)
    def _():
        start_all(b + 1, 1)

    wb = wt_ref.shape[2]
    s_buf = _dot(q, wt_ref[0, 0:G * D, :].astype(BF))
    i = lax.broadcasted_iota(jnp.int32, s_buf.shape, 1)
    t = lax.broadcasted_iota(jnp.int32, s_buf.shape, 0) % TPAD
    s_new = _dot_nt(q, new_ref[0, 2])
    (e_buf, e_new), den = _masked_softmax_parts([(s_buf, wb + t - i <= WINDOW), (s_new, new_ok)])
    o_t = _dot_nt(wt_ref[0, G * D:2 * G * D, :].astype(BF), e_buf.astype(BF))
    o_w = (o_t.T + _dot(e_new.astype(BF), new_ref[0, 3])) / den
    gates = 1.0 / (1.0 + jnp.exp(-gt_ref[0]))
    o_ref[0] = gates[:, 0:1] * o_c + gates[:, 1:2] * o_s + gates[:, 2:3] * o_w


def _sample_compress_weights(cmp_pos, cmp_w1, cmp_w2):
    G, D, S, Hd = NSA_KV_HEADS, NSA_HD, CMP_STRIDE, CMP_HIDDEN
    w1 = cmp_w1.reshape(2, 2, S, D, Hd)
    eye = jnp.eye(G, dtype=w1.dtype)
    wk = jnp.einsum('khudn,gj->kugdhjn', w1, eye).reshape(2, S * G * D, 2 * G * Hd).astype(BF)
    pos = cmp_pos.reshape(2, 2, S, 1, D)
    pos = jnp.broadcast_to(pos, (2, 2, S, G, D)).reshape(2, 2, S * G * D)
    pos = jnp.pad(pos, ((0, 0), (0, 8 - 2), (0, 0))).astype(BF)
    w2 = jnp.einsum('knd,gj->kgnjd', cmp_w2, eye).reshape(2, G * Hd, G * D).astype(BF)
    return wk[0], wk[1], pos, w2


def nsa_sample(page_table, cache_t, layer, q, gi, kv, win_t, cw, DB, DS):
    G, HG, D, H = NSA_KV_HEADS, NSA_GROUP, NSA_HD, NSA_HEADS
    n_pages, page = page_table.shape[1], cache_t.shape[3]
    past_len = n_pages * page
    rows = H * TPAD
    wk, wv, pos, w2 = cw
    heads = lambda a, w: a.reshape(DB, DS, H, w).transpose(0, 2, 1, 3)
    padt = lambda a: jnp.pad(a, ((0, 0), (0, 0), (0, TPAD - DS), (0, 0)))
    qs = padt(heads(q, LANES)).reshape(DB, rows, LANES)
    gts = padt(heads(gi[:, :NSA_G_W], 3)).reshape(DB, rows, 3)
    gts = jnp.pad(gts, ((0, 0), (0, 0), (0, LANES - 3)))
    new = kv[:, 2 * G * D:].reshape(DB, DS, 4, G * D).transpose(0, 2, 1, 3)
    new = jnp.pad(new, ((0, 0), (0, 0), (0, LANES - DS), (0, 0))).astype(BF)
    nsub = past_len // CMP_STRIDE
    sub_per_page = page // CMP_STRIDE
    cover = _cover_matrix(nsub, 2 * LANES)
    tok = np.arange(past_len)[None, :] // SEL_LEN
    expand = jnp.asarray(np.arange(past_len // SEL_LEN)[:, None] == tok, BF)
    r = np.arange(page)
    perm = jnp.asarray((r % sub_per_page)[:, None] * CMP_STRIDE + (r // sub_per_page)[:, None] == r[None, :], BF)
    blk = lambda *s: pl.BlockSpec((1,) + s, lambda b, pt: (b,) + (0,) * len(s))
    full = lambda a: pl.BlockSpec(a.shape, lambda b, pt: (0,) * a.ndim)
    out = pl.pallas_call(
        functools.partial(_nsa_sample_kernel, layer=layer, n_pages=n_pages, page=page, past_len=past_len, n_new=DS),
        grid_spec=pltpu.PrefetchScalarGridSpec(
            num_scalar_prefetch=1,
            grid=(DB,),
            in_specs=[pl.BlockSpec(memory_space=pl.ANY), blk(rows, LANES), blk(rows, LANES),
                      blk(4, LANES, G * D),
                      pl.BlockSpec((None, 1) + win_t.shape[2:], lambda b, pt: (layer, b, 0, 0)),
                      full(perm), full(pos), full(wk), full(wv), full(w2), full(cover), full(expand)],
            out_specs=blk(rows, LANES),
            scratch_shapes=[pltpu.VMEM((2 * G * D, past_len), F32), pltpu.VMEM((2 * G * D, past_len), F32),
                            pltpu.VMEM((nsub, CMP_STRIDE * LANES), F32), pltpu.VMEM((nsub, CMP_STRIDE * LANES), F32),
                            pltpu.VMEM((nsub + 8, G * CMP_HIDDEN), F32), pltpu.SemaphoreType.DMA((2,))]),
        out_shape=jax.ShapeDtypeStruct((DB, rows, LANES), F32),
        compiler_params=_params("arbitrary"),
        name="nsa_sample",
    )(page_table, cache_t, qs, gts, new, win_t, perm, pos, wk, wv, w2, cover, expand)
    out = out.reshape(DB, G, HG, TPAD, G, D)[:, :, :, :DS]
    out = jnp.stack([out[:, g, :, :, g] for g in range(G)], axis=1)
    return out.transpose(0, 3, 1, 2, 4).reshape(DB * DS, H * D).astype(BF)


def _memkv_kernel(x_ref, g_ref, w_ref, kg_ref, seg_ref, o_ref, ob_ref):
    z = _dot(_rms(x_ref[...], g_ref[...]).astype(BF), w_ref[...])
    kw = XA_HEADS * XA_HD
    for j in range(2 * kw // LANES):
        sl = slice(j * LANES, (j + 1) * LANES)
        zc = z[:, sl]
        if j * LANES < kw:
            zc = _head_norm(zc, kg_ref[...], seg_ref[...])
        o_ref[:, sl] = zc
        ob_ref[:, sl] = zc.astype(BF)


def memory_kv(mem, g, w, kg, seg, tm=512):
    M, D = mem.shape
    N = w.shape[1]
    row = lambda wd: pl.BlockSpec((tm, wd), lambda i: (i, 0))
    return pl.pallas_call(
        _memkv_kernel,
        grid=(M // tm,),
        in_specs=[row(D), _full((1, D)), _full(w.shape), _full(kg.shape), _full(seg.shape)],
        out_specs=[row(N), row(N)],
        out_shape=[jax.ShapeDtypeStruct((M, N), F32), jax.ShapeDtypeStruct((M, N), BF)],
        compiler_params=_params("parallel"),
        name="memory_kv",
    )(mem, g.reshape(1, D), w, kg, seg)


def _postmix_kernel(x_ref, a_ref, b_ref, wa_ref, wb_ref, gx_ref, wq_ref, qg_ref, seg_ref, mkv_ref, wo_ref, o_ref,
                    *, rows_per_batch, n_mem):
    x1 = x_ref[...] + _dot(a_ref[...], wa_ref[...]) + _dot(b_ref[...], wb_ref[...])
    xn = _rms(x1, gx_ref[...]).astype(BF)
    zq = _dot(xn, wq_ref[...])
    kw = XA_HEADS * XA_HD
    q = jnp.concatenate([_head_norm(zq[:, j * LANES:(j + 1) * LANES], qg_ref[...], seg_ref[...])
                         for j in range(kw // LANES)], axis=1)
    q = (q * XA_HD ** -0.5).astype(BF)
    tm = x1.shape[0]
    many = rows_per_batch < tm
    mask = None
    if many:
        nk = mkv_ref.shape[1]
        r = lax.broadcasted_iota(jnp.int32, (tm, nk), 0) // rows_per_batch
        c = lax.broadcasted_iota(jnp.int32, (tm, nk), 1) // n_mem
        mask = r == c
    outs = []
    for h in range(XA_HEADS):
        qh = q[:, h * XA_HD:(h + 1) * XA_HD]
        if many:
            s = _dot(qh, mkv_ref[h * XA_HD:(h + 1) * XA_HD, :])
            s = jnp.where(mask, s, NEG)
        else:
            s = _dot_nt(qh, mkv_ref[0, :, h * XA_HD:(h + 1) * XA_HD])
        e = jnp.exp(s - jnp.max(s, axis=1, keepdims=True))
        if many:
            pv = _dot_nt(e.astype(BF), mkv_ref[kw + h * XA_HD:kw + (h + 1) * XA_HD, :])
        else:
            pv = _dot(e.astype(BF), mkv_ref[0, :, kw + h * XA_HD:kw + (h + 1) * XA_HD])
        outs.append(pv / jnp.sum(e, axis=1, keepdims=True))
    o = jnp.concatenate(outs, axis=1).astype(BF)
    o_ref[...] = x1 + _dot(o, wo_ref[...])


def post_mix(x, a, b, wa, wb, gx, wq, qg, seg, mkvb, wo, tm, rows_per_batch):
    M, D = x.shape
    if rows_per_batch >= tm:
        n_mem = mkvb.shape[1]
        per = rows_per_batch // tm
        mspec = pl.BlockSpec((1, n_mem, mkvb.shape[2]), lambda i: (i // per, 0, 0))
    else:
        assert tm == M
        n_mem = mkvb.shape[1] // (M // rows_per_batch)
        mspec = _full(mkvb.shape)
    row = lambda wd: pl.BlockSpec((tm, wd), lambda i: (i, 0))
    return pl.pallas_call(
        functools.partial(_postmix_kernel, rows_per_batch=rows_per_batch, n_mem=n_mem),
        grid=(M // tm,),
        in_specs=[row(D), row(a.shape[1]), row(b.shape[1]), _full(wa.shape), _full(wb.shape), _full((1, D)),
                  _full(wq.shape), _full(qg.shape), _full(seg.shape), mspec, _full(wo.shape)],
        out_specs=row(D),
        out_shape=jax.ShapeDtypeStruct((M, D), F32),
        compiler_params=_params("parallel"),
        name="post_mix",
    )(x, a, b, wa, wb, gx.reshape(1, D), wq, qg, seg, mkvb, wo)


def _mlp_kernel(x_ref, g_ref, w1_ref, w2_ref, o_ref, xn_ref, acc_ref):
    f = pl.program_id(1)

    @pl.when(f == 0)
    def _():
        x = x_ref[...]
        xn_ref[...] = _rms(x, g_ref[...]).astype(BF)
        acc_ref[...] = x

    h = _dot(xn_ref[...], w1_ref[...].astype(BF))
    h = jnp.square(jnp.maximum(h, 0.0)).astype(BF)
    acc_ref[...] += _dot(h, w2_ref[...].astype(BF))

    @pl.when(f == pl.num_programs(1) - 1)
    def _():
        o_ref[...] = acc_ref[...]


def mlp_block(x, g, w1b, w2b, layer, tm, tf=1024):
    M, D = x.shape
    FF = w1b.shape[2]
    return pl.pallas_call(
        _mlp_kernel,
        grid=(M // tm, FF // tf),
        in_specs=[pl.BlockSpec((tm, D), lambda i, f: (i, 0)),
                  pl.BlockSpec((1, D), lambda i, f: (0, 0)),
                  pl.BlockSpec((None, D, tf), lambda i, f: (layer, 0, f)),
                  pl.BlockSpec((None, tf, D), lambda i, f: (layer, f, 0))],
        out_specs=pl.BlockSpec((tm, D), lambda i, f: (i, 0)),
        out_shape=jax.ShapeDtypeStruct((M, D), F32),
        scratch_shapes=[pltpu.VMEM((tm, D), BF), pltpu.VMEM((tm, D), F32)],
        compiler_params=_params("parallel", "arbitrary"),
        name="mlp",
    )(x, g.reshape(1, D), w1b, w2b)


ML_CHUNK = 128
CONV_PAD = 8
ML_PAR = 4


def _dot3(a, b):
    hi = b.astype(BF)
    r1 = b - hi.astype(F32)
    mid = r1.astype(BF)
    lo = (r1 - mid.astype(F32)).astype(BF)
    return _dot(a, hi) + _dot(a, mid) + _dot(a, lo)


def _mlstm_kernel(zqk_ref, zv_ref, zo_ref, gi_ref, cw_ref, cb_ref, gb_ref, hg_ref, tril_ref,
                  c0_ref, n0_ref, m0_ref, cv0_ref,
                  h_out, c_out, n_out, m_out, cv_out,
                  xin_ref, c_ref, n_ref, m_ref, *, n_valid):
    ci = pl.program_id(1)

    @pl.when(ci == 0)
    def _():
        c_ref[...] = c0_ref[...]
        n_ref[...] = n0_ref[...]
        m_ref[...] = m0_ref[...]
        xin_ref[:, pl.ds(0, CONV_PAD), :] = cv0_ref[...]

    for e in range(ML_PAR):
        _mlstm_chunk(zqk_ref.at[e], zv_ref.at[e], zo_ref.at[e], gi_ref.at[e], cw_ref, cb_ref, gb_ref, hg_ref, tril_ref,
                     h_out.at[e], cv_out.at[e], xin_ref.at[e], c_ref.at[e], n_ref.at[e], m_ref.at[e], n_valid)

    @pl.when(ci == pl.num_programs(1) - 1)
    def _():
        c_out[...] = c_ref[...]
        n_out[...] = n_ref[...]
        m_out[...] = m_ref[...]


def _mlstm_chunk(zqk_ref, zv_ref, zo_ref, gi_ref, cw_ref, cb_ref, gb_ref, hg_ref, tril_ref,
                 h_out, cv_out, xin_ref, c_ref, n_ref, m_ref, n_valid):
    L = ML_CHUNK
    xin_ref[pl.ds(CONV_PAD, L), :] = zqk_ref[...]
    y = cb_ref[...]
    for j in range(CONV_W):
        y = y + cw_ref[j:j + 1, :] * xin_ref[pl.ds(CONV_PAD - (CONV_W - 1) + j, L), :]
    qk = y * (1.0 / (1.0 + jnp.exp(-y)))
    hist = xin_ref[pl.ds(n_valid, CONV_PAD), :]
    xin_ref[pl.ds(0, CONV_PAD), :] = hist
    cv_out[...] = hist

    row = lax.broadcasted_iota(jnp.int32, (L, L), 0)
    col = lax.broadcasted_iota(jnp.int32, (L, L), 1)
    valid = row < n_valid
    gl = gi_ref[...] + gb_ref[...]
    for h in range(ML_HEADS):
        hs = slice(h * ML_HD, (h + 1) * ML_HD)
        q = qk[:, hs]
        k = qk[:, ML_W + h * ML_HD:ML_W + (h + 1) * ML_HD] * (ML_HD ** -0.5)
        v = zv_ref[:, hs]
        gi_col = jnp.broadcast_to(gl[:, NSA_G_W + h:NSA_G_W + h + 1], (L, L))
        gf_col = jnp.broadcast_to(gl[:, NSA_G_W + ML_HEADS + h:NSA_G_W + ML_HEADS + h + 1], (L, L))
        li = jnp.where(valid, gi_col, NEG)
        lf = jnp.where(valid, jnp.minimum(gf_col, 0.0) - jnp.log(1.0 + jnp.exp(-jnp.abs(gf_col))), 0.0)
        b = _dot3(tril_ref[...], lf)
        m_prev = m_ref[h:h + 1, :]
        dm = jnp.where(col <= row, b - (b - li).T, NEG)
        inter = b + m_prev
        m_t = jnp.maximum(inter, jnp.max(dm, axis=1, keepdims=True))
        w = jnp.exp(dm - m_t)
        a = jnp.exp(inter - m_t)
        qb, kb, vb = q.astype(BF), k.astype(BF), v.astype(BF)
        wqk = w * _dot_nt(qb, kb)
        c_old = c_ref[h]
        n_old = n_ref[h:h + 1, :]
        num = a * _dot_nt(qb, c_old.astype(BF)) + _dot(wqk.astype(BF), vb)
        den = a * jnp.sum(q * n_old, axis=1, keepdims=True) + jnp.sum(wqk, axis=1, keepdims=True)
        hh = num / jnp.maximum(jnp.abs(den), jnp.exp(-m_t))
        m_new = m_t[L - 1:L, :]
        b_last = b[L - 1:L, :]
        wk = jnp.exp(b_last - b + li - m_new) * k
        decay = jnp.exp(b_last + m_prev - m_new)
        c_ref[h] = decay * c_old + _dot(v.T.astype(BF), wk.astype(BF))
        n_ref[h:h + 1, :] = decay * n_old + jnp.sum(wk, axis=0, keepdims=True)
        m_ref[h:h + 1, :] = m_new
        hn = hh * lax.rsqrt(jnp.mean(hh * hh, axis=1, keepdims=True) + EPS) * hg_ref[...]
        h_out[:, hs] = (hn * (1.0 / (1.0 + jnp.exp(-zo_ref[:, hs])))).astype(BF)


def mlstm(zqk, zv, zo, gi, conv_w, conv_b, gate_b, hn_g, state, nb, n_valid):
    M = zqk.shape[0]
    T = M // nb
    nch = T // ML_CHUNK
    assert n_valid == ML_CHUNK or nch == 1
    assert nb % ML_PAR == 0
    C0, n0, m0, cv0 = state
    m0 = jnp.broadcast_to(m0[:, :, None], (nb, ML_HEADS, LANES))
    cv0 = jnp.pad(cv0, ((0, 0), (CONV_PAD - (CONV_W - 1), 0), (0, 0)))
    gb = jnp.zeros((1, LANES), F32).at[0, NSA_G_W:NSA_G_W + 2 * ML_HEADS].set(gate_b)
    tril = jnp.asarray(np.tril(np.ones((ML_CHUNK, ML_CHUNK))), BF)
    zqk, zv, zo, gi = (a.reshape(nb, T, a.shape[1]) for a in (zqk, zv, zo, gi))
    row = lambda wd: pl.BlockSpec((ML_PAR, ML_CHUNK, wd), lambda b, c: (b, c, 0))
    st = lambda *shape: pl.BlockSpec((ML_PAR,) + shape, lambda b, c: (b,) + (0,) * len(shape))
    h, Cn, nn, mn, cvn = pl.pallas_call(
        functools.partial(_mlstm_kernel, n_valid=n_valid),
        grid=(nb // ML_PAR, nch),
        in_specs=[row(2 * ML_W), row(ML_W), row(ML_W), row(LANES),
                  _full(conv_w.shape), _full((1, 2 * ML_W)), _full((1, LANES)), _full((1, ML_HD)), _full(tril.shape),
                  st(ML_HEADS, ML_HD, ML_HD), st(ML_HEADS, ML_HD), st(ML_HEADS, LANES), st(CONV_PAD, 2 * ML_W)],
        out_specs=[row(ML_W), st(ML_HEADS, ML_HD, ML_HD), st(ML_HEADS, ML_HD), st(ML_HEADS, LANES),
                   st(CONV_PAD, 2 * ML_W)],
        out_shape=[jax.ShapeDtypeStruct((nb, T, ML_W), BF),
                   jax.ShapeDtypeStruct((nb, ML_HEADS, ML_HD, ML_HD), F32),
                   jax.ShapeDtypeStruct((nb, ML_HEADS, ML_HD), F32),
                   jax.ShapeDtypeStruct((nb, ML_HEADS, LANES), F32),
                   jax.ShapeDtypeStruct((nb, CONV_PAD, 2 * ML_W), F32)],
        scratch_shapes=[pltpu.VMEM((ML_PAR, CONV_PAD + ML_CHUNK, 2 * ML_W), F32),
                        pltpu.VMEM((ML_PAR, ML_HEADS, ML_HD, ML_HD), F32),
                        pltpu.VMEM((ML_PAR, ML_HEADS, ML_HD), F32),
                        pltpu.VMEM((ML_PAR, ML_HEADS, LANES), F32)],
        compiler_params=_params("parallel", "arbitrary"),
        name="mlstm",
    )(zqk, zv, zo, gi, conv_w, conv_b.reshape(1, -1), gb, hn_g.reshape(1, -1), tril, C0, n0, m0, cv0)
    return h.reshape(M, ML_W), (Cn, nn, mn[:, :, 0], cvn[:, CONV_PAD - (CONV_W - 1):])


def _rope_tables(pos, rows):
    half = ROPE_DIMS // 2
    freqs = ROPE_THETA ** (-jnp.arange(half, dtype=F32) / half)
    ang = pos.astype(F32)[:, None] * freqs
    cos, sin = jnp.cos(ang), jnp.sin(ang)
    n = pos.shape[0]
    one, zero = jnp.ones((n, NSA_HD - ROPE_DIMS), F32), jnp.zeros((n, NSA_HD - ROPE_DIMS), F32)
    zh = jnp.zeros((n, half), F32)
    c = jnp.concatenate([cos, cos, one], axis=1)
    sa = jnp.concatenate([-sin, zh, zero], axis=1)
    sb = jnp.concatenate([zh, sin, zero], axis=1)
    tile = lambda t: jnp.tile(t, (rows // n, LANES // NSA_HD))
    return tile(c), tile(sa), tile(sb)


def _seg_matrix():
    i = np.arange(LANES)
    return jnp.asarray((i[:, None] // NSA_HD == i[None, :] // NSA_HD) / NSA_HD, BF)


def _cover_matrix(nc_rows, lanes):
    ci = np.arange(nc_rows)[:, None] * CMP_STRIDE
    sj = np.arange(lanes)[None, :]
    return jnp.asarray((ci < (sj + 1) * SEL_LEN) & (ci + CMP_LEN > sj * SEL_LEN), BF)


def _expand_matrix(n_chunks):
    j = np.arange(LANES)[None, :, None]
    c = np.arange(n_chunks)[:, None, None]
    s = np.arange(KCH)[None, None, :]
    return jnp.asarray(j == c * (KCH // SEL_LEN) + s // SEL_LEN, BF)


def _in_weights(w):
    zq, zkv, zg, zqk, zv, zo, zif = jnp.split(w, np.cumsum(COL_SPLITS)[:-1].tolist(), axis=1)
    pad = jnp.zeros((w.shape[0], LANES - NSA_G_W - 2 * ML_HEADS), w.dtype)
    return tuple(a.astype(BF) for a in (zq, zkv, zqk, zv, zo, jnp.concatenate([zg, zif, pad], axis=1)))


def _compress_weights(cmp_pos, cmp_w1, cmp_w2):
    G, D, S = NSA_KV_HEADS, NSA_HD, CMP_STRIDE
    pos = jnp.tile(cmp_pos[:, :, None, :], (1, 1, G, 1))
    pos = pos.transpose(1, 0, 2, 3).reshape(CMP_LEN, 2 * G * D)
    w1 = cmp_w1.reshape(2, CMP_LEN, D, CMP_HIDDEN)
    z = jnp.zeros_like(w1)
    w1g = jnp.stack([jnp.concatenate([w1, z], axis=2), jnp.concatenate([z, w1], axis=2)], axis=1)
    return (pos[:S], pos[S:], w1g[:, :, :S].astype(BF), w1g[:, :, S:].astype(BF), cmp_w2.astype(BF))


def kernel(x_prompt, x_sample, mem_prompt, cache_nsa_kv, cache_win_kv, cache_mem_kv, state_mlstm_C, state_mlstm_n, state_mlstm_m, state_conv, page_table, norm_mix_g, w_in, nsa_qn_g, nsa_kn_g, cmp_pos, cmp_w1, cmp_w2, ml_conv_w, ml_conv_b, ml_gate_b, ml_hn_g, w_out, norm_xa_g, norm_mem_g, xa_wq, xa_wkv, xa_qn_g, xa_kn_g, xa_wo, norm_mlp_g, mlp_w1, mlp_w2):
    B, T, D = x_prompt.shape
    DB, DS, _ = x_sample.shape
    depth = w_in.shape[0]
    n_mem = mem_prompt.shape[1]
    P = page_table.shape[1] * cache_nsa_kv.shape[1]
    wkeep = min(WINDOW, T)
    G, HD = NSA_KV_HEADS, NSA_HD
    MP, MS = B * T, DB * DS
    tm_p, tm_s = 512, MS

    pos_p = jnp.arange(T, dtype=jnp.int32)
    pos_s = P + jnp.arange(DS, dtype=jnp.int32)
    tabs_p = _rope_tables(pos_p, T)
    tabs_s = _rope_tables(pos_s, MS)
    seg = _seg_matrix()
    cover = _cover_matrix(T // CMP_STRIDE, LANES).T
    expand = _expand_matrix(T // KCH)
    tile2 = lambda v: jnp.tile(v.reshape(-1, HD), (1, LANES // HD))
    pad_chunk = lambda a: jnp.pad(a.reshape(DB, DS, -1), ((0, 0), (0, ML_CHUNK - DS), (0, 0))).reshape(DB * ML_CHUNK, -1)

    n_pool, page = cache_nsa_kv.shape[:2]
    cache_t = cache_nsa_kv.transpose(0, 2, 3, 4, 5, 1).reshape(n_pool, depth, 4 * G * HD, page)
    win_t = cache_win_kv.transpose(0, 1, 3, 4, 5, 2).reshape(depth, DB, 2 * G * HD, cache_win_kv.shape[2])

    xp = x_prompt.reshape(MP, D)
    xs = x_sample.reshape(MS, D)
    mem = mem_prompt.reshape(B * n_mem, D)
    nsa_p, nsa_s, win_p, win_s = [], [], [], []
    C_p, C_s, n_p, n_s, m_p, m_s = [], [], [], [], [], []
    cv_p, cv_s, mem_p = [], [], []
    for l in range(depth):
        w_l = _in_weights(w_in[l])
        qg, kg = tile2(nsa_qn_g[l]), tile2(nsa_kn_g[l])
        cw = _compress_weights(cmp_pos[l], cmp_w1[l], cmp_w2[l])
        wa, wb = w_out[l, :NSA_Q_W].astype(BF), w_out[l, NSA_Q_W:].astype(BF)
        xq, xo = xa_wq[l].astype(BF), xa_wo[l].astype(BF)
        xqg = tile2(xa_qn_g[l])
        w1b, w2b = mlp_w1, mlp_w2

        q, kvc, kvw, kb, va, zqk, zv, zo, gi = in_proj(xp, norm_mix_g[l], w_l, tabs_p, qg, kg, seg, tm_p)
        kcv = compress_prompt(kvc, cw, B, T)
        a_p = nsa_prompt(q, gi, kb, va, kcv, cover, expand, B, T)
        nsa_p.append(kvc.reshape(B, T, 4, G, HD))
        win_p.append(kvw.reshape(B, T, 2, G, HD)[:, T - wkeep:])
        st0 = (jnp.zeros((B, ML_HEADS, ML_HD, ML_HD), F32), jnp.zeros((B, ML_HEADS, ML_HD), F32),
               jnp.zeros((B, ML_HEADS), F32), jnp.zeros((B, CONV_W - 1, 2 * ML_W), F32))
        b_p, (Cn, nn, mn, buf_p) = mlstm(zqk, zv, zo, gi, ml_conv_w[l], ml_conv_b[l], ml_gate_b[l], ml_hn_g[l],
                                         st0, B, ML_CHUNK)
        C_p.append(Cn); n_p.append(nn); m_p.append(mn); cv_p.append(buf_p)
        mkv, mkvb = memory_kv(mem, norm_mem_g[l], xa_wkv[l].astype(BF), tile2(xa_kn_g[l]), seg)
        mem_p.append(mkv.reshape(B, n_mem, 2, XA_HEADS, XA_HD))
        xp = post_mix(xp, a_p, b_p, wa, wb, norm_xa_g[l], xq, xqg, seg, mkvb.reshape(B, n_mem, -1), xo, 2 * tm_p, T)
        xp = mlp_block(xp, norm_mlp_g[l], w1b, w2b, l, tm=1024)

        q, kvc, kvw, _, _, zqk, zv, zo, gi = in_proj(xs, norm_mix_g[l], w_l, tabs_s, qg, kg, seg, tm_s)
        kv = jnp.concatenate([kvc, kvw], axis=1)
        kvp = kv.reshape(DB, DS, 3, 2, G, HD)
        a_s = nsa_sample(page_table, cache_t, l, q, gi, kv, win_t,
                         _sample_compress_weights(cmp_pos[l], cmp_w1[l], cmp_w2[l]), DB, DS)
        nsa_s.append(kvp[:, :, :2].reshape(DB, DS, 4, G, HD))
        win_s.append(kvp[:, :, 2])
        st = (state_mlstm_C[l], state_mlstm_n[l], state_mlstm_m[l], state_conv[l])
        b_s, (Cn, nn, mn, buf_s) = mlstm(*(pad_chunk(a) for a in (zqk, zv, zo, gi)), ml_conv_w[l], ml_conv_b[l],
                                         ml_gate_b[l], ml_hn_g[l], st, DB, DS)
        b_s = b_s.reshape(DB, ML_CHUNK, ML_W)[:, :DS].reshape(MS, ML_W)
        C_s.append(Cn); n_s.append(nn); m_s.append(mn); cv_s.append(buf_s)
        mkvb_s = cache_mem_kv[l].transpose(2, 3, 4, 0, 1).reshape(2 * XA_HEADS * XA_HD, DB * n_mem).astype(BF)
        xs = post_mix(xs, a_s, b_s, wa, wb, norm_xa_g[l], xq, xqg, seg, mkvb_s, xo, tm_s, DS)
        xs = mlp_block(xs, norm_mlp_g[l], w1b, w2b, l, tm=MS)
    return (xp.reshape(B, T, D), xs.reshape(DB, DS, D),
            jnp.stack(nsa_p, axis=2), jnp.stack(nsa_s, axis=2),
            jnp.stack(win_p),
            jnp.concatenate([cache_win_kv, jnp.stack(win_s)], axis=2)[:, :, -cache_win_kv.shape[2]:],
            jnp.stack(C_p), jnp.stack(C_s),
            jnp.stack(n_p), jnp.stack(n_s),
            jnp.stack(m_p), jnp.stack(m_s),
            jnp.stack(cv_p), jnp.stack(cv_s),
            jnp.stack(mem_p))
```
